```python
import jax, jax.numpy as jnp
from jax import lax
import numpy as np

D_MODEL = 1024
BATCH = 8
SEQ = 2048
DEPTH = 1
DEC_BATCH = 8
DEC_SEQ = 16
PAST_LEN = 2048

CHUNK = 64
D_MIX = D_MODEL
GLA_HEADS = 4
GLA_DV = (D_MIX // 2) // GLA_HEADS
GLA_DK = GLA_DV // 2
GLA_RANK = 16
GLA_TAU = 16.0
M_HEADS = 4
M_DH = (D_MIX // 2) // M_HEADS
D_FF = 2816
FFN_CONV = 3
EPS = 1e-6
GLA_QK = GLA_HEADS * GLA_DK
GLA_V = GLA_HEADS * GLA_DV
M_W = M_HEADS * M_DH
D_IN = 2 * GLA_QK + 2 * GLA_V + GLA_RANK + 4 * M_W + 2 * M_HEADS

kernel_name = "gla_mlstm_parallel_convffn_stream_step"


def _rmsnorm(x, g):
    xf = x.astype(jnp.float32)
    y = xf * lax.rsqrt(jnp.mean(xf * xf, axis=-1, keepdims=True) + EPS) * g.astype(jnp.float32)
    return y.astype(x.dtype)


def _head_rmsnorm(t, g):
    return t * lax.rsqrt(jnp.mean(t * t, axis=-1, keepdims=True) + EPS) * g.astype(jnp.float32)


def _split_proj(p):
    sizes = (GLA_QK, GLA_QK, GLA_V, GLA_V, GLA_RANK, M_W, M_W, M_W, M_W, M_HEADS, M_HEADS)
    idx = []
    s = 0
    for z in sizes[:-1]:
        s += z
        idx.append(s)
    return jnp.split(p, idx, axis=-1)


def _to_chunks(t, c):
    B, L = t.shape[0], t.shape[1]
    t = t.reshape((B, L // c, c) + t.shape[2:])
    return jnp.moveaxis(t, 1, 0)


def _from_chunks(t):
    t = jnp.moveaxis(t, 0, 1)
    return t.reshape((t.shape[0], t.shape[1] * t.shape[2]) + t.shape[3:])


def _gla(q, k, v, log_a, S0):
    L = q.shape[1]
    c = min(CHUNK, L)
    causal = jnp.tril(jnp.ones((c, c), dtype=bool))

    def step(S, inp):
        qc, kc, vc, ac = inp
        b = jnp.cumsum(ac, axis=1)
        inter = jnp.einsum('bthk,bhkv->bthv', qc * jnp.exp(b), S)
        diff = b[:, :, None] - b[:, None, :]
        diff = jnp.where(causal[None, :, :, None, None], diff, -jnp.inf)
        scores = jnp.einsum('bthk,bshk,btshk->bhts', qc, kc, jnp.exp(diff))
        intra = jnp.einsum('bhts,bshv->bthv', scores, vc)
        b_last = b[:, -1]
        k_dec = kc * jnp.exp(b_last[:, None] - b)
        S_new = jnp.exp(b_last)[..., None] * S + jnp.einsum('bshk,bshv->bhkv', k_dec, vc)
        return S_new, inter + intra

    xs = (_to_chunks(q, c), _to_chunks(k, c), _to_chunks(v, c), _to_chunks(log_a, c))
    S_fin, o = lax.scan(step, S0, xs)
    return _from_chunks(o), S_fin


def _mlstm(q, k, v, i_pre, log_f, C0, n0, m0):
    L = q.shape[1]
    c = min(CHUNK, L)
    causal = jnp.tril(jnp.ones((c, c), dtype=bool))

    def step(carry, inp):
        C, n, m = carry
        qc, kc, vc, ic, fc = inp
        F = jnp.moveaxis(jnp.cumsum(fc, axis=1), 1, 2)
        ih = jnp.moveaxis(ic, 1, 2)
        D = F[..., :, None] - F[..., None, :] + ih[..., None, :]
        D = jnp.where(causal, D, -jnp.inf)
        g_inter = F + m[..., None]
        m_t = jnp.maximum(g_inter, jnp.max(D, axis=-1))
        w_inter = jnp.exp(g_inter - m_t)
        W = jnp.exp(D - m_t[..., None])
        qk = jnp.einsum('bthd,bshd->bhts', qc, kc) * W
        w_bt = jnp.moveaxis(w_inter, 1, 2)
        num = (jnp.einsum('bhts,bshv->bthv', qk, vc)
               + w_bt[..., None] * jnp.einsum('bthd,bhdv->bthv', qc, C))
        den = jnp.sum(qk, axis=-1) + w_inter * jnp.einsum('bthd,bhd->bht', qc, n)
        den = jnp.maximum(jnp.abs(den), jnp.exp(-m_t))
        h = num / jnp.moveaxis(den, 1, 2)[..., None]
        m_c = m_t[..., -1]
        decay = jnp.exp(F[..., -1] + m - m_c)
        ws = jnp.exp(F[..., -1:] - F + ih - m_c[..., None])
        C_new = decay[..., None, None] * C + jnp.einsum('bhs,bshd,bshv->bhdv', ws, kc, vc)
        n_new = decay[..., None] * n + jnp.einsum('bhs,bshd->bhd', ws, kc)
        return (C_new, n_new, m_c), h

    xs = (_to_chunks(q, c), _to_chunks(k, c), _to_chunks(v, c),
          _to_chunks(i_pre, c), _to_chunks(log_f, c))
    (C_f, n_f, m_f), h = lax.scan(step, (C0, n0, m0), xs)
    return _from_chunks(h), C_f, n_f, m_f


def _conv_ffn(h, conv_st, w_g, w_u, conv_w, conv_b, w_d):
    L = h.shape[1]
    g = h @ w_g
    u = h @ w_u
    gp = jnp.concatenate([conv_st.astype(g.dtype), g], axis=1)
    gc = conv_b + sum(conv_w[j] * gp[:, j:j + L] for j in range(FFN_CONV))
    y = (jax.nn.gelu(gc) * u) @ w_d
    return y, gp[:, -(FFN_CONV - 1):]


def _layer(x, S_g, C_m, n_m, m_m, conv_st, norm1, w_in, w_gla_a, b_gla_a, gla_norm,
           b_m_i, b_m_f, m_norm, w_out, norm2, w_g, w_u, conv_w, conv_b, w_d):
    B, L, _ = x.shape
    f32 = jnp.float32
    h = _rmsnorm(x, norm1)
    gq, gk, gv, gr, glr, mq, mk, mv, mo, mi, mf = _split_proj(h @ w_in)

    def heads(t, H, d):
        return t.astype(f32).reshape(B, L, H, d)

    log_a = jax.nn.log_sigmoid((glr @ w_gla_a + b_gla_a).astype(f32)) / GLA_TAU
    o_g, S_new = _gla(heads(gq, GLA_HEADS, GLA_DK) * (GLA_DK ** -0.5),
                      heads(gk, GLA_HEADS, GLA_DK), heads(gv, GLA_HEADS, GLA_DV),
                      heads(log_a, GLA_HEADS, GLA_DK), S_g.astype(f32))
    o_g = _head_rmsnorm(o_g, gla_norm) * jax.nn.silu(heads(gr, GLA_HEADS, GLA_DV))

    i_pre = mi.astype(f32) + b_m_i.astype(f32)
    log_f = jax.nn.log_sigmoid(mf.astype(f32) + b_m_f.astype(f32))
    h_m, C_new, n_new, m_new = _mlstm(heads(mq, M_HEADS, M_DH),
                                      heads(mk, M_HEADS, M_DH) * (M_DH ** -0.5),
                                      heads(mv, M_HEADS, M_DH), i_pre, log_f,
                                      C_m.astype(f32), n_m.astype(f32), m_m.astype(f32))
    h_m = _head_rmsnorm(jax.nn.sigmoid(heads(mo, M_HEADS, M_DH)) * h_m, m_norm)

    mix = jnp.concatenate([o_g.reshape(B, L, GLA_V), h_m.reshape(B, L, M_W)], axis=-1)
    x = x + mix.astype(x.dtype) @ w_out
    y, conv_new = _conv_ffn(_rmsnorm(x, norm2), conv_st, w_g, w_u, conv_w, conv_b, w_d)
    x = x + y
    dt = x.dtype
    return x, (S_new.astype(dt), C_new.astype(dt), n_new.astype(dt), m_new.astype(dt),
               conv_new.astype(dt))


def setup_inputs(seed: int = 0) -> dict:
    key = jax.random.key(seed)
    ks = jax.random.split(key, 32)
    nrm = jax.random.normal
    f32 = jnp.float32
    Dp = DEPTH
    return {
        "x_prompt": nrm(ks[0], (BATCH, SEQ, D_MODEL), f32),
        "x_sample": nrm(ks[1], (DEC_BATCH, DEC_SEQ, D_MODEL), f32),
        "state_gla": 0.5 * nrm(ks[2], (Dp, DEC_BATCH, GLA_HEADS, GLA_DK, GLA_DV), f32),
        "state_mlstm_C": 0.5 * nrm(ks[3], (Dp, DEC_BATCH, M_HEADS, M_DH, M_DH), f32),
        "state_mlstm_n": 0.5 * nrm(ks[4], (Dp, DEC_BATCH, M_HEADS, M_DH), f32),
        "state_mlstm_m": nrm(ks[5], (Dp, DEC_BATCH, M_HEADS), f32),
        "cache_ffn_conv": nrm(ks[6], (Dp, DEC_BATCH, FFN_CONV - 1, D_FF), f32),
        "norm1": 1.0 + 0.02 * nrm(ks[7], (Dp, D_MODEL), f32),
        "w_in": nrm(ks[8], (Dp, D_MODEL, D_IN), f32) * D_MODEL ** -0.5,
        "w_gla_a": nrm(ks[9], (Dp, GLA_RANK, GLA_QK), f32) * GLA_RANK ** -0.5,
        "b_gla_a": 0.01 * nrm(ks[10], (Dp, GLA_QK), f32),
        "gla_norm": 1.0 + 0.02 * nrm(ks[11], (Dp, GLA_DV), f32),
        "b_m_i": 0.1 * nrm(ks[12], (Dp, M_HEADS), f32),
        "b_m_f": jnp.linspace(3.0, 6.0, M_HEADS, dtype=f32)[None] + 0.1 * nrm(ks[13], (Dp, M_HEADS), f32),
        "m_norm": 1.0 + 0.02 * nrm(ks[14], (Dp, M_DH), f32),
        "w_out": nrm(ks[15], (Dp, D_MIX, D_MODEL), f32) * D_MIX ** -0.5,
        "norm2": 1.0 + 0.02 * nrm(ks[16], (Dp, D_MODEL), f32),
        "w_g": nrm(ks[17], (Dp, D_MODEL, D_FF), f32) * D_MODEL ** -0.5,
        "w_u": nrm(ks[18], (Dp, D_MODEL, D_FF), f32) * D_MODEL ** -0.5,
        "conv_w": nrm(ks[19], (Dp, FFN_CONV, D_FF), f32) * FFN_CONV ** -0.5,
        "conv_b": 0.01 * nrm(ks[20], (Dp, D_FF), f32),
        "w_d": nrm(ks[21], (Dp, D_FF, D_MODEL), f32) * D_FF ** -0.5,
        "norm_f": 1.0 + 0.02 * nrm(ks[22], (D_MODEL,), f32),
    }


def reference(x_prompt, x_sample, state_gla, state_mlstm_C, state_mlstm_n, state_mlstm_m,
              cache_ffn_conv, norm1, w_in, w_gla_a, b_gla_a, gla_norm, b_m_i, b_m_f, m_norm,
              w_out, norm2, w_g, w_u, conv_w, conv_b, w_d, norm_f):
    xp = x_prompt
    xs = x_sample
    Bp = x_prompt.shape[0]
    dt = x_prompt.dtype
    p_st = []
    s_st = []
    for l in range(DEPTH):
        w = (norm1[l], w_in[l], w_gla_a[l], b_gla_a[l], gla_norm[l], b_m_i[l], b_m_f[l],
             m_norm[l], w_out[l], norm2[l], w_g[l], w_u[l], conv_w[l], conv_b[l], w_d[l])
        xp, st_p = _layer(xp,
                          jnp.zeros((Bp, GLA_HEADS, GLA_DK, GLA_DV), dt),
                          jnp.zeros((Bp, M_HEADS, M_DH, M_DH), dt),
                          jnp.zeros((Bp, M_HEADS, M_DH), dt),
                          jnp.zeros((Bp, M_HEADS), dt),
                          jnp.zeros((Bp, FFN_CONV - 1, D_FF), dt), *w)
        xs, st_s = _layer(xs, state_gla[l], state_mlstm_C[l], state_mlstm_n[l],
                          state_mlstm_m[l], cache_ffn_conv[l], *w)
        p_st.append(st_p)
        s_st.append(st_s)
    y_prompt = _rmsnorm(xp, norm_f)
    y_sample = _rmsnorm(xs, norm_f)
    p_gla = jnp.stack([s[0] for s in p_st])
    p_mC = jnp.stack([s[1] for s in p_st])
    p_mn = jnp.stack([s[2] for s in p_st])
    p_mm = jnp.stack([s[3] for s in p_st])
    p_conv = jnp.stack([s[4] for s in p_st])
    s_gla = jnp.stack([s[0] for s in s_st])
    s_mC = jnp.stack([s[1] for s in s_st])
    s_mn = jnp.stack([s[2] for s in s_st])
    s_mm = jnp.stack([s[3] for s in s_st])
    s_conv = jnp.stack([s[4] for s in s_st])
    return (y_prompt, y_sample, p_gla, p_mC, p_mn, p_mm, p_conv, s_gla, s_mC, s_mn, s_mm, s_conv)
```

```python
import functools

import numpy as np
import jax
import jax.numpy as jnp
from jax import lax
from jax.experimental import pallas as pl
from jax.experimental.pallas import tpu as pltpu

F32 = jnp.float32
BF16 = jnp.bfloat16

EPS = 1e-6
CHUNK = 64
GLA_HEADS = 4
GLA_DK = 64
GLA_DV = 128
GLA_RANK = 16
GLA_TAU = 16.0
M_HEADS = 4
M_DH = 128
FFN_CONV = 3
LANES = 128

GLA_QK = GLA_HEADS * GLA_DK
GLA_V = GLA_HEADS * GLA_DV
M_W = M_HEADS * M_DH
OFF_GQ = 0
OFF_GK = OFF_GQ + GLA_QK
OFF_GV = OFF_GK + GLA_QK
OFF_GR = OFF_GV + GLA_V
OFF_MQ = OFF_GR + GLA_V
OFF_MK = OFF_MQ + M_W
OFF_MV = OFF_MK + M_W
OFF_MO = OFF_MV + M_W
OFF_SMALL = OFF_MO + M_W
P_COLS = OFF_SMALL + LANES
SM_I = GLA_RANK
SM_F = GLA_RANK + M_HEADS

VMEM_LIMIT = 56 * 1024 * 1024


def _dot(a, b):
    return jnp.dot(a, b, preferred_element_type=F32)


def _dot_nt(a, b):
    return lax.dot_general(a, b, (((1,), (1,)), ((), ())), preferred_element_type=F32)


def _dot_tn(a, b):
    return lax.dot_general(a, b, (((0,), (0,)), ((), ())), preferred_element_type=F32)


def _split_hilo(x):
    hi = x.astype(BF16)
    lo = (x - hi.astype(F32)).astype(BF16)
    return hi, lo


def _log_sigmoid(x):
    return jnp.minimum(x, 0.0) - jnp.log1p(jnp.exp(-jnp.abs(x)))


def _rms(x, g):
    return x * lax.rsqrt(jnp.mean(x * x, axis=-1, keepdims=True) + EPS) * g


def _inproj_kernel(x_ref, n1_ref, w_ref, p_ref):
    h = _rms(x_ref[...], n1_ref[...]).astype(BF16)
    p_ref[...] = _dot(h, w_ref[...])


def _inproj(x2d, norm1, w_in_r, rows_tile):
    rows, d = x2d.shape
    return pl.pallas_call(
        _inproj_kernel,
        grid=(rows // rows_tile,),
        in_specs=[
            pl.BlockSpec((rows_tile, d), lambda i: (i, 0)),
            pl.BlockSpec((1, d), lambda i: (0, 0)),
            pl.BlockSpec((d, P_COLS), lambda i: (0, 0)),
        ],
        out_specs=pl.BlockSpec((rows_tile, P_COLS), lambda i: (i, 0)),
        out_shape=jax.ShapeDtypeStruct((rows, P_COLS), F32),
        compiler_params=pltpu.CompilerParams(
            dimension_semantics=("arbitrary",), vmem_limit_bytes=VMEM_LIMIT),
        name="inproj",
    )(x2d, norm1, w_in_r)


def _mixer_consts(c):
    nlev = int(np.log2(c))
    assert 2 ** nlev == c
    t = np.arange(c)[:, None]
    u = np.arange(c)[None, :]
    blocks = [(u <= t), (u > t)]
    for l in range(1, nlev + 1):
        bsz, half = 2 ** l, 2 ** (l - 1)
        mid = (t // bsz) * bsz + half
        right = (t % bsz) >= half
        blocks.append(np.where(right, (u >= mid) & (u <= t), (u > t) & (u < mid)))
    a = np.concatenate(blocks, axis=0).astype(np.float32)
    a2 = np.concatenate([a, a], axis=1)
    x = np.bitwise_xor(t, u)
    lv = np.where(u > t, -1, np.where(x == 0, 0, np.floor(np.log2(np.maximum(x, 1))) + 1))
    e = np.zeros((LANES, 2 * M_HEADS * LANES), np.float32)
    for h in range(M_HEADS):
        e[SM_I + h, h * LANES:(h + 1) * LANES] = 1.0
        e[SM_F + h, (M_HEADS + h) * LANES:(M_HEADS + h + 1) * LANES] = 1.0
    e2 = np.concatenate([e, e], axis=0)
    return (jnp.asarray(a2, BF16), jnp.asarray(lv, jnp.int32), jnp.asarray(e2, BF16), nlev)


def _mixer_kernel(p_ref, wa_ref, ba_ref, gn_ref, bsm_ref, mn_ref, a2_ref, lv_ref, e2_ref,
                  s0_ref, c0_ref, m0_ref,
                  mix_ref, s_out, c_out, m_out,
                  s_sc, c_sc, m_sc, *, c, nlev):
    j = pl.program_id(1)

    @pl.when(j == 0)
    def _():
        s_sc[...] = s0_ref[0]
        c_sc[...] = c0_ref[0]
        m_sc[...] = m0_ref[0]

    small = p_ref[:, OFF_SMALL:OFF_SMALL + LANES]

    xa = _dot(small.astype(BF16), wa_ref[...]) + ba_ref[...]
    la = _log_sigmoid(xa) * (1.0 / GLA_TAU)
    la_hi, la_lo = _split_hilo(la)
    z = _dot(a2_ref[...], jnp.concatenate([la_hi, la_lo], axis=0))
    ez = jnp.exp(z)
    eb = ez[0:c]
    ek = ez[c:2 * c]
    dec = eb[c - 1:c, :]
    q = p_ref[:, OFF_GQ:OFF_GQ + GLA_QK] * (GLA_DK ** -0.5)
    k = p_ref[:, OFF_GK:OFF_GK + GLA_QK]
    q_in = (q * eb).astype(BF16)
    k_dec = (k * ek).astype(BF16)
    q_lv = [q.astype(BF16)] + [(q * ez[(2 + l) * c:(3 + l) * c]).astype(BF16) for l in range(nlev)]
    k_lv = [k.astype(BF16)] + [(k * ez[(2 + l) * c:(3 + l) * c]).astype(BF16) for l in range(nlev)]
    lv = lv_ref[...]
    for h in range(GLA_HEADS):
        sl = slice(h * GLA_DK, (h + 1) * GLA_DK)
        sc = jnp.zeros((c, c), F32)
        for l in range(nlev + 1):
            sc = jnp.where(lv == l, _dot_nt(q_lv[l][:, sl], k_lv[l][:, sl]), sc)
        vb = p_ref[:, OFF_GV + h * GLA_DV:OFF_GV + (h + 1) * GLA_DV].astype(BF16)
        st = s_sc[:, sl]
        o = _dot_nt(q_in[:, sl], st.astype(BF16)) + _dot(sc.astype(BF16), vb)
        s_sc[:, sl] = dec[:, sl] * st + _dot_tn(vb, k_dec[:, sl])
        r = p_ref[:, OFF_GR + h * GLA_DV:OFF_GR + (h + 1) * GLA_DV]
        mix_ref[:, h * GLA_DV:(h + 1) * GLA_DV] = _rms(o, gn_ref[...]) * (r * jax.nn.sigmoid(r))

    lane = lax.broadcasted_iota(jnp.int32, (c, LANES), 1)
    row = lax.broadcasted_iota(jnp.int32, (c, LANES), 0)
    is_f = (lane >= SM_F) & (lane < SM_F + M_HEADS)
    x = small + bsm_ref[...]
    x = jnp.where(is_f, _log_sigmoid(x), x)
    x_hi, x_lo = _split_hilo(x)
    xcum = _dot(a2_ref[0:c, :], jnp.concatenate([x_hi, x_lo], axis=0))
    xc = jnp.where(is_f, xcum, x)
    xc_hi, xc_lo = _split_hilo(xc)
    xe = _dot(jnp.concatenate([xc_hi, xc_lo], axis=1), e2_ref[...])
    ones = jnp.ones((c, LANES), F32)
    for h in range(M_HEADS):
        ie = xe[:, h * LANES:(h + 1) * LANES]
        fe = xe[:, (M_HEADS + h) * LANES:(M_HEADS + h + 1) * LANES]
        a = ie - fe
        a_row = jnp.sum(jnp.where(row == lane, a, 0.0), axis=0, keepdims=True)
        d = jnp.where(lane <= row, fe + a_row, -jnp.inf)[:, :c]
        g = fe + m_sc[h]
        m_t = jnp.maximum(g, jnp.max(d, axis=-1, keepdims=True))
        w_int = jnp.exp(g - m_t)
        w = jnp.exp(d - m_t[:, :c])
        qh = p_ref[:, OFF_MQ + h * M_DH:OFF_MQ + (h + 1) * M_DH].astype(BF16)
        kf = p_ref[:, OFF_MK + h * M_DH:OFF_MK + (h + 1) * M_DH] * (M_DH ** -0.5)
        vaug = jnp.concatenate(
            [p_ref[:, OFF_MV + h * M_DH:OFF_MV + (h + 1) * M_DH], ones], axis=1).astype(BF16)
        qk = (_dot_nt(qh, kf.astype(BF16)) * w).astype(BF16)
        caug = c_sc[h]
        nd = _dot(qk, vaug) + jnp.concatenate([w_int, w_int], axis=1) * _dot(qh, caug.astype(BF16))
        den = jnp.maximum(jnp.abs(nd[:, M_DH:]), jnp.exp(-m_t))
        hh = nd[:, :M_DH] / den
        og = jax.nn.sigmoid(p_ref[:, OFF_MO + h * M_DH:OFF_MO + (h + 1) * M_DH]) * hh
        mix_ref[:, GLA_V + h * M_DH:GLA_V + (h + 1) * M_DH] = _rms(og, mn_ref[...])
        m_c = m_t[c - 1:c, :]
        decay = jnp.exp(g[c - 1:c, :] - m_c)
        ws = jnp.exp(a + (fe[c - 1:c, :] - m_c))
        kw = (kf * ws).astype(BF16)
        c_sc[h] = jnp.concatenate([decay, decay], axis=1) * caug + _dot_tn(kw, vaug)
        m_sc[h] = m_c

    @pl.when(j == pl.num_programs(1) - 1)
    def _():
        s_out[0] = s_sc[...]
        c_out[0] = c_sc[...]
        m_out[0] = m_sc[...]


def _mixer(p, wa_pad, ba, gn, bsm, mn, s0t, caug0, m0e, batch, length, c):
    a2, lv, e2, nlev = _mixer_consts(c)
    nchunk = length // c
    const2 = lambda shape: pl.BlockSpec(shape, lambda b, j: (0, 0))
    return pl.pallas_call(
        functools.partial(_mixer_kernel, c=c, nlev=nlev),
        grid=(batch, nchunk),
        in_specs=[
            pl.BlockSpec((c, P_COLS), lambda b, j: (b * nchunk + j, 0)),
            const2(wa_pad.shape), const2(ba.shape), const2(gn.shape), const2(bsm.shape),
            const2(mn.shape), const2(a2.shape), const2(lv.shape), const2(e2.shape),
            pl.BlockSpec((1, GLA_DV, GLA_QK), lambda b, j: (b, 0, 0)),
            pl.BlockSpec((1, M_HEADS, M_DH, 2 * M_DH), lambda b, j: (b, 0, 0, 0)),
            pl.BlockSpec((1, M_HEADS, 1, LANES), lambda b, j: (b, 0, 0, 0)),
        ],
        out_specs=[
            pl.BlockSpec((c, GLA_V + M_W), lambda b, j: (b * nchunk + j, 0)),
            pl.BlockSpec((1, GLA_DV, GLA_QK), lambda b, j: (b, 0, 0)),
            pl.BlockSpec((1, M_HEADS, M_DH, 2 * M_DH), lambda b, j: (b, 0, 0, 0)),
            pl.BlockSpec((1, M_HEADS, 1, LANES), lambda b, j: (b, 0, 0, 0)),
        ],
        out_shape=[
            jax.ShapeDtypeStruct((batch * length, GLA_V + M_W), F32),
            jax.ShapeDtypeStruct((batch, GLA_DV, GLA_QK), F32),
            jax.ShapeDtypeStruct((batch, M_HEADS, M_DH, 2 * M_DH), F32),
            jax.ShapeDtypeStruct((batch, M_HEADS, 1, LANES), F32),
        ],
        scratch_shapes=[
            pltpu.VMEM((GLA_DV, GLA_QK), F32),
            pltpu.VMEM((M_HEADS, M_DH, 2 * M_DH), F32),
            pltpu.VMEM((M_HEADS, 1, LANES), F32),
        ],
        compiler_params=pltpu.CompilerParams(
            dimension_semantics=("arbitrary", "arbitrary"), vmem_limit_bytes=VMEM_LIMIT),
        name="mixer",
    )(p, wa_pad, ba, gn, bsm, mn, a2, lv, e2, s0t, caug0, m0e)


def _ffn_kernel(x_ref, mix_ref, wo_ref, n2_ref, wg_ref, wu_ref, cw_ref, cb_ref, wd_ref, nf_ref,
                cst_ref, y_ref, cout_ref, carry_sc, *, tile, fslab):
    j = pl.program_id(1)

    @pl.when(j == 0)
    def _():
        carry_sc[...] = cst_ref[0]

    x1 = x_ref[...] + _dot(mix_ref[...].astype(BF16), wo_ref[...])
    h2 = _rms(x1, n2_ref[...]).astype(BF16)
    d_ff = wg_ref.shape[1]
    row = lax.broadcasted_iota(jnp.int32, (tile, fslab), 0)
    acc = jnp.zeros(x1.shape, F32)
    for s in range(d_ff // fslab):
        cs = slice(s * fslab, (s + 1) * fslab)
        g = _dot(h2, wg_ref[:, cs])
        u = _dot(h2, wu_ref[:, cs])
        p0 = carry_sc[0:1, cs]
        p1 = carry_sc[1:2, cs]
        g1 = jnp.where(row == 0, p1, pltpu.roll(g, 1, axis=0))
        g2 = jnp.where(row == 0, p0, jnp.where(row == 1, p1, pltpu.roll(g, 2, axis=0)))
        gc = cb_ref[:, cs] + (cw_ref[0:1, cs] * g2 + cw_ref[1:2, cs] * g1 + cw_ref[2:3, cs] * g)
        carry_sc[0:1, cs] = g[tile - 2:tile - 1, :]
        carry_sc[1:2, cs] = g[tile - 1:tile, :]
        act = (jax.nn.gelu(gc) * u).astype(BF16)
        acc = acc + _dot(act, wd_ref[cs, :])
    y_ref[...] = _rms(x1 + acc, nf_ref[...])

    @pl.when(j == pl.num_programs(1) - 1)
    def _():
        cout_ref[0] = carry_sc[...]


def _ffn(x2d, mix, w_out, norm2, w_g, w_u, conv_w, conv_b, w_d, norm_f, conv_st, batch, length,
         tile, fslab):
    d = x2d.shape[1]
    d_ff = w_g.shape[1]
    ntile = length // tile
    const2 = lambda shape: pl.BlockSpec(shape, lambda b, j: (0, 0))
    rows = lambda width: pl.BlockSpec((tile, width), lambda b, j: (b * ntile + j, 0))
    return pl.pallas_call(
        functools.partial(_ffn_kernel, tile=tile, fslab=fslab),
        grid=(batch, ntile),
        in_specs=[
            rows(d), rows(mix.shape[1]),
            const2(w_out.shape), const2(norm2.shape), const2(w_g.shape), const2(w_u.shape),
            const2(conv_w.shape), const2(conv_b.shape), const2(w_d.shape), const2(norm_f.shape),
            pl.BlockSpec((1, FFN_CONV - 1, d_ff), lambda b, j: (b, 0, 0)),
        ],
        out_specs=[
            rows(d),
            pl.BlockSpec((1, FFN_CONV - 1, d_ff), lambda b, j: (b, 0, 0)),
        ],
        out_shape=[
            jax.ShapeDtypeStruct(x2d.shape, F32),
            jax.ShapeDtypeStruct((batch, FFN_CONV - 1, d_ff), F32),
        ],
        scratch_shapes=[pltpu.VMEM((FFN_CONV - 1, d_ff), F32)],
        compiler_params=pltpu.CompilerParams(
            dimension_semantics=("arbitrary", "arbitrary"), vmem_limit_bytes=VMEM_LIMIT),
        name="ffn",
    )(x2d, mix, w_out, norm2, w_g, w_u, conv_w, conv_b, w_d, norm_f, conv_st)


def _layer(x, s_gla, c_m, n_m, m_m, conv_st, wts, rows_tile, ffn_tile):
    batch, length, d = x.shape
    c = min(CHUNK, length)
    x2d = x.reshape(batch * length, d)
    p = _inproj(x2d, wts["norm1"], wts["w_in_r"], rows_tile)
    s0t = jnp.transpose(s_gla, (0, 3, 1, 2)).reshape(batch, GLA_DV, GLA_QK)
    caug0 = jnp.concatenate(
        [c_m, jnp.broadcast_to(n_m[..., None], (batch, M_HEADS, M_DH, M_DH))], axis=-1)
    m0e = jnp.broadcast_to(m_m[..., None, None], (batch, M_HEADS, 1, LANES))
    mix, s_t, caug, m_e = _mixer(p, wts["wa_pad"], wts["b_gla_a"], wts["gla_norm"], wts["b_small"],
                                 wts["m_norm"], s0t, caug0, m0e, batch, length, c)
    y, conv_new = _ffn(x2d, mix, wts["w_out"], wts["norm2"], wts["w_g"], wts["w_u"], wts["conv_w"],
                       wts["conv_b"], wts["w_d"], wts["norm_f"], conv_st, batch, length,
                       ffn_tile, 256)
    s_new = jnp.transpose(s_t.reshape(batch, GLA_DV, GLA_HEADS, GLA_DK), (0, 2, 3, 1))
    return (y.reshape(batch, length, d), s_new, caug[..., :M_DH], caug[..., M_DH], m_e[:, :, 0, 0],
            conv_new)


def kernel(x_prompt, x_sample, state_gla, state_mlstm_C, state_mlstm_n, state_mlstm_m, cache_ffn_conv,
           norm1, w_in, w_gla_a, b_gla_a, gla_norm, b_m_i, b_m_f, m_norm, w_out, norm2, w_g, w_u,
           conv_w, conv_b, w_d, norm_f):
    assert w_in.shape[0] == 1, "single-layer stack"
    d = x_prompt.shape[-1]
    w = w_in[0]
    o_glr = 2 * GLA_QK + 2 * GLA_V
    o_m = o_glr + GLA_RANK
    o_if = o_m + 4 * M_W
    w_in_r = jnp.concatenate(
        [w[:, :o_glr], w[:, o_m:o_if], w[:, o_glr:o_m], w[:, o_if:],
         jnp.zeros((d, LANES - GLA_RANK - 2 * M_HEADS), w.dtype)], axis=1).astype(BF16)
    wa_pad = jnp.concatenate(
        [w_gla_a[0], jnp.zeros((LANES - GLA_RANK, GLA_QK), w_gla_a.dtype)], axis=0).astype(BF16)
    b_small = jnp.concatenate(
        [jnp.zeros((GLA_RANK,), F32), b_m_i[0], b_m_f[0],
         jnp.zeros((LANES - GLA_RANK - 2 * M_HEADS,), F32)])[None]
    wts = dict(
        norm1=norm1, w_in_r=w_in_r, wa_pad=wa_pad, b_gla_a=b_gla_a, gla_norm=gla_norm,
        b_small=b_small, m_norm=m_norm, w_out=w_out[0].astype(BF16), norm2=norm2,
        w_g=w_g[0].astype(BF16), w_u=w_u[0].astype(BF16), conv_w=conv_w[0], conv_b=conv_b,
        w_d=w_d[0].astype(BF16), norm_f=norm_f[None])

    bp = x_prompt.shape[0]
    dt = x_prompt.dtype
    d_ff = w_g.shape[-1]
    zp = lambda *s: jnp.zeros((bp,) + s, dt)
    outs_p = _layer(x_prompt, zp(GLA_HEADS, GLA_DK, GLA_DV), zp(M_HEADS, M_DH, M_DH),
                    zp(M_HEADS, M_DH), zp(M_HEADS), zp(FFN_CONV - 1, d_ff), wts,
                    rows_tile=256, ffn_tile=256)
    outs_s = _layer(x_sample, state_gla[0], state_mlstm_C[0], state_mlstm_n[0], state_mlstm_m[0],
                    cache_ffn_conv[0], wts,
                    rows_tile=x_sample.shape[0] * x_sample.shape[1], ffn_tile=x_sample.shape[1])
    y_p, *st_p = outs_p
    y_s, *st_s = outs_s
    return (y_p, y_s) + tuple(s[None] for s in st_p) + tuple(s[None] for s in st_s)
```

```python
import functools

import numpy as np
import jax
import jax.numpy as jnp
from jax import lax
from jax.experimental import pallas as pl
from jax.experimental.pallas import tpu as pltpu

F32 = jnp.float32
BF16 = jnp.bfloat16

EPS = 1e-6
CHUNK = 64
GLA_HEADS = 4
GLA_DK = 64
GLA_DV = 128
GLA_RANK = 16
GLA_TAU = 16.0
M_HEADS = 4
M_DH = 128
FFN_CONV = 3
LANES = 128

GLA_QK = GLA_HEADS * GLA_DK
GLA_V = GLA_HEADS * GLA_DV
M_W = M_HEADS * M_DH
OFF_GQ = 0
OFF_GK = OFF_GQ + GLA_QK
OFF_GV = OFF_GK + GLA_QK
OFF_GR = OFF_GV + GLA_V
OFF_MQ = OFF_GR + GLA_V
OFF_MK = OFF_MQ + M_W
OFF_MV = OFF_MK + M_W
OFF_MO = OFF_MV + M_W
OFF_SMALL = OFF_MO + M_W
P_COLS = OFF_SMALL + LANES
SM_I = GLA_RANK
SM_F = GLA_RANK + M_HEADS

VMEM_LIMIT = 56 * 1024 * 1024


def _dot(a, b):
    return jnp.dot(a, b, preferred_element_type=F32)


def _dot_nt(a, b):
    return lax.dot_general(a, b, (((1,), (1,)), ((), ())), preferred_element_type=F32)


def _dot_tn(a, b):
    return lax.dot_general(a, b, (((0,), (0,)), ((), ())), preferred_element_type=F32)


def _split_hilo(x):
    hi = x.astype(BF16)
    lo = (x - hi.astype(F32)).astype(BF16)
    return hi, lo


def _log_sigmoid(x):
    return jnp.minimum(x, 0.0) - jnp.log1p(jnp.exp(-jnp.abs(x)))


def _rms(x, g):
    return x * lax.rsqrt(jnp.mean(x * x, axis=-1, keepdims=True) + EPS) * g


def _inproj_kernel(x_ref, n1_ref, w_ref, p_ref):
    h = _rms(x_ref[...], n1_ref[...]).astype(BF16)
    p_ref[...] = _dot(h, w_ref[...])


def _inproj(x2d, norm1, w_in_r, rows_tile):
    rows, d = x2d.shape
    return pl.pallas_call(
        _inproj_kernel,
        grid=(rows // rows_tile,),
        in_specs=[
            pl.BlockSpec((rows_tile, d), lambda i: (i, 0)),
            pl.BlockSpec((1, d), lambda i: (0, 0)),
            pl.BlockSpec((d, P_COLS), lambda i: (0, 0)),
        ],
        out_specs=pl.BlockSpec((rows_tile, P_COLS), lambda i: (i, 0)),
        out_shape=jax.ShapeDtypeStruct((rows, P_COLS), F32),
        compiler_params=pltpu.CompilerParams(
            dimension_semantics=("arbitrary",), vmem_limit_bytes=VMEM_LIMIT),
        name="inproj",
    )(x2d, norm1, w_in_r)


def _mixer_consts(c):
    nlev = int(np.log2(c))
    assert 2 ** nlev == c
    t = np.arange(c)[:, None]
    u = np.arange(c)[None, :]
    blocks = [(u <= t), (u > t)]
    for l in range(1, nlev + 1):
        bsz, half = 2 ** l, 2 ** (l - 1)
        mid = (t // bsz) * bsz + half
        right = (t % bsz) >= half
        blocks.append(np.where(right, (u >= mid) & (u <= t), (u > t) & (u < mid)))
    a = np.concatenate(blocks, axis=0).astype(np.float32)
    a2 = np.concatenate([a, a], axis=1)
    x = np.bitwise_xor(t, u)
    lv = np.where(u > t, -1, np.where(x == 0, 0, np.floor(np.log2(np.maximum(x, 1))) + 1))
    e = np.zeros((LANES, 2 * M_HEADS * LANES), np.float32)
    for h in range(M_HEADS):
        e[SM_I + h, h * LANES:(h + 1) * LANES] = 1.0
        e[SM_F + h, (M_HEADS + h) * LANES:(M_HEADS + h + 1) * LANES] = 1.0
    e2 = np.concatenate([e, e], axis=0)
    return (jnp.asarray(a2, BF16), jnp.asarray(lv, jnp.int32), jnp.asarray(e2, BF16), nlev)


def _mixer_kernel(p_ref, wa_ref, ba_ref, gn_ref, bsm_ref, mn_ref, a2_ref, lv_ref, e2_ref,
                  s0_ref, c0_ref, m0_ref,
                  mix_ref, s_out, c_out, m_out,
                  s_sc, c_sc, m_sc, *, c, nlev, streams):
    j = pl.program_id(1)

    @pl.when(j == 0)
    def _():
        s_sc[...] = s0_ref[...]
        c_sc[...] = c0_ref[...]
        m_sc[...] = m0_ref[...]

    lv = lv_ref[...]
    lv_is = [lv == l for l in range(nlev + 1)]
    lane = lax.broadcasted_iota(jnp.int32, (c, LANES), 1)
    row = lax.broadcasted_iota(jnp.int32, (c, LANES), 0)
    is_f = (lane >= SM_F) & (lane < SM_F + M_HEADS)
    on_diag = row == lane
    causal = lane <= row
    ones = jnp.ones((c, LANES), F32)

    gs = range(streams)
    small = [p_ref[g, :, OFF_SMALL:OFF_SMALL + LANES] for g in gs]

    xa = [_dot(small[g].astype(BF16), wa_ref[...]) + ba_ref[...] for g in gs]
    x = [small[g] + bsm_ref[...] for g in gs]
    la = [_log_sigmoid(xa[g]) * (1.0 / GLA_TAU) for g in gs]
    x = [jnp.where(is_f, _log_sigmoid(x[g]), x[g]) for g in gs]
    la_sp = [jnp.concatenate(_split_hilo(la[g]), axis=0) for g in gs]
    x_sp = [jnp.concatenate(_split_hilo(x[g]), axis=0) for g in gs]
    z = [_dot(a2_ref[...], la_sp[g]) for g in gs]
    xcum = [_dot(a2_ref[0:c, :], x_sp[g]) for g in gs]
    xc = [jnp.where(is_f, xcum[g], x[g]) for g in gs]
    xc_sp = [jnp.concatenate(_split_hilo(xc[g]), axis=1) for g in gs]
    xe = [_dot(xc_sp[g], e2_ref[...]) for g in gs]
    ez = [jnp.exp(z[g]) for g in gs]

    q = [p_ref[g, :, OFF_GQ:OFF_GQ + GLA_QK] * (GLA_DK ** -0.5) for g in gs]
    k = [p_ref[g, :, OFF_GK:OFF_GK + GLA_QK] for g in gs]
    q_in = [(q[g] * ez[g][0:c]).astype(BF16) for g in gs]
    k_dec = [(k[g] * ez[g][c:2 * c]).astype(BF16) for g in gs]
    dec = [ez[g][c - 1:c, :] for g in gs]
    q_lv = [[q[g].astype(BF16)] + [(q[g] * ez[g][(2 + l) * c:(3 + l) * c]).astype(BF16)
                                   for l in range(nlev)] for g in gs]
    k_lv = [[k[g].astype(BF16)] + [(k[g] * ez[g][(2 + l) * c:(3 + l) * c]).astype(BF16)
                                   for l in range(nlev)] for g in gs]
    for h in range(GLA_HEADS):
        sl = slice(h * GLA_DK, (h + 1) * GLA_DK)
        sc = [jnp.zeros((c, c), F32) for g in gs]
        for l in range(nlev + 1):
            for g in gs:
                sc[g] = jnp.where(lv_is[l], _dot_nt(q_lv[g][l][:, sl], k_lv[g][l][:, sl]), sc[g])
        for g in gs:
            vb = p_ref[g, :, OFF_GV + h * GLA_DV:OFF_GV + (h + 1) * GLA_DV].astype(BF16)
            st = s_sc[g, :, sl]
            o = _dot_nt(q_in[g][:, sl], st.astype(BF16)) + _dot(sc[g].astype(BF16), vb)
            s_sc[g, :, sl] = dec[g][:, sl] * st + _dot_tn(vb, k_dec[g][:, sl])
            r = p_ref[g, :, OFF_GR + h * GLA_DV:OFF_GR + (h + 1) * GLA_DV]
            mix_ref[g, :, h * GLA_DV:(h + 1) * GLA_DV] = _rms(o, gn_ref[...]) * (r * jax.nn.sigmoid(r))

    for h in range(M_HEADS):
        for g in gs:
            ie = xe[g][:, h * LANES:(h + 1) * LANES]
            fe = xe[g][:, (M_HEADS + h) * LANES:(M_HEADS + h + 1) * LANES]
            a = ie - fe
            a_row = jnp.sum(jnp.where(on_diag, a, 0.0), axis=0, keepdims=True)
            d = jnp.where(causal, fe + a_row, -jnp.inf)[:, :c]
            gi = fe + m_sc[g, h]
            m_t = jnp.maximum(gi, jnp.max(d, axis=-1, keepdims=True))
            w_int = jnp.exp(gi - m_t)
            w = jnp.exp(d - m_t[:, :c])
            qh = p_ref[g, :, OFF_MQ + h * M_DH:OFF_MQ + (h + 1) * M_DH].astype(BF16)
            kf = p_ref[g, :, OFF_MK + h * M_DH:OFF_MK + (h + 1) * M_DH] * (M_DH ** -0.5)
            vaug = jnp.concatenate(
                [p_ref[g, :, OFF_MV + h * M_DH:OFF_MV + (h + 1) * M_DH], ones], axis=1).astype(BF16)
            qk = (_dot_nt(qh, kf.astype(BF16)) * w).astype(BF16)
            caug = c_sc[g, h]
            nd = _dot(qk, vaug) + jnp.concatenate([w_int, w_int], axis=1) * _dot(qh, caug.astype(BF16))
            den = jnp.maximum(jnp.abs(nd[:, M_DH:]), jnp.exp(-m_t))
            hh = nd[:, :M_DH] / den
            og = jax.nn.sigmoid(p_ref[g, :, OFF_MO + h * M_DH:OFF_MO + (h + 1) * M_DH]) * hh
            mix_ref[g, :, GLA_V + h * M_DH:GLA_V + (h + 1) * M_DH] = _rms(og, mn_ref[...])
            m_c = m_t[c - 1:c, :]
            decay = jnp.exp(gi[c - 1:c, :] - m_c)
            ws = jnp.exp(a + (fe[c - 1:c, :] - m_c))
            kw = (kf * ws).astype(BF16)
            c_sc[g, h] = jnp.concatenate([decay, decay], axis=1) * caug + _dot_tn(kw, vaug)
            m_sc[g, h] = m_c

    @pl.when(j == pl.num_programs(1) - 1)
    def _():
        s_out[...] = s_sc[...]
        c_out[...] = c_sc[...]
        m_out[...] = m_sc[...]


def _mixer(p, wa_pad, ba, gn, bsm, mn, s0t, caug0, m0e, batch, length, c, streams):
    a2, lv, e2, nlev = _mixer_consts(c)
    nchunk = length // c
    const2 = lambda shape: pl.BlockSpec(shape, lambda b, j: (0, 0))
    state_specs = [
        pl.BlockSpec((streams, GLA_DV, GLA_QK), lambda b, j: (b, 0, 0)),
        pl.BlockSpec((streams, M_HEADS, M_DH, 2 * M_DH), lambda b, j: (b, 0, 0, 0)),
        pl.BlockSpec((streams, M_HEADS, 1, LANES), lambda b, j: (b, 0, 0, 0)),
    ]
    return pl.pallas_call(
        functools.partial(_mixer_kernel, c=c, nlev=nlev, streams=streams),
        grid=(batch // streams, nchunk),
        in_specs=[
            pl.BlockSpec((streams, c, P_COLS), lambda b, j: (b, j, 0)),
            const2(wa_pad.shape), const2(ba.shape), const2(gn.shape), const2(bsm.shape),
            const2(mn.shape), const2(a2.shape), const2(lv.shape), const2(e2.shape),
        ] + state_specs,
        out_specs=[pl.BlockSpec((streams, c, GLA_V + M_W), lambda b, j: (b, j, 0))] + state_specs,
        out_shape=[
            jax.ShapeDtypeStruct((batch, length, GLA_V + M_W), F32),
            jax.ShapeDtypeStruct((batch, GLA_DV, GLA_QK), F32),
            jax.ShapeDtypeStruct((batch, M_HEADS, M_DH, 2 * M_DH), F32),
            jax.ShapeDtypeStruct((batch, M_HEADS, 1, LANES), F32),
        ],
        scratch_shapes=[
            pltpu.VMEM((streams, GLA_DV, GLA_QK), F32),
            pltpu.VMEM((streams, M_HEADS, M_DH, 2 * M_DH), F32),
            pltpu.VMEM((streams, M_HEADS, 1, LANES), F32),
        ],
        compiler_params=pltpu.CompilerParams(
            dimension_semantics=("arbitrary", "arbitrary"), vmem_limit_bytes=VMEM_LIMIT),
        name="mixer",
    )(p.reshape(batch, length, P_COLS), wa_pad, ba, gn, bsm, mn, a2, lv, e2, s0t, caug0, m0e)


def _ffn_kernel(x_ref, mix_ref, wo_ref, n2_ref, wg_ref, wu_ref, cw_ref, cb_ref, wd_ref, nf_ref,
                cst_ref, y_ref, cout_ref, carry_sc, *, tile, fslab):
    j = pl.program_id(1)

    @pl.when(j == 0)
    def _():
        carry_sc[...] = cst_ref[0]

    x1 = x_ref[...] + _dot(mix_ref[...].astype(BF16), wo_ref[...])
    h2 = _rms(x1, n2_ref[...]).astype(BF16)
    d_ff = wg_ref.shape[1]
    row = lax.broadcasted_iota(jnp.int32, (tile, fslab), 0)
    nslab = d_ff // fslab

    def gate_up(s):
        cs = slice(s * fslab, (s + 1) * fslab)
        return _dot(h2, wg_ref[:, cs]), _dot(h2, wu_ref[:, cs])

    acc = jnp.zeros(x1.shape, F32)
    nxt = gate_up(0)
    for s in range(nslab):
        cs = slice(s * fslab, (s + 1) * fslab)
        g, u = nxt
        if s + 1 < nslab:
            nxt = gate_up(s + 1)
        p0 = carry_sc[0:1, cs]
        p1 = carry_sc[1:2, cs]
        g1 = jnp.where(row == 0, p1, pltpu.roll(g, 1, axis=0))
        g2 = jnp.where(row == 0, p0, jnp.where(row == 1, p1, pltpu.roll(g, 2, axis=0)))
        gc = cb_ref[:, cs] + (cw_ref[0:1, cs] * g2 + cw_ref[1:2, cs] * g1 + cw_ref[2:3, cs] * g)
        carry_sc[0:1, cs] = g[tile - 2:tile - 1, :]
        carry_sc[1:2, cs] = g[tile - 1:tile, :]
        act = (jax.nn.gelu(gc) * u).astype(BF16)
        acc = acc + _dot(act, wd_ref[cs, :])
    y_ref[...] = _rms(x1 + acc, nf_ref[...])

    @pl.when(j == pl.num_programs(1) - 1)
    def _():
        cout_ref[0] = carry_sc[...]


def _ffn(x2d, mix, w_out, norm2, w_g, w_u, conv_w, conv_b, w_d, norm_f, conv_st, batch, length,
         tile, fslab):
    d = x2d.shape[1]
    d_ff = w_g.shape[1]
    ntile = length // tile
    const2 = lambda shape: pl.BlockSpec(shape, lambda b, j: (0, 0))
    rows = lambda width: pl.BlockSpec((tile, width), lambda b, j: (b * ntile + j, 0))
    return pl.pallas_call(
        functools.partial(_ffn_kernel, tile=tile, fslab=fslab),
        grid=(batch, ntile),
        in_specs=[
            rows(d), rows(mix.shape[1]),
            const2(w_out.shape), const2(norm2.shape), const2(w_g.shape), const2(w_u.shape),
            const2(conv_w.shape), const2(conv_b.shape), const2(w_d.shape), const2(norm_f.shape),
            pl.BlockSpec((1, FFN_CONV - 1, d_ff), lambda b, j: (b, 0, 0)),
        ],
        out_specs=[
            rows(d),
            pl.BlockSpec((1, FFN_CONV - 1, d_ff), lambda b, j: (b, 0, 0)),
        ],
        out_shape=[
            jax.ShapeDtypeStruct(x2d.shape, F32),
            jax.ShapeDtypeStruct((batch, FFN_CONV - 1, d_ff), F32),
        ],
        scratch_shapes=[pltpu.VMEM((FFN_CONV - 1, d_ff), F32)],
        compiler_params=pltpu.CompilerParams(
            dimension_semantics=("arbitrary", "arbitrary"), vmem_limit_bytes=VMEM_LIMIT),
        name="ffn",
    )(x2d, mix, w_out, norm2, w_g, w_u, conv_w, conv_b, w_d, norm_f, conv_st)


def _layer(x, s_gla, c_m, n_m, m_m, conv_st, wts, rows_tile, ffn_tile, streams):
    batch, length, d = x.shape
    c = min(CHUNK, length)
    x2d = x.reshape(batch * length, d)
    p = _inproj(x2d, wts["norm1"], wts["w_in_r"], rows_tile)
    s0t = jnp.transpose(s_gla, (0, 3, 1, 2)).reshape(batch, GLA_DV, GLA_QK)
    caug0 = jnp.concatenate(
        [c_m, jnp.broadcast_to(n_m[..., None], (batch, M_HEADS, M_DH, M_DH))], axis=-1)
    m0e = jnp.broadcast_to(m_m[..., None, None], (batch, M_HEADS, 1, LANES))
    mix, s_t, caug, m_e = _mixer(p, wts["wa_pad"], wts["b_gla_a"], wts["gla_norm"], wts["b_small"],
                                 wts["m_norm"], s0t, caug0, m0e, batch, length, c, streams)
    y, conv_new = _ffn(x2d, mix.reshape(batch * length, GLA_V + M_W), wts["w_out"], wts["norm2"],
                       wts["w_g"], wts["w_u"], wts["conv_w"], wts["conv_b"], wts["w_d"],
                       wts["norm_f"], conv_st, batch, length, ffn_tile, 256)
    s_new = jnp.transpose(s_t.reshape(batch, GLA_DV, GLA_HEADS, GLA_DK), (0, 2, 3, 1))
    return (y.reshape(batch, length, d), s_new, caug[..., :M_DH], caug[..., M_DH], m_e[:, :, 0, 0],
            conv_new)


def kernel(x_prompt, x_sample, state_gla, state_mlstm_C, state_mlstm_n, state_mlstm_m, cache_ffn_conv,
           norm1, w_in, w_gla_a, b_gla_a, gla_norm, b_m_i, b_m_f, m_norm, w_out, norm2, w_g, w_u,
           conv_w, conv_b, w_d, norm_f):
    assert w_in.shape[0] == 1, "single-layer stack"
    d = x_prompt.shape[-1]
    w = w_in[0]
    o_glr = 2 * GLA_QK + 2 * GLA_V
    o_m = o_glr + GLA_RANK
    o_if = o_m + 4 * M_W
    w_in_r = jnp.concatenate(
        [w[:, :o_glr], w[:, o_m:o_if], w[:, o_glr:o_m], w[:, o_if:],
         jnp.zeros((d, LANES - GLA_RANK - 2 * M_HEADS), w.dtype)], axis=1).astype(BF16)
    wa_pad = jnp.concatenate(
        [w_gla_a[0], jnp.zeros((LANES - GLA_RANK, GLA_QK), w_gla_a.dtype)], axis=0).astype(BF16)
    b_small = jnp.concatenate(
        [jnp.zeros((GLA_RANK,), F32), b_m_i[0], b_m_f[0],
         jnp.zeros((LANES - GLA_RANK - 2 * M_HEADS,), F32)])[None]
    wts = dict(
        norm1=norm1, w_in_r=w_in_r, wa_pad=wa_pad, b_gla_a=b_gla_a, gla_norm=gla_norm,
        b_small=b_small, m_norm=m_norm, w_out=w_out[0].astype(BF16), norm2=norm2,
        w_g=w_g[0].astype(BF16), w_u=w_u[0].astype(BF16), conv_w=conv_w[0], conv_b=conv_b,
        w_d=w_d[0].astype(BF16), norm_f=norm_f[None])

    bp = x_prompt.shape[0]
    dt = x_prompt.dtype
    d_ff = w_g.shape[-1]
    zp = lambda *s: jnp.zeros((bp,) + s, dt)
    outs_p = _layer(x_prompt, zp(GLA_HEADS, GLA_DK, GLA_DV), zp(M_HEADS, M_DH, M_DH),
                    zp(M_HEADS, M_DH), zp(M_HEADS), zp(FFN_CONV - 1, d_ff), wts,
                    rows_tile=256, ffn_tile=256, streams=4)
    outs_s = _layer(x_sample, state_gla[0], state_mlstm_C[0], state_mlstm_n[0], state_mlstm_m[0],
                    cache_ffn_conv[0], wts,
                    rows_tile=x_sample.shape[0] * x_sample.shape[1], ffn_tile=x_sample.shape[1],
                    streams=x_sample.shape[0])
    y_p, *st_p = outs_p
    y_s, *st_s = outs_s
    return (y_p, y_s) + tuple(s[None] for s in st_p) + tuple(s[None] for s in st_s)
```

```python
import functools

import numpy as np
import jax
import jax.numpy as jnp
from jax import lax
from jax.experimental import pallas as pl
from jax.experimental.pallas import tpu as pltpu

F32 = jnp.float32
BF16 = jnp.bfloat16

EPS = 1e-6
CHUNK = 64
GLA_HEADS = 4
GLA_DK = 64
GLA_DV = 128
GLA_RANK = 16
GLA_TAU = 16.0
M_HEADS = 4
M_DH = 128
FFN_CONV = 3
LANES = 128
SUBLANES = 8

GLA_QK = GLA_HEADS * GLA_DK
GLA_V = GLA_HEADS * GLA_DV
M_W = M_HEADS * M_DH
D_MIX = GLA_V + M_W
OFF_GQ = 0
OFF_GK = OFF_GQ + GLA_QK
OFF_GV = OFF_GK + GLA_QK
OFF_GR = OFF_GV + GLA_V
OFF_MQ = OFF_GR + GLA_V
OFF_MK = OFF_MQ + M_W
OFF_MV = OFF_MK + M_W
OFF_MO = OFF_MV + M_W
OFF_SMALL = OFF_MO + M_W
P_COLS = OFF_SMALL + LANES
SM_I = GLA_RANK
SM_F = GLA_RANK + M_HEADS

INPROJ_SLAB = 512
FFN_SLAB = 256
VMEM_LIMIT = 60 * 1024 * 1024


def _dot(a, b):
    return jnp.dot(a, b, preferred_element_type=F32)


def _dot_nt(a, b):
    return lax.dot_general(a, b, (((1,), (1,)), ((), ())), preferred_element_type=F32)


def _dot_tn(a, b):
    return lax.dot_general(a, b, (((0,), (0,)), ((), ())), preferred_element_type=F32)


def _split_hilo(x):
    hi = x.astype(BF16)
    lo = (x - hi.astype(F32)).astype(BF16)
    return hi, lo


def _log_sigmoid(x):
    return jnp.minimum(x, 0.0) - jnp.log1p(jnp.exp(-jnp.abs(x)))


def _rms(x, g):
    return x * lax.rsqrt(jnp.mean(x * x, axis=-1, keepdims=True) + EPS) * g


def _interleave(*tasks):
    alive = list(tasks)
    while alive:
        for task in list(alive):
            try:
                next(task)
            except StopIteration:
                alive.remove(task)


def _mixer_consts(c):
    nlev = int(np.log2(c))
    assert 2 ** nlev == c
    t = np.arange(c)[:, None]
    u = np.arange(c)[None, :]
    blocks = [(u <= t), (u > t)]
    for l in range(1, nlev + 1):
        bsz, half = 2 ** l, 2 ** (l - 1)
        mid = (t // bsz) * bsz + half
        right = (t % bsz) >= half
        blocks.append(np.where(right, (u >= mid) & (u <= t), (u > t) & (u < mid)))
    a = np.concatenate(blocks, axis=0).astype(np.float32)
    a2 = np.concatenate([a, a], axis=1)
    x = np.bitwise_xor(t, u)
    lv = np.where(u > t, -1, np.where(x == 0, 0, np.floor(np.log2(np.maximum(x, 1))) + 1))
    e = np.zeros((LANES, 2 * M_HEADS * LANES), np.float32)
    for h in range(M_HEADS):
        e[SM_I + h, h * LANES:(h + 1) * LANES] = 1.0
        e[SM_F + h, (M_HEADS + h) * LANES:(M_HEADS + h + 1) * LANES] = 1.0
    e2 = np.concatenate([e, e], axis=0)
    return (jnp.asarray(a2, BF16), jnp.asarray(lv, jnp.int32), jnp.asarray(e2, BF16), nlev)


def _front(xf_ref, n1_ref, win_ref, wa_ref, ba_ref, gn_ref, bsm_ref, mn_ref, a2_ref, lv_ref, e2_ref,
           p_sc, mix_sc, slot, s_sc, c_sc, m_sc, *, c, nlev, streams):
    rows = streams * c
    d = xf_ref.shape[-1]
    h_in = _rms(xf_ref[...].reshape(rows, d), n1_ref[...]).astype(BF16)
    yield
    for lo in range(0, P_COLS, INPROJ_SLAB):
        hi = min(lo + INPROJ_SLAB, P_COLS)
        p_sc[:, lo:hi] = _dot(h_in, win_ref[:, lo:hi])
        yield

    def pcol(g, off, width):
        return p_sc[g * c:(g + 1) * c, off:off + width]

    lv = lv_ref[...]
    lv_is = [lv == l for l in range(nlev + 1)]
    lane = lax.broadcasted_iota(jnp.int32, (c, LANES), 1)
    row = lax.broadcasted_iota(jnp.int32, (c, LANES), 0)
    is_f = (lane >= SM_F) & (lane < SM_F + M_HEADS)
    on_diag = row == lane
    causal = lane <= row
    ones = jnp.ones((c, LANES), F32)

    gs = range(streams)
    small = [pcol(g, OFF_SMALL, LANES) for g in gs]

    xa = [_dot(small[g].astype(BF16), wa_ref[...]) + ba_ref[...] for g in gs]
    x = [small[g] + bsm_ref[...] for g in gs]
    la = [_log_sigmoid(xa[g]) * (1.0 / GLA_TAU) for g in gs]
    x = [jnp.where(is_f, _log_sigmoid(x[g]), x[g]) for g in gs]
    la_sp = [jnp.concatenate(_split_hilo(la[g]), axis=0) for g in gs]
    x_sp = [jnp.concatenate(_split_hilo(x[g]), axis=0) for g in gs]
    yield
    z = [_dot(a2_ref[...], la_sp[g]) for g in gs]
    xcum = [_dot(a2_ref[0:c, :], x_sp[g]) for g in gs]
    xc = [jnp.where(is_f, xcum[g], x[g]) for g in gs]
    xc_sp = [jnp.concatenate(_split_hilo(xc[g]), axis=1) for g in gs]
    yield
    xe = [_dot(xc_sp[g], e2_ref[...]) for g in gs]
    ez = [jnp.exp(z[g]) for g in gs]
    yield

    q = [pcol(g, OFF_GQ, GLA_QK) * (GLA_DK ** -0.5) for g in gs]
    k = [pcol(g, OFF_GK, GLA_QK) for g in gs]
    q_in = [(q[g] * ez[g][0:c]).astype(BF16) for g in gs]
    k_dec = [(k[g] * ez[g][c:2 * c]).astype(BF16) for g in gs]
    dec = [ez[g][c - 1:c, :] for g in gs]
    q_lv = [[q[g].astype(BF16)] + [(q[g] * ez[g][(2 + l) * c:(3 + l) * c]).astype(BF16)
                                   for l in range(nlev)] for g in gs]
    k_lv = [[k[g].astype(BF16)] + [(k[g] * ez[g][(2 + l) * c:(3 + l) * c]).astype(BF16)
                                   for l in range(nlev)] for g in gs]
    yield
    for h in range(GLA_HEADS):
        sl = slice(h * GLA_DK, (h + 1) * GLA_DK)
        sc = [jnp.zeros((c, c), F32) for g in gs]
        for l in range(nlev + 1):
            for g in gs:
                sc[g] = jnp.where(lv_is[l], _dot_nt(q_lv[g][l][:, sl], k_lv[g][l][:, sl]), sc[g])
        for g in gs:
            vb = pcol(g, OFF_GV + h * GLA_DV, GLA_DV).astype(BF16)
            st = s_sc[g, :, sl]
            o = _dot_nt(q_in[g][:, sl], st.astype(BF16)) + _dot(sc[g].astype(BF16), vb)
            s_sc[g, :, sl] = dec[g][:, sl] * st + _dot_tn(vb, k_dec[g][:, sl])
            r = pcol(g, OFF_GR + h * GLA_DV, GLA_DV)
            mix_sc[slot, g * c:(g + 1) * c, h * GLA_DV:(h + 1) * GLA_DV] = (
                _rms(o, gn_ref[...]) * (r * jax.nn.sigmoid(r)))
        yield

    for h in range(M_HEADS):
        for g in gs:
            ie = xe[g][:, h * LANES:(h + 1) * LANES]
            fe = xe[g][:, (M_HEADS + h) * LANES:(M_HEADS + h + 1) * LANES]
            a = ie - fe
            a_row = jnp.sum(jnp.where(on_diag, a, 0.0), axis=0, keepdims=True)
            dm = jnp.where(causal, fe + a_row, -jnp.inf)[:, :c]
            gi = fe + m_sc[g, h]
            m_t = jnp.maximum(gi, jnp.max(dm, axis=-1, keepdims=True))
            w_int = jnp.exp(gi - m_t)
            w = jnp.exp(dm - m_t[:, :c])
            qh = pcol(g, OFF_MQ + h * M_DH, M_DH).astype(BF16)
            kf = pcol(g, OFF_MK + h * M_DH, M_DH) * (M_DH ** -0.5)
            vaug = jnp.concatenate([pcol(g, OFF_MV + h * M_DH, M_DH), ones], axis=1).astype(BF16)
            qk = (_dot_nt(qh, kf.astype(BF16)) * w).astype(BF16)
            caug = c_sc[g, h]
            nd = _dot(qk, vaug) + jnp.concatenate([w_int, w_int], axis=1) * _dot(qh, caug.astype(BF16))
            den = jnp.maximum(jnp.abs(nd[:, M_DH:]), jnp.exp(-m_t))
            hh = nd[:, :M_DH] / den
            og = jax.nn.sigmoid(pcol(g, OFF_MO + h * M_DH, M_DH)) * hh
            mix_sc[slot, g * c:(g + 1) * c, GLA_V + h * M_DH:GLA_V + (h + 1) * M_DH] = (
                _rms(og, mn_ref[...]))
            m_c = m_t[c - 1:c, :]
            decay = jnp.exp(gi[c - 1:c, :] - m_c)
            ws = jnp.exp(a + (fe[c - 1:c, :] - m_c))
            kw = (kf * ws).astype(BF16)
            c_sc[g, h] = jnp.concatenate([decay, decay], axis=1) * caug + _dot_tn(kw, vaug)
            m_sc[g, h] = m_c
        yield


def _back(xb_ref, mix_sc, slot_prev, wo_ref, n2_ref, wg_ref, wu_ref, cw_ref, cb_ref, wd_ref, nf_ref,
          y_ref, carry_sc, *, c, streams):
    rows = streams * c
    d = xb_ref.shape[-1]
    d_ff = wg_ref.shape[1]
    nslab = d_ff // FFN_SLAB
    x1 = xb_ref[...].reshape(rows, d) + _dot(mix_sc[slot_prev].astype(BF16), wo_ref[...])
    h2 = _rms(x1, n2_ref[...]).astype(BF16)
    yield

    def gate_up(s):
        cs = slice(s * FFN_SLAB, (s + 1) * FFN_SLAB)
        return _dot(h2, wg_ref[:, cs]), _dot(h2, wu_ref[:, cs])

    row8 = lax.broadcasted_iota(jnp.int32, (SUBLANES, FFN_SLAB), 0)
    acc = jnp.zeros((rows, d), F32)
    nxt = gate_up(0)
    yield
    for s in range(nslab):
        cs = slice(s * FFN_SLAB, (s + 1) * FFN_SLAB)
        g, u = nxt
        if s + 1 < nslab:
            nxt = gate_up(s + 1)
        r1 = pltpu.roll(g, 1, axis=0)
        r2 = pltpu.roll(g, 2, axis=0)
        g1, g2 = [], []
        for st in range(streams):
            b0 = st * c
            p0 = carry_sc[st, 0:1, cs]
            p1 = carry_sc[st, 1:2, cs]
            g1 += [jnp.where(row8 == 0, p1, r1[b0:b0 + SUBLANES]), r1[b0 + SUBLANES:b0 + c]]
            g2 += [jnp.where(row8 == 0, p0, jnp.where(row8 == 1, p1, r2[b0:b0 + SUBLANES])),
                   r2[b0 + SUBLANES:b0 + c]]
            carry_sc[st, 0:1, cs] = g[b0 + c - 2:b0 + c - 1, :]
            carry_sc[st, 1:2, cs] = g[b0 + c - 1:b0 + c, :]
        g1 = jnp.concatenate(g1, axis=0)
        g2 = jnp.concatenate(g2, axis=0)
        gc = cb_ref[:, cs] + (cw_ref[0:1, cs] * g2 + cw_ref[1:2, cs] * g1 + cw_ref[2:3, cs] * g)
        act = (jax.nn.gelu(gc) * u).astype(BF16)
        acc = acc + _dot(act, wd_ref[cs, :])
        yield
    y_ref[...] = _rms(x1 + acc, nf_ref[...]).reshape(streams, c, d)


def _layer_kernel(xf_ref, xb_ref, n1_ref, win_ref, wa_ref, ba_ref, gn_ref, bsm_ref, mn_ref, a2_ref,
                  lv_ref, e2_ref, wo_ref, n2_ref, wg_ref, wu_ref, cw_ref, cb_ref, wd_ref, nf_ref,
                  s0_ref, c0_ref, m0_ref, cst_ref,
                  y_ref, s_out, c_out, m_out, cv_out,
                  p_sc, mix_sc, s_sc, c_sc, m_sc, carry_sc, *, c, nlev, streams, nj, nsteps):
    t = pl.program_id(0)
    tf = jnp.minimum(t, nsteps - 1)
    jf = lax.rem(tf, nj)
    tb = jnp.maximum(t - 1, 0)
    jb = lax.rem(tb, nj)
    slot = lax.rem(t, 2)

    @pl.when(t == 0)
    def _():
        mix_sc[1] = jnp.zeros(mix_sc.shape[1:], F32)

    @pl.when(jf == 0)
    def _():
        s_sc[...] = s0_ref[...]
        c_sc[...] = c0_ref[...]
        m_sc[...] = m0_ref[...]

    @pl.when(jb == 0)
    def _():
        carry_sc[...] = cst_ref[...]

    _interleave(
        _front(xf_ref, n1_ref, win_ref, wa_ref, ba_ref, gn_ref, bsm_ref, mn_ref, a2_ref, lv_ref,
               e2_ref, p_sc, mix_sc, slot, s_sc, c_sc, m_sc, c=c, nlev=nlev, streams=streams),
        _back(xb_ref, mix_sc, 1 - slot, wo_ref, n2_ref, wg_ref, wu_ref, cw_ref, cb_ref, wd_ref,
              nf_ref, y_ref, carry_sc, c=c, streams=streams))

    @pl.when((jf == nj - 1) & (t < nsteps))
    def _():
        s_out[...] = s_sc[...]
        c_out[...] = c_sc[...]
        m_out[...] = m_sc[...]

    @pl.when((jb == nj - 1) & (t >= 1))
    def _():
        cv_out[...] = carry_sc[...]


def _layer(x, s_gla, c_m, n_m, m_m, conv_st, wts, streams):
    batch, length, d = x.shape
    c = min(CHUNK, length)
    nj = length // c
    nsteps = (batch // streams) * nj
    d_ff = wts["w_g"].shape[1]
    a2, lv, e2, nlev = _mixer_consts(c)
    s0t = jnp.transpose(s_gla, (0, 3, 1, 2)).reshape(batch, GLA_DV, GLA_QK)
    caug0 = jnp.concatenate(
        [c_m, jnp.broadcast_to(n_m[..., None], (batch, M_HEADS, M_DH, M_DH))], axis=-1)
    m0e = jnp.broadcast_to(m_m[..., None, None], (batch, M_HEADS, 1, LANES))

    def front_idx(t):
        tf = jnp.minimum(t, nsteps - 1)
        return tf // nj, tf % nj

    def back_idx(t):
        tb = jnp.maximum(t - 1, 0)
        return tb // nj, tb % nj

    once = pl.Buffered(1)
    const = lambda a: pl.BlockSpec(a.shape, lambda t: (0,) * a.ndim, pipeline_mode=once)
    fstate = lambda *blk, **kw: pl.BlockSpec(
        (streams,) + blk, lambda t: (front_idx(t)[0],) + (0,) * len(blk), **kw)
    bstate = lambda *blk, **kw: pl.BlockSpec(
        (streams,) + blk, lambda t: (back_idx(t)[0],) + (0,) * len(blk), **kw)
    consts = [wts["norm1"], wts["w_in_r"], wts["wa_pad"], wts["b_gla_a"], wts["gla_norm"],
              wts["b_small"], wts["m_norm"], a2, lv, e2, wts["w_out"], wts["norm2"], wts["w_g"],
              wts["w_u"], wts["conv_w"], wts["conv_b"], wts["w_d"], wts["norm_f"]]
    outs = pl.pallas_call(
        functools.partial(_layer_kernel, c=c, nlev=nlev, streams=streams, nj=nj, nsteps=nsteps),
        grid=(nsteps + 1,),
        in_specs=[
            pl.BlockSpec((streams, c, d), lambda t: front_idx(t) + (0,)),
            pl.BlockSpec((streams, c, d), lambda t: back_idx(t) + (0,)),
        ] + [const(a) for a in consts] + [
            fstate(GLA_DV, GLA_QK, pipeline_mode=once),
            fstate(M_HEADS, M_DH, 2 * M_DH, pipeline_mode=once),
            fstate(M_HEADS, 1, LANES, pipeline_mode=once),
            bstate(FFN_CONV - 1, d_ff, pipeline_mode=once),
        ],
        out_specs=[
            pl.BlockSpec((streams, c, d), lambda t: back_idx(t) + (0,)),
            fstate(GLA_DV, GLA_QK), fstate(M_HEADS, M_DH, 2 * M_DH), fstate(M_HEADS, 1, LANES),
            bstate(FFN_CONV - 1, d_ff),
        ],
        out_shape=[
            jax.ShapeDtypeStruct((batch, length, d), F32),
            jax.ShapeDtypeStruct((batch, GLA_DV, GLA_QK), F32),
            jax.ShapeDtypeStruct((batch, M_HEADS, M_DH, 2 * M_DH), F32),
            jax.ShapeDtypeStruct((batch, M_HEADS, 1, LANES), F32),
            jax.ShapeDtypeStruct((batch, FFN_CONV - 1, d_ff), F32),
        ],
        scratch_shapes=[
            pltpu.VMEM((streams * c, P_COLS), F32),
            pltpu.VMEM((2, streams * c, D_MIX), F32),
            pltpu.VMEM((streams, GLA_DV, GLA_QK), F32),
            pltpu.VMEM((streams, M_HEADS, M_DH, 2 * M_DH), F32),
            pltpu.VMEM((streams, M_HEADS, 1, LANES), F32),
            pltpu.VMEM((streams, FFN_CONV - 1, d_ff), F32),
        ],
        compiler_params=pltpu.CompilerParams(
            dimension_semantics=("arbitrary",), vmem_limit_bytes=VMEM_LIMIT),
        name="layer",
    )(x, x, *consts, s0t, caug0, m0e, conv_st)
    y, s_t, caug, m_e, conv_new = outs
    s_new = jnp.transpose(s_t.reshape(batch, GLA_DV, GLA_HEADS, GLA_DK), (0, 2, 3, 1))
    return (y, s_new, caug[..., :M_DH], caug[..., M_DH], m_e[:, :, 0, 0], conv_new)


def kernel(x_prompt, x_sample, state_gla, state_mlstm_C, state_mlstm_n, state_mlstm_m, cache_ffn_conv,
           norm1, w_in, w_gla_a, b_gla_a, gla_norm, b_m_i, b_m_f, m_norm, w_out, norm2, w_g, w_u,
           conv_w, conv_b, w_d, norm_f):
    assert w_in.shape[0] == 1, "single-layer stack"
    d = x_prompt.shape[-1]
    w = w_in[0]
    o_glr = 2 * GLA_QK + 2 * GLA_V
    o_m = o_glr + GLA_RANK
    o_if = o_m + 4 * M_W
    w_in_r = jnp.concatenate(
        [w[:, :o_glr], w[:, o_m:o_if], w[:, o_glr:o_m], w[:, o_if:],
         jnp.zeros((d, LANES - GLA_RANK - 2 * M_HEADS), w.dtype)], axis=1).astype(BF16)
    wa_pad = jnp.concatenate(
        [w_gla_a[0], jnp.zeros((LANES - GLA_RANK, GLA_QK), w_gla_a.dtype)], axis=0).astype(BF16)
    b_small = jnp.concatenate(
        [jnp.zeros((GLA_RANK,), F32), b_m_i[0], b_m_f[0],
         jnp.zeros((LANES - GLA_RANK - 2 * M_HEADS,), F32)])[None]
    wts = dict(
        norm1=norm1, w_in_r=w_in_r, wa_pad=wa_pad, b_gla_a=b_gla_a, gla_norm=gla_norm,
        b_small=b_small, m_norm=m_norm, w_out=w_out[0].astype(BF16), norm2=norm2,
        w_g=w_g[0].astype(BF16), w_u=w_u[0].astype(BF16), conv_w=conv_w[0], conv_b=conv_b,
        w_d=w_d[0].astype(BF16), norm_f=norm_f[None])

    bp = x_prompt.shape[0]
    dt = x_prompt.dtype
    d_ff = w_g.shape[-1]
    zp = lambda *s: jnp.zeros((bp,) + s, dt)
    outs_p = _layer(x_prompt, zp(GLA_HEADS, GLA_DK, GLA_DV), zp(M_HEADS, M_DH, M_DH),
                    zp(M_HEADS, M_DH), zp(M_HEADS), zp(FFN_CONV - 1, d_ff), wts, streams=4)
    outs_s = _layer(x_sample, state_gla[0], state_mlstm_C[0], state_mlstm_n[0], state_mlstm_m[0],
                    cache_ffn_conv[0], wts, streams=x_sample.shape[0])
    y_p, *st_p = outs_p
    y_s, *st_s = outs_s
    return (y_p, y_s) + tuple(s[None] for s in st_p) + tuple(s[None] for s in st_s)
```

```python
import functools

import numpy as np
import jax
import jax.numpy as jnp
from jax import lax
from jax.experimental import pallas as pl
from jax.experimental.pallas import tpu as pltpu

F32 = jnp.float32
BF16 = jnp.bfloat16

EPS = 1e-6
CHUNK = 64
GLA_HEADS = 4
GLA_DK = 64
GLA_DV = 128
GLA_RANK = 16
GLA_TAU = 16.0
M_HEADS = 4
M_DH = 128
FFN_CONV = 3
LANES = 128
SUBLANES = 8

GLA_QK = GLA_HEADS * GLA_DK
GLA_V = GLA_HEADS * GLA_DV
M_W = M_HEADS * M_DH
D_MIX = GLA_V + M_W
OFF_GQ = 0
OFF_GK = OFF_GQ + GLA_QK
OFF_GV = OFF_GK + GLA_QK
OFF_GR = OFF_GV + GLA_V
OFF_MQ = OFF_GR + GLA_V
OFF_MK = OFF_MQ + M_W
OFF_MV = OFF_MK + M_W
OFF_MO = OFF_MV + M_W
OFF_SMALL = OFF_MO + M_W
P_COLS = OFF_SMALL + LANES
SM_I = GLA_RANK
SM_F = GLA_RANK + M_HEADS

INPROJ_SLAB = 512
FFN_SLAB = 256
DOWN_SLAB = 256
VMEM_LIMIT = 60 * 1024 * 1024


def _dot(a, b):
    return jnp.dot(a, b, preferred_element_type=F32)


def _dot_nt(a, b):
    return lax.dot_general(a, b, (((1,), (1,)), ((), ())), preferred_element_type=F32)


def _dot_tn(a, b):
    return lax.dot_general(a, b, (((0,), (0,)), ((), ())), preferred_element_type=F32)


def _split_hilo(x):
    hi = x.astype(BF16)
    lo = (x - hi.astype(F32)).astype(BF16)
    return hi, lo


def _log_sigmoid(x):
    return jnp.minimum(x, 0.0) - jnp.log1p(jnp.exp(-jnp.abs(x)))


def _rms(x, g):
    return x * lax.rsqrt(jnp.mean(x * x, axis=-1, keepdims=True) + EPS) * g


def _interleave(*tasks, lead=0):
    for _ in range(lead):
        next(tasks[0])
    alive = list(tasks)
    while alive:
        for task in list(alive):
            try:
                next(task)
            except StopIteration:
                alive.remove(task)


def _mixer_consts(c):
    nlev = int(np.log2(c))
    assert 2 ** nlev == c
    t = np.arange(c)[:, None]
    u = np.arange(c)[None, :]
    blocks = [(u <= t), (u > t)]
    for l in range(1, nlev + 1):
        bsz, half = 2 ** l, 2 ** (l - 1)
        mid = (t // bsz) * bsz + half
        right = (t % bsz) >= half
        blocks.append(np.where(right, (u >= mid) & (u <= t), (u > t) & (u < mid)))
    a = np.concatenate(blocks, axis=0).astype(np.float32)
    a2 = np.concatenate([a, a], axis=1)
    x = np.bitwise_xor(t, u)
    lv = np.where(u > t, -1, np.where(x == 0, 0, np.floor(np.log2(np.maximum(x, 1))) + 1))
    e = np.zeros((LANES, 2 * M_HEADS * LANES), np.float32)
    for h in range(M_HEADS):
        e[SM_I + h, h * LANES:(h + 1) * LANES] = 1.0
        e[SM_F + h, (M_HEADS + h) * LANES:(M_HEADS + h + 1) * LANES] = 1.0
    e2 = np.concatenate([e, e], axis=0)
    lv4 = np.tile(lv, (1, GLA_HEADS))
    bd = (np.arange(GLA_HEADS * c)[:, None] // c == np.arange(GLA_QK)[None, :] // GLA_DK)
    return (jnp.asarray(a2, BF16), jnp.asarray(lv4, jnp.int32), jnp.asarray(e2, BF16),
            jnp.asarray(bd, BF16), nlev)


def _front(xf_ref, n1_ref, win_ref, wa_ref, ba_ref, gn_ref, bsm_ref, mn_ref, a2_ref, lv_ref, e2_ref,
           bd_ref, p_sc, mix_sc, slot, s_sc, c_sc, m_sc, *, c, nlev, streams):
    rows = streams * c
    d = xf_ref.shape[-1]
    h_in = _rms(xf_ref[...].reshape(rows, d), n1_ref[...]).astype(BF16)
    yield
    for lo in range(0, P_COLS, INPROJ_SLAB):
        hi = min(lo + INPROJ_SLAB, P_COLS)
        p_sc[:, lo:hi] = _dot(h_in, win_ref[:, lo:hi])
        yield

    def pcol(g, off, width):
        return p_sc[g * c:(g + 1) * c, off:off + width]

    lv = lv_ref[...]
    lv_is = [lv == l for l in range(nlev + 1)]
    lane = lax.broadcasted_iota(jnp.int32, (c, LANES), 1)
    row = lax.broadcasted_iota(jnp.int32, (c, LANES), 0)
    on_diag = row == lane
    causal = lane <= row
    ones = jnp.ones((c, LANES), F32)
    lane_all = lax.broadcasted_iota(jnp.int32, (rows, LANES), 1)
    is_f = (lane_all >= SM_F) & (lane_all < SM_F + M_HEADS)

    gs = range(streams)
    rs = [slice(g * c, (g + 1) * c) for g in gs]

    small = p_sc[:, OFF_SMALL:OFF_SMALL + LANES]
    xa = _dot(small.astype(BF16), wa_ref[...]) + ba_ref[...]
    la_hi, la_lo = _split_hilo(_log_sigmoid(xa) * (1.0 / GLA_TAU))
    x = small + bsm_ref[...]
    x = jnp.where(is_f, _log_sigmoid(x), x)
    x_hi, x_lo = _split_hilo(x)
    yield
    z = [_dot(a2_ref[...], jnp.concatenate([la_hi[rs[g]], la_lo[rs[g]]], axis=0))
         for g in gs]
    xcum = jnp.concatenate(
        [_dot(a2_ref[0:c, :], jnp.concatenate([x_hi[rs[g]], x_lo[rs[g]]], axis=0)) for g in gs],
        axis=0)
    xc = jnp.where(is_f, xcum, x)
    yield
    xe_all = _dot(jnp.concatenate(_split_hilo(xc), axis=1), e2_ref[...])
    xe = [xe_all[rs[g]] for g in gs]
    ez = [jnp.exp(z[g]) for g in gs]
    yield

    q = [pcol(g, OFF_GQ, GLA_QK) * (GLA_DK ** -0.5) for g in gs]
    k = [pcol(g, OFF_GK, GLA_QK) for g in gs]
    q_in = [(q[g] * ez[g][0:c]).astype(BF16) for g in gs]
    k_dec = [(k[g] * ez[g][c:2 * c]).astype(BF16) for g in gs]
    dec = [ez[g][c - 1:c, :] for g in gs]
    q_lv = [[q[g].astype(BF16)] + [(q[g] * ez[g][(2 + l) * c:(3 + l) * c]).astype(BF16)
                                   for l in range(nlev)] for g in gs]
    k_lv = [[k[g].astype(BF16)] + [(k[g] * ez[g][(2 + l) * c:(3 + l) * c]).astype(BF16)
                                   for l in range(nlev)] for g in gs]
    yield
    bd = bd_ref[...]
    sc = [jnp.zeros((c, GLA_HEADS * c), F32) for g in gs]
    for l in range(nlev + 1):
        for g in gs:
            k_bd = jnp.concatenate([k_lv[g][l]] * GLA_HEADS, axis=0) * bd
            sc[g] = jnp.where(lv_is[l], _dot_nt(q_lv[g][l], k_bd), sc[g])
        yield
    sc = [sc[g].astype(BF16) for g in gs]
    for h in range(GLA_HEADS):
        sl = slice(h * GLA_DK, (h + 1) * GLA_DK)
        for g in gs:
            vb = pcol(g, OFF_GV + h * GLA_DV, GLA_DV).astype(BF16)
            st = s_sc[g, :, sl]
            o = _dot_nt(q_in[g][:, sl], st.astype(BF16)) + _dot(sc[g][:, h * c:(h + 1) * c], vb)
            s_sc[g, :, sl] = dec[g][:, sl] * st + _dot_tn(vb, k_dec[g][:, sl])
            r = pcol(g, OFF_GR + h * GLA_DV, GLA_DV)
            mix_sc[slot, g * c:(g + 1) * c, h * GLA_DV:(h + 1) * GLA_DV] = (
                _rms(o, gn_ref[...]) * (r * jax.nn.sigmoid(r)))
        yield

    for h in range(M_HEADS):
        ie = [xe[g][:, h * LANES:(h + 1) * LANES] for g in gs]
        fe = [xe[g][:, (M_HEADS + h) * LANES:(M_HEADS + h + 1) * LANES] for g in gs]
        a = [ie[g] - fe[g] for g in gs]
        a_row = [jnp.sum(jnp.where(on_diag, a[g], 0.0), axis=0, keepdims=True) for g in gs]
        dm = [jnp.where(causal, fe[g] + a_row[g], -jnp.inf)[:, :c] for g in gs]
        gi = [fe[g] + m_sc[g, h] for g in gs]
        m_t = [jnp.maximum(gi[g], jnp.max(dm[g], axis=-1, keepdims=True)) for g in gs]
        w_int = [jnp.exp(gi[g] - m_t[g]) for g in gs]
        w = [jnp.exp(dm[g] - m_t[g][:, :c]) for g in gs]
        yield
        qh = [pcol(g, OFF_MQ + h * M_DH, M_DH).astype(BF16) for g in gs]
        kf = [pcol(g, OFF_MK + h * M_DH, M_DH) * (M_DH ** -0.5) for g in gs]
        vaug = [jnp.concatenate([pcol(g, OFF_MV + h * M_DH, M_DH), ones], axis=1).astype(BF16)
                for g in gs]
        qk = [(_dot_nt(qh[g], kf[g].astype(BF16)) * w[g]).astype(BF16) for g in gs]
        yield
        caug = [c_sc[g, h] for g in gs]
        nd = [_dot(qk[g], vaug[g])
              + jnp.concatenate([w_int[g], w_int[g]], axis=1) * _dot(qh[g], caug[g].astype(BF16))
              for g in gs]
        for g in gs:
            den = jnp.maximum(jnp.abs(nd[g][:, M_DH:]), jnp.exp(-m_t[g]))
            og = jax.nn.sigmoid(pcol(g, OFF_MO + h * M_DH, M_DH)) * (nd[g][:, :M_DH] / den)
            mix_sc[slot, g * c:(g + 1) * c, GLA_V + h * M_DH:GLA_V + (h + 1) * M_DH] = (
                _rms(og, mn_ref[...]))
        yield
        for g in gs:
            m_c = m_t[g][c - 1:c, :]
            decay = jnp.exp(gi[g][c - 1:c, :] - m_c)
            ws = jnp.exp(a[g] + (fe[g][c - 1:c, :] - m_c))
            kw = (kf[g] * ws).astype(BF16)
            c_sc[g, h] = jnp.concatenate([decay, decay], axis=1) * caug[g] + _dot_tn(kw, vaug[g])
            m_sc[g, h] = m_c
        yield


def _back(xb_ref, mix_sc, slot_prev, wo_ref, n2_ref, wg_ref, wu_ref, cw_ref, cb_ref, wd_ref, nf_ref,
          y_ref, carry_sc, act_sc, *, c, streams):
    rows = streams * c
    d = xb_ref.shape[-1]
    d_ff = wg_ref.shape[1]
    nslab = d_ff // FFN_SLAB
    x1 = xb_ref[...].reshape(rows, d) + _dot(mix_sc[slot_prev].astype(BF16), wo_ref[...])
    h2 = _rms(x1, n2_ref[...]).astype(BF16)
    yield

    def gate_up(s):
        cs = slice(s * FFN_SLAB, (s + 1) * FFN_SLAB)
        return _dot(h2, wg_ref[:, cs]), _dot(h2, wu_ref[:, cs])

    row8 = lax.broadcasted_iota(jnp.int32, (SUBLANES, FFN_SLAB), 0)
    nxt = gate_up(0)
    yield
    for s in range(nslab):
        cs = slice(s * FFN_SLAB, (s + 1) * FFN_SLAB)
        g, u = nxt
        if s + 1 < nslab:
            nxt = gate_up(s + 1)
        yield
        r1 = pltpu.roll(g, 1, axis=0)
        r2 = pltpu.roll(g, 2, axis=0)
        g1, g2 = [], []
        for st in range(streams):
            b0 = st * c
            p0 = carry_sc[st, 0:1, cs]
            p1 = carry_sc[st, 1:2, cs]
            g1 += [jnp.where(row8 == 0, p1, r1[b0:b0 + SUBLANES]), r1[b0 + SUBLANES:b0 + c]]
            g2 += [jnp.where(row8 == 0, p0, jnp.where(row8 == 1, p1, r2[b0:b0 + SUBLANES])),
                   r2[b0 + SUBLANES:b0 + c]]
            carry_sc[st, 0:1, cs] = g[b0 + c - 2:b0 + c - 1, :]
            carry_sc[st, 1:2, cs] = g[b0 + c - 1:b0 + c, :]
        g1 = jnp.concatenate(g1, axis=0)
        g2 = jnp.concatenate(g2, axis=0)
        gc = cb_ref[:, cs] + (cw_ref[0:1, cs] * g2 + cw_ref[1:2, cs] * g1 + cw_ref[2:3, cs] * g)
        yield
        act = (jax.nn.gelu(gc) * u).astype(BF16)
        act_sc[:, cs] = act
        yield
    down = []
    for lo in range(0, d, DOWN_SLAB):
        down.append(_dot(act_sc[...], wd_ref[:, lo:lo + DOWN_SLAB]))
        yield
    y_ref[...] = _rms(x1 + jnp.concatenate(down, axis=1), nf_ref[...]).reshape(streams, c, d)


def _layer_kernel(xf_ref, xb_ref, n1_ref, win_ref, wa_ref, ba_ref, gn_ref, bsm_ref, mn_ref, a2_ref,
                  lv_ref, e2_ref, bd_ref, wo_ref, n2_ref, wg_ref, wu_ref, cw_ref, cb_ref, wd_ref, nf_ref,
                  s0_ref, c0_ref, m0_ref, cst_ref,
                  y_ref, s_out, c_out, m_out, cv_out,
                  p_sc, mix_sc, s_sc, c_sc, m_sc, carry_sc, act_sc, *, c, nlev, streams, nj, nsteps):
    t = pl.program_id(0)
    tf = jnp.minimum(t, nsteps - 1)
    jf = lax.rem(tf, nj)
    tb = jnp.maximum(t - 1, 0)
    jb = lax.rem(tb, nj)
    slot = lax.rem(t, 2)

    @pl.when(t == 0)
    def _():
        mix_sc[1] = jnp.zeros(mix_sc.shape[1:], F32)

    @pl.when(jf == 0)
    def _():
        s_sc[...] = s0_ref[...]
        c_sc[...] = c0_ref[...]
        m_sc[...] = m0_ref[...]

    @pl.when(jb == 0)
    def _():
        carry_sc[...] = cst_ref[...]

    _interleave(
        _front(xf_ref, n1_ref, win_ref, wa_ref, ba_ref, gn_ref, bsm_ref, mn_ref, a2_ref, lv_ref,
               e2_ref, bd_ref, p_sc, mix_sc, slot, s_sc, c_sc, m_sc, c=c, nlev=nlev, streams=streams),
        _back(xb_ref, mix_sc, 1 - slot, wo_ref, n2_ref, wg_ref, wu_ref, cw_ref, cb_ref, wd_ref,
              nf_ref, y_ref, carry_sc, act_sc, c=c, streams=streams),
        lead=1 + pl.cdiv(P_COLS, INPROJ_SLAB))

    @pl.when((jf == nj - 1) & (t < nsteps))
    def _():
        s_out[...] = s_sc[...]
        c_out[...] = c_sc[...]
        m_out[...] = m_sc[...]

    @pl.when((jb == nj - 1) & (t >= 1))
    def _():
        cv_out[...] = carry_sc[...]


def _layer(x, s_gla, c_m, n_m, m_m, conv_st, wts, streams):
    batch, length, d = x.shape
    c = min(CHUNK, length)
    nj = length // c
    nsteps = (batch // streams) * nj
    d_ff = wts["w_g"].shape[1]
    a2, lv, e2, bd, nlev = _mixer_consts(c)
    s0t = jnp.transpose(s_gla, (0, 3, 1, 2)).reshape(batch, GLA_DV, GLA_QK)
    caug0 = jnp.concatenate(
        [c_m, jnp.broadcast_to(n_m[..., None], (batch, M_HEADS, M_DH, M_DH))], axis=-1)
    m0e = jnp.broadcast_to(m_m[..., None, None], (batch, M_HEADS, 1, LANES))

    def front_idx(t):
        tf = jnp.minimum(t, nsteps - 1)
        return tf // nj, tf % nj

    def back_idx(t):
        tb = jnp.maximum(t - 1, 0)
        return tb // nj, tb % nj

    once = pl.Buffered(1)
    const = lambda a: pl.BlockSpec(a.shape, lambda t: (0,) * a.ndim, pipeline_mode=once)
    fstate = lambda *blk, **kw: pl.BlockSpec(
        (streams,) + blk, lambda t: (front_idx(t)[0],) + (0,) * len(blk), **kw)
    bstate = lambda *blk, **kw: pl.BlockSpec(
        (streams,) + blk, lambda t: (back_idx(t)[0],) + (0,) * len(blk), **kw)
    consts = [wts["norm1"], wts["w_in_r"], wts["wa_pad"], wts["b_gla_a"], wts["gla_norm"],
              wts["b_small"], wts["m_norm"], a2, lv, e2, bd, wts["w_out"], wts["norm2"], wts["w_g"],
              wts["w_u"], wts["conv_w"], wts["conv_b"], wts["w_d"], wts["norm_f"]]
    outs = pl.pallas_call(
        functools.partial(_layer_kernel, c=c, nlev=nlev, streams=streams, nj=nj, nsteps=nsteps),
        grid=(nsteps + 1,),
        in_specs=[
            pl.BlockSpec((streams, c, d), lambda t: front_idx(t) + (0,)),
            pl.BlockSpec((streams, c, d), lambda t: back_idx(t) + (0,)),
        ] + [const(a) for a in consts] + [
            fstate(GLA_DV, GLA_QK, pipeline_mode=once),
            fstate(M_HEADS, M_DH, 2 * M_DH, pipeline_mode=once),
            fstate(M_HEADS, 1, LANES, pipeline_mode=once),
            bstate(FFN_CONV - 1, d_ff, pipeline_mode=once),
        ],
        out_specs=[
            pl.BlockSpec((streams, c, d), lambda t: back_idx(t) + (0,)),
            fstate(GLA_DV, GLA_QK), fstate(M_HEADS, M_DH, 2 * M_DH), fstate(M_HEADS, 1, LANES),
            bstate(FFN_CONV - 1, d_ff),
        ],
        out_shape=[
            jax.ShapeDtypeStruct((batch, length, d), F32),
            jax.ShapeDtypeStruct((batch, GLA_DV, GLA_QK), F32),
            jax.ShapeDtypeStruct((batch, M_HEADS, M_DH, 2 * M_DH), F32),
            jax.ShapeDtypeStruct((batch, M_HEADS, 1, LANES), F32),
            jax.ShapeDtypeStruct((batch, FFN_CONV - 1, d_ff), F32),
        ],
        scratch_shapes=[
            pltpu.VMEM((streams * c, P_COLS), F32),
            pltpu.VMEM((2, streams * c, D_MIX), F32),
            pltpu.VMEM((streams, GLA_DV, GLA_QK), F32),
            pltpu.VMEM((streams, M_HEADS, M_DH, 2 * M_DH), F32),
            pltpu.VMEM((streams, M_HEADS, 1, LANES), F32),
            pltpu.VMEM((streams, FFN_CONV - 1, d_ff), F32),
            pltpu.VMEM((streams * c, d_ff), BF16),
        ],
        compiler_params=pltpu.CompilerParams(
            dimension_semantics=("arbitrary",), vmem_limit_bytes=VMEM_LIMIT),
        name="layer",
    )(x, x, *consts, s0t, caug0, m0e, conv_st)
    y, s_t, caug, m_e, conv_new = outs
    s_new = jnp.transpose(s_t.reshape(batch, GLA_DV, GLA_HEADS, GLA_DK), (0, 2, 3, 1))
    return (y, s_new, caug[..., :M_DH], caug[..., M_DH], m_e[:, :, 0, 0], conv_new)


def kernel(x_prompt, x_sample, state_gla, state_mlstm_C, state_mlstm_n, state_mlstm_m, cache_ffn_conv,
           norm1, w_in, w_gla_a, b_gla_a, gla_norm, b_m_i, b_m_f, m_norm, w_out, norm2, w_g, w_u,
           conv_w, conv_b, w_d, norm_f):
    assert w_in.shape[0] == 1, "single-layer stack"
    d = x_prompt.shape[-1]
    w = w_in[0]
    o_glr = 2 * GLA_QK + 2 * GLA_V
    o_m = o_glr + GLA_RANK
    o_if = o_m + 4 * M_W
    w_in_r = jnp.concatenate(
        [w[:, :o_glr], w[:, o_m:o_if], w[:, o_glr:o_m], w[:, o_if:],
         jnp.zeros((d, LANES - GLA_RANK - 2 * M_HEADS), w.dtype)], axis=1).astype(BF16)
    wa_pad = jnp.concatenate(
        [w_gla_a[0], jnp.zeros((LANES - GLA_RANK, GLA_QK), w_gla_a.dtype)], axis=0).astype(BF16)
    b_small = jnp.concatenate(
        [jnp.zeros((GLA_RANK,), F32), b_m_i[0], b_m_f[0],
         jnp.zeros((LANES - GLA_RANK - 2 * M_HEADS,), F32)])[None]
    wts = dict(
        norm1=norm1, w_in_r=w_in_r, wa_pad=wa_pad, b_gla_a=b_gla_a, gla_norm=gla_norm,
        b_small=b_small, m_norm=m_norm, w_out=w_out[0].astype(BF16), norm2=norm2,
        w_g=w_g[0].astype(BF16), w_u=w_u[0].astype(BF16), conv_w=conv_w[0], conv_b=conv_b,
        w_d=w_d[0].astype(BF16), norm_f=norm_f[None])

    bp = x_prompt.shape[0]
    dt = x_prompt.dtype
    d_ff = w_g.shape[-1]
    zp = lambda *s: jnp.zeros((bp,) + s, dt)
    outs_p = _layer(x_prompt, zp(GLA_HEADS, GLA_DK, GLA_DV), zp(M_HEADS, M_DH, M_DH),
                    zp(M_HEADS, M_DH), zp(M_HEADS), zp(FFN_CONV - 1, d_ff), wts, streams=4)
    outs_s = _layer(x_sample, state_gla[0], state_mlstm_C[0], state_mlstm_n[0], state_mlstm_m[0],
                    cache_ffn_conv[0], wts, streams=x_sample.shape[0])
    y_p, *st_p = outs_p
    y_s, *st_s = outs_s
    return (y_p, y_s) + tuple(s[None] for s in st_p) + tuple(s[None] for s in st_s)
```

```python
import functools

import numpy as np
import jax
import jax.numpy as jnp
from jax import lax
from jax.experimental import pallas as pl
from jax.experimental.pallas import tpu as pltpu

F32 = jnp.float32
BF16 = jnp.bfloat16

EPS = 1e-6
CHUNK = 64
GLA_HEADS = 4
GLA_DK = 64
GLA_DV = 128
GLA_RANK = 16
GLA_TAU = 16.0
M_HEADS = 4
M_DH = 128
FFN_CONV = 3
LANES = 128
SUBLANES = 8

GLA_QK = GLA_HEADS * GLA_DK
GLA_V = GLA_HEADS * GLA_DV
M_W = M_HEADS * M_DH
D_MIX = GLA_V + M_W
OFF_GQ = 0
OFF_GK = OFF_GQ + GLA_QK
OFF_GV = OFF_GK + GLA_QK
OFF_GR = OFF_GV + GLA_V
OFF_MQ = OFF_GR + GLA_V
OFF_MK = OFF_MQ + M_W
OFF_MV = OFF_MK + M_W
OFF_MO = OFF_MV + M_W
OFF_SMALL = OFF_MO + M_W
P_COLS = OFF_SMALL + LANES
SM_I = GLA_RANK
SM_F = GLA_RANK + M_HEADS

INPROJ_SLAB = 512
FFN_SLAB = 256
DOWN_SLAB = 256
VMEM_LIMIT = 60 * 1024 * 1024


def _dot(a, b):
    return jnp.dot(a, b, preferred_element_type=F32)


def _dot_nt(a, b):
    return lax.dot_general(a, b, (((1,), (1,)), ((), ())), preferred_element_type=F32)


def _dot_tn(a, b):
    return lax.dot_general(a, b, (((0,), (0,)), ((), ())), preferred_element_type=F32)


def _split_hilo(x):
    hi = x.astype(BF16)
    lo = (x - hi.astype(F32)).astype(BF16)
    return hi, lo


def _log_sigmoid(x):
    return jnp.minimum(x, 0.0) - jnp.log1p(jnp.exp(-jnp.abs(x)))


def _rms(x, g):
    return x * lax.rsqrt(jnp.mean(x * x, axis=-1, keepdims=True) + EPS) * g


def _interleave(tasks, strides):
    alive = list(zip(tasks, strides))
    while alive:
        for item in list(alive):
            task, stride = item
            for _ in range(stride):
                try:
                    next(task)
                except StopIteration:
                    alive.remove(item)
                    break


def _mixer_consts(c):
    nlev = int(np.log2(c))
    assert 2 ** nlev == c
    t = np.arange(c)[:, None]
    u = np.arange(c)[None, :]
    blocks = [(u <= t), (u > t)]
    for l in range(1, nlev + 1):
        bsz, half = 2 ** l, 2 ** (l - 1)
        mid = (t // bsz) * bsz + half
        right = (t % bsz) >= half
        blocks.append(np.where(right, (u >= mid) & (u <= t), (u > t) & (u < mid)))
    a = np.concatenate(blocks, axis=0).astype(np.float32)
    a2 = np.concatenate([a, a], axis=1)
    x = np.bitwise_xor(t, u)
    lv = np.where(u > t, -1, np.where(x == 0, 0, np.floor(np.log2(np.maximum(x, 1))) + 1))
    e = np.zeros((LANES, 2 * M_HEADS * LANES), np.float32)
    for h in range(M_HEADS):
        e[SM_I + h, h * LANES:(h + 1) * LANES] = 1.0
        e[SM_F + h, (M_HEADS + h) * LANES:(M_HEADS + h + 1) * LANES] = 1.0
    e2 = np.concatenate([e, e], axis=0)
    lv4 = np.tile(lv, (1, GLA_HEADS))
    bd = (np.arange(GLA_HEADS * c)[:, None] // c == np.arange(GLA_QK)[None, :] // GLA_DK)
    return (jnp.asarray(a2, BF16), jnp.asarray(lv4, jnp.int32), jnp.asarray(e2, BF16),
            jnp.asarray(bd, BF16), nlev)


def _inproj(xf_ref, n1_ref, win_ref, p_sc, ready, *, c, streams):
    rows = streams * c
    d = xf_ref.shape[-1]
    h_in = _rms(xf_ref[...].reshape(rows, d), n1_ref[...]).astype(BF16)
    for lo in range(0, P_COLS, INPROJ_SLAB):
        hi = min(lo + INPROJ_SLAB, P_COLS)
        p_sc[:, lo:hi] = _dot(h_in, win_ref[:, lo:hi])
        ready.append((lo, hi))
        yield


def _p_reader(p_sc, ready, rs):
    def pcol(g, off, width):
        assert any(lo <= off and off + width <= hi for lo, hi in ready), (off, width)
        return p_sc[rs[g], off:off + width]
    return pcol


def _gla(wa_ref, ba_ref, gn_ref, a2_ref, lv_ref, bd_ref, p_sc, ready, mix_sc, slot, s_sc, *, c, nlev,
         streams):
    gs = range(streams)
    rs = [slice(g * c, (g + 1) * c) for g in gs]
    pcol = _p_reader(p_sc, ready, rs)
    assert (OFF_SMALL, P_COLS) in ready
    small = p_sc[:, OFF_SMALL:OFF_SMALL + LANES]
    xa = _dot(small.astype(BF16), wa_ref[...]) + ba_ref[...]
    la_hi, la_lo = _split_hilo(_log_sigmoid(xa) * (1.0 / GLA_TAU))
    yield
    z = [_dot(a2_ref[...], jnp.concatenate([la_hi[rs[g]], la_lo[rs[g]]], axis=0))
         for g in gs]
    yield
    ez = [jnp.exp(z[g]) for g in gs]
    yield
    q = [pcol(g, OFF_GQ, GLA_QK) * (GLA_DK ** -0.5) for g in gs]
    k = [pcol(g, OFF_GK, GLA_QK) for g in gs]
    q_in = [(q[g] * ez[g][0:c]).astype(BF16) for g in gs]
    k_dec = [(k[g] * ez[g][c:2 * c]).astype(BF16) for g in gs]
    dec = [ez[g][c - 1:c, :] for g in gs]
    q_lv = [[q[g].astype(BF16)] + [(q[g] * ez[g][(2 + l) * c:(3 + l) * c]).astype(BF16)
                                   for l in range(nlev)] for g in gs]
    k_lv = [[k[g].astype(BF16)] + [(k[g] * ez[g][(2 + l) * c:(3 + l) * c]).astype(BF16)
                                   for l in range(nlev)] for g in gs]
    yield
    lv = lv_ref[...]
    bd = bd_ref[...]
    sc = [jnp.zeros((c, GLA_HEADS * c), F32) for g in gs]
    for l in range(nlev + 1):
        for g in gs:
            k_bd = jnp.concatenate([k_lv[g][l]] * GLA_HEADS, axis=0) * bd
            sc[g] = jnp.where(lv == l, _dot_nt(q_lv[g][l], k_bd), sc[g])
        yield
    sc = [sc[g].astype(BF16) for g in gs]
    for h in range(GLA_HEADS):
        sl = slice(h * GLA_DK, (h + 1) * GLA_DK)
        for g in gs:
            vb = pcol(g, OFF_GV + h * GLA_DV, GLA_DV).astype(BF16)
            st = s_sc[g, :, sl]
            o = _dot_nt(q_in[g][:, sl], st.astype(BF16)) + _dot(sc[g][:, h * c:(h + 1) * c], vb)
            s_sc[g, :, sl] = dec[g][:, sl] * st + _dot_tn(vb, k_dec[g][:, sl])
            r = pcol(g, OFF_GR + h * GLA_DV, GLA_DV)
            mix_sc[slot, rs[g], h * GLA_DV:(h + 1) * GLA_DV] = (
                _rms(o, gn_ref[...]) * (r * jax.nn.sigmoid(r)))
        yield


def _mlstm(bsm_ref, mn_ref, a2_ref, e2_ref, p_sc, ready, mix_sc, slot, c_sc, m_sc, *, c, streams):
    rows = streams * c
    gs = range(streams)
    rs = [slice(g * c, (g + 1) * c) for g in gs]
    pcol = _p_reader(p_sc, ready, rs)
    assert (OFF_SMALL, P_COLS) in ready
    lane = lax.broadcasted_iota(jnp.int32, (c, LANES), 1)
    row = lax.broadcasted_iota(jnp.int32, (c, LANES), 0)
    on_diag = row == lane
    causal = lane <= row
    ones = jnp.ones((c, LANES), F32)
    lane_all = lax.broadcasted_iota(jnp.int32, (rows, LANES), 1)
    is_f = (lane_all >= SM_F) & (lane_all < SM_F + M_HEADS)

    x = p_sc[:, OFF_SMALL:OFF_SMALL + LANES] + bsm_ref[...]
    x = jnp.where(is_f, _log_sigmoid(x), x)
    x_hi, x_lo = _split_hilo(x)
    yield
    xcum = jnp.concatenate(
        [_dot(a2_ref[0:c, :], jnp.concatenate([x_hi[rs[g]], x_lo[rs[g]]], axis=0)) for g in gs],
        axis=0)
    xc = jnp.where(is_f, xcum, x)
    yield
    xe_all = _dot(jnp.concatenate(_split_hilo(xc), axis=1), e2_ref[...])
    xe = [xe_all[rs[g]] for g in gs]
    yield
    for h in range(M_HEADS):
        ie = [xe[g][:, h * LANES:(h + 1) * LANES] for g in gs]
        fe = [xe[g][:, (M_HEADS + h) * LANES:(M_HEADS + h + 1) * LANES] for g in gs]
        a = [ie[g] - fe[g] for g in gs]
        a_row = [jnp.sum(jnp.where(on_diag, a[g], 0.0), axis=0, keepdims=True) for g in gs]
        dm = [jnp.where(causal, fe[g] + a_row[g], -jnp.inf)[:, :c] for g in gs]
        gi = [fe[g] + m_sc[g, h] for g in gs]
        m_t = [jnp.maximum(gi[g], jnp.max(dm[g], axis=-1, keepdims=True)) for g in gs]
        w_int = [jnp.exp(gi[g] - m_t[g]) for g in gs]
        w = [jnp.exp(dm[g] - m_t[g][:, :c]) for g in gs]
        yield
        qh = [pcol(g, OFF_MQ + h * M_DH, M_DH).astype(BF16) for g in gs]
        kf = [pcol(g, OFF_MK + h * M_DH, M_DH) * (M_DH ** -0.5) for g in gs]
        vaug = [jnp.concatenate([pcol(g, OFF_MV + h * M_DH, M_DH), ones], axis=1).astype(BF16)
                for g in gs]
        qk = [(_dot_nt(qh[g], kf[g].astype(BF16)) * w[g]).astype(BF16) for g in gs]
        yield
        caug = [c_sc[g, h] for g in gs]
        nd = [_dot(qk[g], vaug[g])
              + jnp.concatenate([w_int[g], w_int[g]], axis=1) * _dot(qh[g], caug[g].astype(BF16))
              for g in gs]
        for g in gs:
            den = jnp.maximum(jnp.abs(nd[g][:, M_DH:]), jnp.exp(-m_t[g]))
            og = jax.nn.sigmoid(pcol(g, OFF_MO + h * M_DH, M_DH)) * (nd[g][:, :M_DH] / den)
            mix_sc[slot, rs[g], GLA_V + h * M_DH:GLA_V + (h + 1) * M_DH] = _rms(og, mn_ref[...])
        yield
        for g in gs:
            m_c = m_t[g][c - 1:c, :]
            decay = jnp.exp(gi[g][c - 1:c, :] - m_c)
            ws = jnp.exp(a[g] + (fe[g][c - 1:c, :] - m_c))
            kw = (kf[g] * ws).astype(BF16)
            c_sc[g, h] = jnp.concatenate([decay, decay], axis=1) * caug[g] + _dot_tn(kw, vaug[g])
            m_sc[g, h] = m_c
        yield


def _back(xb_ref, mix_sc, slot_prev, wo_ref, n2_ref, wg_ref, wu_ref, cw_ref, cb_ref, wd_ref, nf_ref,
          y_ref, carry_sc, act_sc, *, c, streams):
    rows = streams * c
    d = xb_ref.shape[-1]
    d_ff = wg_ref.shape[1]
    nslab = d_ff // FFN_SLAB
    x1 = xb_ref[...].reshape(rows, d) + _dot(mix_sc[slot_prev].astype(BF16), wo_ref[...])
    h2 = _rms(x1, n2_ref[...]).astype(BF16)
    yield

    def gate_up(s):
        cs = slice(s * FFN_SLAB, (s + 1) * FFN_SLAB)
        return _dot(h2, wg_ref[:, cs]), _dot(h2, wu_ref[:, cs])

    row8 = lax.broadcasted_iota(jnp.int32, (SUBLANES, FFN_SLAB), 0)
    nxt = gate_up(0)
    yield
    for s in range(nslab):
        cs = slice(s * FFN_SLAB, (s + 1) * FFN_SLAB)
        g, u = nxt
        if s + 1 < nslab:
            nxt = gate_up(s + 1)
        yield
        r1 = pltpu.roll(g, 1, axis=0)
        r2 = pltpu.roll(g, 2, axis=0)
        g1, g2 = [], []
        for st in range(streams):
            b0 = st * c
            p0 = carry_sc[st, 0:1, cs]
            p1 = carry_sc[st, 1:2, cs]
            g1 += [jnp.where(row8 == 0, p1, r1[b0:b0 + SUBLANES]), r1[b0 + SUBLANES:b0 + c]]
            g2 += [jnp.where(row8 == 0, p0, jnp.where(row8 == 1, p1, r2[b0:b0 + SUBLANES])),
                   r2[b0 + SUBLANES:b0 + c]]
            carry_sc[st, 0:1, cs] = g[b0 + c - 2:b0 + c - 1, :]
            carry_sc[st, 1:2, cs] = g[b0 + c - 1:b0 + c, :]
        g1 = jnp.concatenate(g1, axis=0)
        g2 = jnp.concatenate(g2, axis=0)
        gc = cb_ref[:, cs] + (cw_ref[0:1, cs] * g2 + cw_ref[1:2, cs] * g1 + cw_ref[2:3, cs] * g)
        yield
        act = (jax.nn.gelu(gc) * u).astype(BF16)
        act_sc[:, cs] = act
        yield
    down = []
    for lo in range(0, d, DOWN_SLAB):
        down.append(_dot(act_sc[...], wd_ref[:, lo:lo + DOWN_SLAB]))
        yield
    y_ref[...] = _rms(x1 + jnp.concatenate(down, axis=1), nf_ref[...]).reshape(streams, c, d)


def _layer_kernel(xf_ref, xb_ref, n1_ref, win_ref, wa_ref, ba_ref, gn_ref, bsm_ref, mn_ref, a2_ref,
                  lv_ref, e2_ref, bd_ref, wo_ref, n2_ref, wg_ref, wu_ref, cw_ref, cb_ref, wd_ref, nf_ref,
                  s0_ref, c0_ref, m0_ref, cst_ref,
                  y_ref, s_out, c_out, m_out, cv_out,
                  p_sc, mix_sc, s_sc, c_sc, m_sc, carry_sc, act_sc, *, c, nlev, streams, nj, nsteps):
    t = pl.program_id(0)
    tf = jnp.minimum(t, nsteps - 1)
    jf = lax.rem(tf, nj)
    tb = jnp.maximum(t - 1, 0)
    jb = lax.rem(tb, nj)
    slot = lax.rem(t, 2)

    @pl.when(t == 0)
    def _():
        mix_sc[1] = jnp.zeros(mix_sc.shape[1:], F32)

    @pl.when(jf == 0)
    def _():
        s_sc[...] = s0_ref[...]
        c_sc[...] = c0_ref[...]
        m_sc[...] = m0_ref[...]

    @pl.when(jb == 0)
    def _():
        carry_sc[...] = cst_ref[...]

    back = _back(xb_ref, mix_sc, 1 - slot, wo_ref, n2_ref, wg_ref, wu_ref, cw_ref, cb_ref, wd_ref,
                 nf_ref, y_ref, carry_sc, act_sc, c=c, streams=streams)
    next(back)
    ready = []
    for _ in _inproj(xf_ref, n1_ref, win_ref, p_sc, ready, c=c, streams=streams):
        pass
    _interleave(
        [_gla(wa_ref, ba_ref, gn_ref, a2_ref, lv_ref, bd_ref, p_sc, ready, mix_sc, slot, s_sc,
              c=c, nlev=nlev, streams=streams),
         _mlstm(bsm_ref, mn_ref, a2_ref, e2_ref, p_sc, ready, mix_sc, slot, c_sc, m_sc,
                c=c, streams=streams),
         back],
        strides=(1, 1, 3))

    @pl.when((jf == nj - 1) & (t < nsteps))
    def _():
        s_out[...] = s_sc[...]
        c_out[...] = c_sc[...]
        m_out[...] = m_sc[...]

    @pl.when((jb == nj - 1) & (t >= 1))
    def _():
        cv_out[...] = carry_sc[...]


def _layer(x, s_gla, c_m, n_m, m_m, conv_st, wts, streams):
    batch, length, d = x.shape
    c = min(CHUNK, length)
    nj = length // c
    nsteps = (batch // streams) * nj
    d_ff = wts["w_g"].shape[1]
    a2, lv, e2, bd, nlev = _mixer_consts(c)
    s0t = jnp.transpose(s_gla, (0, 3, 1, 2)).reshape(batch, GLA_DV, GLA_QK)
    caug0 = jnp.concatenate(
        [c_m, jnp.broadcast_to(n_m[..., None], (batch, M_HEADS, M_DH, M_DH))], axis=-1)
    m0e = jnp.broadcast_to(m_m[..., None, None], (batch, M_HEADS, 1, LANES))

    def front_idx(t):
        tf = jnp.minimum(t, nsteps - 1)
        return tf // nj, tf % nj

    def back_idx(t):
        tb = jnp.maximum(t - 1, 0)
        return tb // nj, tb % nj

    once = pl.Buffered(1)
    const = lambda a: pl.BlockSpec(a.shape, lambda t: (0,) * a.ndim, pipeline_mode=once)
    fstate = lambda *blk, **kw: pl.BlockSpec(
        (streams,) + blk, lambda t: (front_idx(t)[0],) + (0,) * len(blk), **kw)
    bstate = lambda *blk, **kw: pl.BlockSpec(
        (streams,) + blk, lambda t: (back_idx(t)[0],) + (0,) * len(blk), **kw)
    consts = [wts["norm1"], wts["w_in_r"], wts["wa_pad"], wts["b_gla_a"], wts["gla_norm"],
              wts["b_small"], wts["m_norm"], a2, lv, e2, bd, wts["w_out"], wts["norm2"], wts["w_g"],
              wts["w_u"], wts["conv_w"], wts["conv_b"], wts["w_d"], wts["norm_f"]]
    outs = pl.pallas_call(
        functools.partial(_layer_kernel, c=c, nlev=nlev, streams=streams, nj=nj, nsteps=nsteps),
        grid=(nsteps + 1,),
        in_specs=[
            pl.BlockSpec((streams, c, d), lambda t: front_idx(t) + (0,)),
            pl.BlockSpec((streams, c, d), lambda t: back_idx(t) + (0,)),
        ] + [const(a) for a in consts] + [
            fstate(GLA_DV, GLA_QK, pipeline_mode=once),
            fstate(M_HEADS, M_DH, 2 * M_DH, pipeline_mode=once),
            fstate(M_HEADS, 1, LANES, pipeline_mode=once),
            bstate(FFN_CONV - 1, d_ff, pipeline_mode=once),
        ],
        out_specs=[
            pl.BlockSpec((streams, c, d), lambda t: back_idx(t) + (0,)),
            fstate(GLA_DV, GLA_QK), fstate(M_HEADS, M_DH, 2 * M_DH), fstate(M_HEADS, 1, LANES),
            bstate(FFN_CONV - 1, d_ff),
        ],
        out_shape=[
            jax.ShapeDtypeStruct((batch, length, d), F32),
            jax.ShapeDtypeStruct((batch, GLA_DV, GLA_QK), F32),
            jax.ShapeDtypeStruct((batch, M_HEADS, M_DH, 2 * M_DH), F32),
            jax.ShapeDtypeStruct((batch, M_HEADS, 1, LANES), F32),
            jax.ShapeDtypeStruct((batch, FFN_CONV - 1, d_ff), F32),
        ],
        scratch_shapes=[
            pltpu.VMEM((streams * c, P_COLS), F32),
            pltpu.VMEM((2, streams * c, D_MIX), F32),
            pltpu.VMEM((streams, GLA_DV, GLA_QK), F32),
            pltpu.VMEM((streams, M_HEADS, M_DH, 2 * M_DH), F32),
            pltpu.VMEM((streams, M_HEADS, 1, LANES), F32),
            pltpu.VMEM((streams, FFN_CONV - 1, d_ff), F32),
            pltpu.VMEM((streams * c, d_ff), BF16),
        ],
        compiler_params=pltpu.CompilerParams(
            dimension_semantics=("arbitrary",), vmem_limit_bytes=VMEM_LIMIT),
        name="layer",
    )(x, x, *consts, s0t, caug0, m0e, conv_st)
    y, s_t, caug, m_e, conv_new = outs
    s_new = jnp.transpose(s_t.reshape(batch, GLA_DV, GLA_HEADS, GLA_DK), (0, 2, 3, 1))
    return (y, s_new, caug[..., :M_DH], caug[..., M_DH], m_e[:, :, 0, 0], conv_new)


def kernel(x_prompt, x_sample, state_gla, state_mlstm_C, state_mlstm_n, state_mlstm_m, cache_ffn_conv,
           norm1, w_in, w_gla_a, b_gla_a, gla_norm, b_m_i, b_m_f, m_norm, w_out, norm2, w_g, w_u,
           conv_w, conv_b, w_d, norm_f):
    assert w_in.shape[0] == 1, "single-layer stack"
    d = x_prompt.shape[-1]
    w = w_in[0]
    o_glr = 2 * GLA_QK + 2 * GLA_V
    o_m = o_glr + GLA_RANK
    o_if = o_m + 4 * M_W
    w_in_r = jnp.concatenate(
        [w[:, :o_glr], w[:, o_m:o_if], w[:, o_glr:o_m], w[:, o_if:],
         jnp.zeros((d, LANES - GLA_RANK - 2 * M_HEADS), w.dtype)], axis=1).astype(BF16)
    wa_pad = jnp.concatenate(
        [w_gla_a[0], jnp.zeros((LANES - GLA_RANK, GLA_QK), w_gla_a.dtype)], axis=0).astype(BF16)
    b_small = jnp.concatenate(
        [jnp.zeros((GLA_RANK,), F32), b_m_i[0], b_m_f[0],
         jnp.zeros((LANES - GLA_RANK - 2 * M_HEADS,), F32)])[None]
    wts = dict(
        norm1=norm1, w_in_r=w_in_r, wa_pad=wa_pad, b_gla_a=b_gla_a, gla_norm=gla_norm,
        b_small=b_small, m_norm=m_norm, w_out=w_out[0].astype(BF16), norm2=norm2,
        w_g=w_g[0].astype(BF16), w_u=w_u[0].astype(BF16), conv_w=conv_w[0], conv_b=conv_b,
        w_d=w_d[0].astype(BF16), norm_f=norm_f[None])

    bp = x_prompt.shape[0]
    dt = x_prompt.dtype
    d_ff = w_g.shape[-1]
    zp = lambda *s: jnp.zeros((bp,) + s, dt)
    outs_p = _layer(x_prompt, zp(GLA_HEADS, GLA_DK, GLA_DV), zp(M_HEADS, M_DH, M_DH),
                    zp(M_HEADS, M_DH), zp(M_HEADS), zp(FFN_CONV - 1, d_ff), wts, streams=4)
    outs_s = _layer(x_sample, state_gla[0], state_mlstm_C[0], state_mlstm_n[0], state_mlstm_m[0],
                    cache_ffn_conv[0], wts, streams=x_sample.shape[0])
    y_p, *st_p = outs_p
    y_s, *st_s = outs_s
    return (y_p, y_s) + tuple(s[None] for s in st_p) + tuple(s[None] for s in st_s)
```

```python
import functools

import numpy as np
import jax
import jax.numpy as jnp
from jax import lax
from jax.experimental import pallas as pl
from jax.experimental.pallas import tpu as pltpu

F32 = jnp.float32
BF16 = jnp.bfloat16

EPS = 1e-6
CHUNK = 64
GLA_HEADS = 4
GLA_DK = 64
GLA_DV = 128
GLA_RANK = 16
GLA_TAU = 16.0
M_HEADS = 4
M_DH = 128
FFN_CONV = 3
LANES = 128
SUBLANES = 8

GLA_QK = GLA_HEADS * GLA_DK
GLA_V = GLA_HEADS * GLA_DV
M_W = M_HEADS * M_DH
D_MIX = GLA_V + M_W
OFF_GQ = 0
OFF_GK = OFF_GQ + GLA_QK
OFF_GV = OFF_GK + GLA_QK
OFF_GR = OFF_GV + GLA_V
OFF_MQ = OFF_GR + GLA_V
OFF_MK = OFF_MQ + M_W
OFF_MV = OFF_MK + M_W
OFF_MO = OFF_MV + M_W
OFF_SMALL = OFF_MO + M_W
P_COLS = OFF_SMALL + LANES
SM_I = GLA_RANK
SM_F = GLA_RANK + M_HEADS

INPROJ_SLAB = 512
FFN_SLAB = 256
DOWN_SLAB = 256
VMEM_LIMIT = 60 * 1024 * 1024


def _dot(a, b):
    return jnp.dot(a, b, preferred_element_type=F32)


def _dot_nt(a, b):
    return lax.dot_general(a, b, (((1,), (1,)), ((), ())), preferred_element_type=F32)


def _dot_tn(a, b):
    return lax.dot_general(a, b, (((0,), (0,)), ((), ())), preferred_element_type=F32)


def _split_hilo(x):
    hi = x.astype(BF16)
    lo = (x - hi.astype(F32)).astype(BF16)
    return hi, lo


def _log_sigmoid(x):
    return jnp.minimum(x, 0.0) - jnp.log1p(jnp.exp(-jnp.abs(x)))


def _rms(x, g):
    return x * lax.rsqrt(jnp.mean(x * x, axis=-1, keepdims=True) + EPS) * g


def _interleave(tasks, strides):
    alive = list(zip(tasks, strides))
    while alive:
        for item in list(alive):
            task, stride = item
            for _ in range(stride):
                try:
                    next(task)
                except StopIteration:
                    alive.remove(item)
                    break


def _mixer_consts(c):
    nlev = int(np.log2(c))
    assert 2 ** nlev == c
    t = np.arange(c)[:, None]
    u = np.arange(c)[None, :]
    blocks = [(u <= t), (u > t)]
    for l in range(1, nlev + 1):
        bsz, half = 2 ** l, 2 ** (l - 1)
        mid = (t // bsz) * bsz + half
        right = (t % bsz) >= half
        blocks.append(np.where(right, (u >= mid) & (u <= t), (u > t) & (u < mid)))
    a = np.concatenate(blocks, axis=0).astype(np.float32)
    a2 = np.concatenate([a, a], axis=1)
    x = np.bitwise_xor(t, u)
    lv = np.where(u > t, -1, np.where(x == 0, 0, np.floor(np.log2(np.maximum(x, 1))) + 1))
    e = np.zeros((LANES, 2 * M_HEADS * LANES), np.float32)
    for h in range(M_HEADS):
        e[SM_I + h, h * LANES:(h + 1) * LANES] = 1.0
        e[SM_F + h, (M_HEADS + h) * LANES:(M_HEADS + h + 1) * LANES] = 1.0
    e2 = np.concatenate([e, e], axis=0)
    lv4 = np.tile(lv, (1, GLA_HEADS))
    bd = (np.arange(GLA_HEADS * c)[:, None] // c == np.arange(GLA_QK)[None, :] // GLA_DK)
    return (jnp.asarray(a2, BF16), jnp.asarray(lv4, jnp.int32), jnp.asarray(e2, BF16),
            jnp.asarray(bd, BF16), nlev)


def _inproj(xf_ref, n1_ref, win_ref, p_sc, ready, *, c, streams):
    rows = streams * c
    d = xf_ref.shape[-1]
    h_in = _rms(xf_ref[...].reshape(rows, d), n1_ref[...]).astype(BF16)
    for lo in range(0, P_COLS, INPROJ_SLAB):
        hi = min(lo + INPROJ_SLAB, P_COLS)
        p_sc[:, lo:hi] = _dot(h_in, win_ref[:, lo:hi])
        ready.append((lo, hi))
        yield


def _p_reader(p_sc, ready, rs):
    def pcol(g, off, width):
        assert any(lo <= off and off + width <= hi for lo, hi in ready), (off, width)
        return p_sc[rs[g], off:off + width]
    return pcol


def _gla(wa_ref, ba_ref, gn_ref, a2_ref, lv_ref, bd_ref, p_sc, ready, mix_sc, slot, s_sc, *, c, nlev,
         streams):
    gs = range(streams)
    rs = [slice(g * c, (g + 1) * c) for g in gs]
    pcol = _p_reader(p_sc, ready, rs)
    assert (OFF_SMALL, P_COLS) in ready
    small = p_sc[:, OFF_SMALL:OFF_SMALL + LANES]
    xa = _dot(small.astype(BF16), wa_ref[...]) + ba_ref[...]
    la_hi, la_lo = _split_hilo(_log_sigmoid(xa) * (1.0 / GLA_TAU))
    yield
    z = [_dot(a2_ref[...], jnp.concatenate([la_hi[rs[g]], la_lo[rs[g]]], axis=0))
         for g in gs]
    yield
    ez = [jnp.exp(z[g]) for g in gs]
    yield
    q = [pcol(g, OFF_GQ, GLA_QK) * (GLA_DK ** -0.5) for g in gs]
    k = [pcol(g, OFF_GK, GLA_QK) for g in gs]
    q_in = [(q[g] * ez[g][0:c]).astype(BF16) for g in gs]
    k_dec = [(k[g] * ez[g][c:2 * c]).astype(BF16) for g in gs]
    dec = [ez[g][c - 1:c, :] for g in gs]
    q_lv = [[q[g].astype(BF16)] + [(q[g] * ez[g][(2 + l) * c:(3 + l) * c]).astype(BF16)
                                   for l in range(nlev)] for g in gs]
    k_lv = [[k[g].astype(BF16)] + [(k[g] * ez[g][(2 + l) * c:(3 + l) * c]).astype(BF16)
                                   for l in range(nlev)] for g in gs]
    yield
    lv = lv_ref[...]
    bd = bd_ref[...]
    sc = [jnp.zeros((c, GLA_HEADS * c), F32) for g in gs]
    for l in range(nlev + 1):
        for g in gs:
            k_bd = jnp.concatenate([k_lv[g][l]] * GLA_HEADS, axis=0) * bd
            sc[g] = jnp.where(lv == l, _dot_nt(q_lv[g][l], k_bd), sc[g])
        yield
    sc = [sc[g].astype(BF16) for g in gs]
    for h in range(GLA_HEADS):
        sl = slice(h * GLA_DK, (h + 1) * GLA_DK)
        for g in gs:
            vb = pcol(g, OFF_GV + h * GLA_DV, GLA_DV).astype(BF16)
            st = s_sc[g, :, sl]
            o = _dot_nt(q_in[g][:, sl], st.astype(BF16)) + _dot(sc[g][:, h * c:(h + 1) * c], vb)
            s_sc[g, :, sl] = dec[g][:, sl] * st + _dot_tn(vb, k_dec[g][:, sl])
            r = pcol(g, OFF_GR + h * GLA_DV, GLA_DV)
            mix_sc[slot, rs[g], h * GLA_DV:(h + 1) * GLA_DV] = (
                _rms(o, gn_ref[...]) * (r * jax.nn.sigmoid(r)))
        yield


def _mlstm(bsm_ref, mn_ref, a2_ref, e2_ref, p_sc, ready, mix_sc, slot, c_sc, m_sc, *, c, streams):
    rows = streams * c
    gs = range(streams)
    rs = [slice(g * c, (g + 1) * c) for g in gs]
    pcol = _p_reader(p_sc, ready, rs)
    assert (OFF_SMALL, P_COLS) in ready
    lane = lax.broadcasted_iota(jnp.int32, (c, LANES), 1)
    row = lax.broadcasted_iota(jnp.int32, (c, LANES), 0)
    on_diag = row == lane
    causal = lane <= row
    ones = jnp.ones((c, LANES), F32)
    lane_all = lax.broadcasted_iota(jnp.int32, (rows, LANES), 1)
    is_f = (lane_all >= SM_F) & (lane_all < SM_F + M_HEADS)

    x = p_sc[:, OFF_SMALL:OFF_SMALL + LANES] + bsm_ref[...]
    x = jnp.where(is_f, _log_sigmoid(x), x)
    x_hi, x_lo = _split_hilo(x)
    yield
    xcum = jnp.concatenate(
        [_dot(a2_ref[0:c, :], jnp.concatenate([x_hi[rs[g]], x_lo[rs[g]]], axis=0)) for g in gs],
        axis=0)
    xc = jnp.where(is_f, xcum, x)
    yield
    xe_all = _dot(jnp.concatenate(_split_hilo(xc), axis=1), e2_ref[...])
    xe = [xe_all[rs[g]] for g in gs]
    yield
    for h in range(M_HEADS):
        ie = [xe[g][:, h * LANES:(h + 1) * LANES] for g in gs]
        fe = [xe[g][:, (M_HEADS + h) * LANES:(M_HEADS + h + 1) * LANES] for g in gs]
        a = [ie[g] - fe[g] for g in gs]
        a_row = [jnp.sum(jnp.where(on_diag, a[g], 0.0), axis=0, keepdims=True) for g in gs]
        dm = [jnp.where(causal, fe[g] + a_row[g], -jnp.inf)[:, :c] for g in gs]
        gi = [fe[g] + m_sc[g, h] for g in gs]
        m_t = [jnp.maximum(gi[g], jnp.max(dm[g], axis=-1, keepdims=True)) for g in gs]
        w_int = [jnp.exp(gi[g] - m_t[g]) for g in gs]
        w = [jnp.exp(dm[g] - m_t[g][:, :c]) for g in gs]
        yield
        qh = [pcol(g, OFF_MQ + h * M_DH, M_DH).astype(BF16) for g in gs]
        kf = [pcol(g, OFF_MK + h * M_DH, M_DH) * (M_DH ** -0.5) for g in gs]
        vaug = [jnp.concatenate([pcol(g, OFF_MV + h * M_DH, M_DH), ones], axis=1).astype(BF16)
                for g in gs]
        qk = [(_dot_nt(qh[g], kf[g].astype(BF16)) * w[g]).astype(BF16) for g in gs]
        yield
        caug = [c_sc[g, h] for g in gs]
        nd = [_dot(qk[g], vaug[g])
              + jnp.concatenate([w_int[g], w_int[g]], axis=1) * _dot(qh[g], caug[g].astype(BF16))
              for g in gs]
        for g in gs:
            den = jnp.maximum(jnp.abs(nd[g][:, M_DH:]), jnp.exp(-m_t[g]))
            og = jax.nn.sigmoid(pcol(g, OFF_MO + h * M_DH, M_DH)) * (nd[g][:, :M_DH] / den)
            mix_sc[slot, rs[g], GLA_V + h * M_DH:GLA_V + (h + 1) * M_DH] = _rms(og, mn_ref[...])
        yield
        for g in gs:
            m_c = m_t[g][c - 1:c, :]
            decay = jnp.exp(gi[g][c - 1:c, :] - m_c)
            ws = jnp.exp(a[g] + (fe[g][c - 1:c, :] - m_c))
            kw = (kf[g] * ws).astype(BF16)
            c_sc[g, h] = jnp.concatenate([decay, decay], axis=1) * caug[g] + _dot_tn(kw, vaug[g])
            m_sc[g, h] = m_c
        yield


def _back(xb_ref, mix_sc, slot_prev, wo_ref, n2_ref, wg_ref, wu_ref, cw_ref, cb_ref, wd_ref, nf_ref,
          y_ref, carry_sc, act_sc, *, c, streams):
    rows = streams * c
    d = xb_ref.shape[-1]
    d_ff = wg_ref.shape[1]
    nslab = d_ff // FFN_SLAB
    x1 = xb_ref[...].reshape(rows, d) + _dot(mix_sc[slot_prev].astype(BF16), wo_ref[...])
    h2 = _rms(x1, n2_ref[...]).astype(BF16)
    yield

    def gate_up(s):
        cs = slice(s * FFN_SLAB, (s + 1) * FFN_SLAB)
        return _dot(h2, wg_ref[:, cs]), _dot(h2, wu_ref[:, cs])

    row8 = lax.broadcasted_iota(jnp.int32, (SUBLANES, FFN_SLAB), 0)
    nxt = gate_up(0)
    yield
    for s in range(nslab):
        cs = slice(s * FFN_SLAB, (s + 1) * FFN_SLAB)
        g, u = nxt
        if s + 1 < nslab:
            nxt = gate_up(s + 1)
        yield
        r1 = pltpu.roll(g, 1, axis=0)
        r2 = pltpu.roll(g, 2, axis=0)
        g1, g2 = [], []
        for st in range(streams):
            b0 = st * c
            p0 = carry_sc[st, 0:1, cs]
            p1 = carry_sc[st, 1:2, cs]
            g1 += [jnp.where(row8 == 0, p1, r1[b0:b0 + SUBLANES]), r1[b0 + SUBLANES:b0 + c]]
            g2 += [jnp.where(row8 == 0, p0, jnp.where(row8 == 1, p1, r2[b0:b0 + SUBLANES])),
                   r2[b0 + SUBLANES:b0 + c]]
            carry_sc[st, 0:1, cs] = g[b0 + c - 2:b0 + c - 1, :]
            carry_sc[st, 1:2, cs] = g[b0 + c - 1:b0 + c, :]
        g1 = jnp.concatenate(g1, axis=0)
        g2 = jnp.concatenate(g2, axis=0)
        gc = cb_ref[:, cs] + (cw_ref[0:1, cs] * g2 + cw_ref[1:2, cs] * g1 + cw_ref[2:3, cs] * g)
        yield
        act = (jax.nn.gelu(gc) * u).astype(BF16)
        act_sc[:, cs] = act
        yield
    down = []
    for lo in range(0, d, DOWN_SLAB):
        down.append(_dot(act_sc[...], wd_ref[:, lo:lo + DOWN_SLAB]))
        yield
    y_ref[...] = _rms(x1 + jnp.concatenate(down, axis=1), nf_ref[...]).reshape(streams, c, d)


def _layer_kernel(xf_ref, xb_ref, n1_ref, win_ref, wa_ref, ba_ref, gn_ref, bsm_ref, mn_ref, a2_ref,
                  lv_ref, e2_ref, bd_ref, wo_ref, n2_ref, wg_ref, wu_ref, cw_ref, cb_ref, wd_ref, nf_ref,
                  s0_ref, c0_ref, m0_ref, cst_ref,
                  y_ref, s_out, c_out, m_out, cv_out,
                  p_sc, mix_sc, s_sc, c_sc, m_sc, carry_sc, act_sc, *, c, nlev, streams, nj, nsteps,
                  skew):
    t = pl.program_id(0)
    tf = jnp.minimum(t, nsteps - 1)
    jf = lax.rem(tf, nj)
    tb = jnp.maximum(t - skew, 0)
    jb = lax.rem(tb, nj)
    slot = lax.rem(t, 2)
    slot_back = 1 - slot if skew else slot

    if skew:
        @pl.when(t == 0)
        def _():
            mix_sc[1] = jnp.zeros(mix_sc.shape[1:], F32)

    @pl.when(jf == 0)
    def _():
        s_sc[...] = s0_ref[...]
        c_sc[...] = c0_ref[...]
        m_sc[...] = m0_ref[...]

    @pl.when(jb == 0)
    def _():
        carry_sc[...] = cst_ref[...]

    back = _back(xb_ref, mix_sc, slot_back, wo_ref, n2_ref, wg_ref, wu_ref, cw_ref, cb_ref, wd_ref,
                 nf_ref, y_ref, carry_sc, act_sc, c=c, streams=streams)
    if skew:
        next(back)
    ready = []
    for _ in _inproj(xf_ref, n1_ref, win_ref, p_sc, ready, c=c, streams=streams):
        pass
    mixers = [_gla(wa_ref, ba_ref, gn_ref, a2_ref, lv_ref, bd_ref, p_sc, ready, mix_sc, slot, s_sc,
                   c=c, nlev=nlev, streams=streams),
              _mlstm(bsm_ref, mn_ref, a2_ref, e2_ref, p_sc, ready, mix_sc, slot, c_sc, m_sc,
                     c=c, streams=streams)]
    if skew:
        _interleave(mixers + [back], strides=(1, 1, 3))
    else:
        _interleave(mixers, strides=(1, 1))
        _interleave([back], strides=(1,))

    @pl.when((jf == nj - 1) & (t < nsteps))
    def _():
        s_out[...] = s_sc[...]
        c_out[...] = c_sc[...]
        m_out[...] = m_sc[...]

    @pl.when((jb == nj - 1) & (t >= skew))
    def _():
        cv_out[...] = carry_sc[...]


def _layer(x, s_gla, c_m, n_m, m_m, conv_st, wts, streams):
    batch, length, d = x.shape
    c = min(CHUNK, length)
    nj = length // c
    nsteps = (batch // streams) * nj
    d_ff = wts["w_g"].shape[1]
    a2, lv, e2, bd, nlev = _mixer_consts(c)
    s0t = jnp.transpose(s_gla, (0, 3, 1, 2)).reshape(batch, GLA_DV, GLA_QK)
    caug0 = jnp.concatenate(
        [c_m, jnp.broadcast_to(n_m[..., None], (batch, M_HEADS, M_DH, M_DH))], axis=-1)
    m0e = jnp.broadcast_to(m_m[..., None, None], (batch, M_HEADS, 1, LANES))

    def front_idx(t):
        tf = jnp.minimum(t, nsteps - 1)
        return tf // nj, tf % nj

    skew = 1 if nsteps > 1 else 0

    def back_idx(t):
        tb = jnp.maximum(t - skew, 0)
        return tb // nj, tb % nj

    once = pl.Buffered(1)
    const = lambda a: pl.BlockSpec(a.shape, lambda t: (0,) * a.ndim, pipeline_mode=once)
    fstate = lambda *blk, **kw: pl.BlockSpec(
        (streams,) + blk, lambda t: (front_idx(t)[0],) + (0,) * len(blk), **kw)
    bstate = lambda *blk, **kw: pl.BlockSpec(
        (streams,) + blk, lambda t: (back_idx(t)[0],) + (0,) * len(blk), **kw)
    consts = [wts["norm1"], wts["w_in_r"], wts["wa_pad"], wts["b_gla_a"], wts["gla_norm"],
              wts["b_small"], wts["m_norm"], a2, lv, e2, bd, wts["w_out"], wts["norm2"], wts["w_g"],
              wts["w_u"], wts["conv_w"], wts["conv_b"], wts["w_d"], wts["norm_f"]]
    outs = pl.pallas_call(
        functools.partial(_layer_kernel, c=c, nlev=nlev, streams=streams, nj=nj, nsteps=nsteps,
                          skew=skew),
        grid=(nsteps + skew,),
        in_specs=[
            pl.BlockSpec((streams, c, d), lambda t: front_idx(t) + (0,)),
            pl.BlockSpec((streams, c, d), lambda t: back_idx(t) + (0,)),
        ] + [const(a) for a in consts] + [
            fstate(GLA_DV, GLA_QK, pipeline_mode=once),
            fstate(M_HEADS, M_DH, 2 * M_DH, pipeline_mode=once),
            fstate(M_HEADS, 1, LANES, pipeline_mode=once),
            bstate(FFN_CONV - 1, d_ff, pipeline_mode=once),
        ],
        out_specs=[
            pl.BlockSpec((streams, c, d), lambda t: back_idx(t) + (0,)),
            fstate(GLA_DV, GLA_QK), fstate(M_HEADS, M_DH, 2 * M_DH), fstate(M_HEADS, 1, LANES),
            bstate(FFN_CONV - 1, d_ff),
        ],
        out_shape=[
            jax.ShapeDtypeStruct((batch, length, d), F32),
            jax.ShapeDtypeStruct((batch, GLA_DV, GLA_QK), F32),
            jax.ShapeDtypeStruct((batch, M_HEADS, M_DH, 2 * M_DH), F32),
            jax.ShapeDtypeStruct((batch, M_HEADS, 1, LANES), F32),
            jax.ShapeDtypeStruct((batch, FFN_CONV - 1, d_ff), F32),
        ],
        scratch_shapes=[
            pltpu.VMEM((streams * c, P_COLS), F32),
            pltpu.VMEM((2, streams * c, D_MIX), F32),
            pltpu.VMEM((streams, GLA_DV, GLA_QK), F32),
            pltpu.VMEM((streams, M_HEADS, M_DH, 2 * M_DH), F32),
            pltpu.VMEM((streams, M_HEADS, 1, LANES), F32),
            pltpu.VMEM((streams, FFN_CONV - 1, d_ff), F32),
            pltpu.VMEM((streams * c, d_ff), BF16),
        ],
        compiler_params=pltpu.CompilerParams(
            dimension_semantics=("arbitrary",), vmem_limit_bytes=VMEM_LIMIT),
        name="layer",
    )(x, x, *consts, s0t, caug0, m0e, conv_st)
    y, s_t, caug, m_e, conv_new = outs
    s_new = jnp.transpose(s_t.reshape(batch, GLA_DV, GLA_HEADS, GLA_DK), (0, 2, 3, 1))
    return (y, s_new, caug[..., :M_DH], caug[..., M_DH], m_e[:, :, 0, 0], conv_new)


def _regroup_kernel(w_ref, o_ref):
    o_glr = OFF_MQ
    o_m = o_glr + GLA_RANK
    o_if = o_m + 4 * M_W
    rows = w_ref.shape[0]
    o_ref[:, :OFF_MQ] = w_ref[:, :o_glr].astype(BF16)
    o_ref[:, OFF_MQ:OFF_SMALL] = w_ref[:, o_m:o_if].astype(BF16)
    o_ref[:, OFF_SMALL:] = jnp.concatenate(
        [w_ref[:, o_glr:o_m], w_ref[:, o_if:],
         jnp.zeros((rows, LANES - GLA_RANK - 2 * M_HEADS), F32)], axis=1).astype(BF16)


def _regroup_w_in(w):
    d = w.shape[0]
    blk = d // 4
    return pl.pallas_call(
        _regroup_kernel,
        grid=(d // blk,),
        in_specs=[pl.BlockSpec((blk, w.shape[1]), lambda i: (i, 0))],
        out_specs=pl.BlockSpec((blk, P_COLS), lambda i: (i, 0)),
        out_shape=jax.ShapeDtypeStruct((d, P_COLS), BF16),
        compiler_params=pltpu.CompilerParams(dimension_semantics=("arbitrary",)),
        name="regroup_w_in",
    )(w)


def kernel(x_prompt, x_sample, state_gla, state_mlstm_C, state_mlstm_n, state_mlstm_m, cache_ffn_conv,
           norm1, w_in, w_gla_a, b_gla_a, gla_norm, b_m_i, b_m_f, m_norm, w_out, norm2, w_g, w_u,
           conv_w, conv_b, w_d, norm_f):
    assert w_in.shape[0] == 1, "single-layer stack"
    w_in_r = _regroup_w_in(w_in[0])
    wa_pad = jnp.concatenate(
        [w_gla_a[0], jnp.zeros((LANES - GLA_RANK, GLA_QK), w_gla_a.dtype)], axis=0).astype(BF16)
    b_small = jnp.concatenate(
        [jnp.zeros((GLA_RANK,), F32), b_m_i[0], b_m_f[0],
         jnp.zeros((LANES - GLA_RANK - 2 * M_HEADS,), F32)])[None]
    wts = dict(
        norm1=norm1, w_in_r=w_in_r, wa_pad=wa_pad, b_gla_a=b_gla_a, gla_norm=gla_norm,
        b_small=b_small, m_norm=m_norm, w_out=w_out[0].astype(BF16), norm2=norm2,
        w_g=w_g[0].astype(BF16), w_u=w_u[0].astype(BF16), conv_w=conv_w[0], conv_b=conv_b,
        w_d=w_d[0].astype(BF16), norm_f=norm_f[None])

    bp = x_prompt.shape[0]
    dt = x_prompt.dtype
    d_ff = w_g.shape[-1]
    zp = lambda *s: jnp.zeros((bp,) + s, dt)
    outs_p = _layer(x_prompt, zp(GLA_HEADS, GLA_DK, GLA_DV), zp(M_HEADS, M_DH, M_DH),
                    zp(M_HEADS, M_DH), zp(M_HEADS), zp(FFN_CONV - 1, d_ff), wts, streams=4)
    outs_s = _layer(x_sample, state_gla[0], state_mlstm_C[0], state_mlstm_n[0], state_mlstm_m[0],
                    cache_ffn_conv[0], wts, streams=x_sample.shape[0])
    y_p, *st_p = outs_p
    y_s, *st_s = outs_s
    return (y_p, y_s) + tuple(s[None] for s in st_p) + tuple(s[None] for s in st_s)
```

```python
import functools

import numpy as np
import jax
import jax.numpy as jnp
from jax import lax
from jax.experimental import pallas as pl
from jax.experimental.pallas import tpu as pltpu

F32 = jnp.float32
BF16 = jnp.bfloat16

EPS = 1e-6
CHUNK = 64
GLA_HEADS = 4
GLA_DK = 64
GLA_DV = 128
GLA_RANK = 16
GLA_TAU = 16.0
M_HEADS = 4
M_DH = 128
FFN_CONV = 3
LANES = 128
SUBLANES = 8

GLA_QK = GLA_HEADS * GLA_DK
GLA_V = GLA_HEADS * GLA_DV
M_W = M_HEADS * M_DH
D_MIX = GLA_V + M_W
OFF_GQ = 0
OFF_GK = OFF_GQ + GLA_QK
OFF_GV = OFF_GK + GLA_QK
OFF_GR = OFF_GV + GLA_V
OFF_MQ = OFF_GR + GLA_V
OFF_MK = OFF_MQ + M_W
OFF_MV = OFF_MK + M_W
OFF_MO = OFF_MV + M_W
OFF_SMALL = OFF_MO + M_W
P_COLS = OFF_SMALL + LANES
SM_I = GLA_RANK
SM_F = GLA_RANK + M_HEADS

INPROJ_SLAB = 512
FFN_SLAB = 256
DOWN_SLAB = 256
VMEM_LIMIT = 60 * 1024 * 1024


def _dot(a, b):
    return jnp.dot(a, b, preferred_element_type=F32)


def _dot_nt(a, b):
    return lax.dot_general(a, b, (((1,), (1,)), ((), ())), preferred_element_type=F32)


def _dot_tn(a, b):
    return lax.dot_general(a, b, (((0,), (0,)), ((), ())), preferred_element_type=F32)


def _split_hilo(x):
    hi = x.astype(BF16)
    lo = (x - hi.astype(F32)).astype(BF16)
    return hi, lo


def _log_sigmoid(x):
    return jnp.minimum(x, 0.0) - jnp.log1p(jnp.exp(-jnp.abs(x)))


def _rms(x, g):
    return x * lax.rsqrt(jnp.mean(x * x, axis=-1, keepdims=True) + EPS) * g


def _interleave(tasks, strides):
    alive = list(zip(tasks, strides))
    while alive:
        for item in list(alive):
            task, stride = item
            for _ in range(stride):
                try:
                    next(task)
                except StopIteration:
                    alive.remove(item)
                    break


def _mixer_consts(c):
    nlev = int(np.log2(c))
    assert 2 ** nlev == c
    t = np.arange(c)[:, None]
    u = np.arange(c)[None, :]
    blocks = [(u <= t), (u > t)]
    for l in range(1, nlev + 1):
        bsz, half = 2 ** l, 2 ** (l - 1)
        mid = (t // bsz) * bsz + half
        right = (t % bsz) >= half
        blocks.append(np.where(right, (u >= mid) & (u <= t), (u > t) & (u < mid)))
    a = np.concatenate(blocks, axis=0).astype(np.float32)
    a2 = np.concatenate([a, a], axis=1)
    x = np.bitwise_xor(t, u)
    lv = np.where(u > t, -1, np.where(x == 0, 0, np.floor(np.log2(np.maximum(x, 1))) + 1))
    e = np.zeros((LANES, 2 * M_HEADS * LANES), np.float32)
    for h in range(M_HEADS):
        e[SM_I + h, h * LANES:(h + 1) * LANES] = 1.0
        e[SM_F + h, (M_HEADS + h) * LANES:(M_HEADS + h + 1) * LANES] = 1.0
    e2 = np.concatenate([e, e], axis=0)
    lv4 = np.tile(lv, (1, GLA_HEADS))
    bd = (np.arange(GLA_HEADS * c)[:, None] // c == np.arange(GLA_QK)[None, :] // GLA_DK)
    return (jnp.asarray(a2, BF16), jnp.asarray(lv4, jnp.int32), jnp.asarray(e2, BF16),
            jnp.asarray(bd, BF16), nlev)


def _inproj(xf_ref, n1_ref, win_ref, wsm_ref, p_sc, ready, *, c, streams):
    rows = streams * c
    d = xf_ref.shape[-1]
    h_in = _rms(xf_ref[...].reshape(rows, d), n1_ref[...]).astype(BF16)
    for lo in range(0, OFF_SMALL, INPROJ_SLAB):
        hi = lo + INPROJ_SLAB
        p_sc[:, lo:hi] = _dot(h_in, win_ref[:, lo:hi])
        ready.append((lo, hi))
        yield
    p_sc[:, OFF_SMALL:] = _dot(h_in, wsm_ref[...])
    ready.append((OFF_SMALL, P_COLS))
    yield


def _p_reader(p_sc, ready, rs):
    def pcol(g, off, width):
        assert any(lo <= off and off + width <= hi for lo, hi in ready), (off, width)
        return p_sc[rs[g], off:off + width]
    return pcol


def _gla(wa_ref, ba_ref, gn_ref, a2_ref, lv_ref, bd_ref, p_sc, ready, mix_sc, slot, s_sc, *, c, nlev,
         streams):
    gs = range(streams)
    rs = [slice(g * c, (g + 1) * c) for g in gs]
    pcol = _p_reader(p_sc, ready, rs)
    assert (OFF_SMALL, P_COLS) in ready
    small = p_sc[:, OFF_SMALL:OFF_SMALL + LANES]
    xa = _dot(small.astype(BF16), wa_ref[...]) + ba_ref[...]
    la_hi, la_lo = _split_hilo(_log_sigmoid(xa) * (1.0 / GLA_TAU))
    yield
    z = [_dot(a2_ref[...], jnp.concatenate([la_hi[rs[g]], la_lo[rs[g]]], axis=0))
         for g in gs]
    yield
    ez = [jnp.exp(z[g]) for g in gs]
    yield
    q = [pcol(g, OFF_GQ, GLA_QK) * (GLA_DK ** -0.5) for g in gs]
    k = [pcol(g, OFF_GK, GLA_QK) for g in gs]
    q_in = [(q[g] * ez[g][0:c]).astype(BF16) for g in gs]
    k_dec = [(k[g] * ez[g][c:2 * c]).astype(BF16) for g in gs]
    dec = [ez[g][c - 1:c, :] for g in gs]
    q_lv = [[q[g].astype(BF16)] + [(q[g] * ez[g][(2 + l) * c:(3 + l) * c]).astype(BF16)
                                   for l in range(nlev)] for g in gs]
    k_lv = [[k[g].astype(BF16)] + [(k[g] * ez[g][(2 + l) * c:(3 + l) * c]).astype(BF16)
                                   for l in range(nlev)] for g in gs]
    yield
    lv = lv_ref[...]
    bd = bd_ref[...]
    sc = [jnp.zeros((c, GLA_HEADS * c), F32) for g in gs]
    for l in range(nlev + 1):
        for g in gs:
            k_bd = jnp.concatenate([k_lv[g][l]] * GLA_HEADS, axis=0) * bd
            sc[g] = jnp.where(lv == l, _dot_nt(q_lv[g][l], k_bd), sc[g])
        yield
    sc = [sc[g].astype(BF16) for g in gs]
    for h in range(GLA_HEADS):
        sl = slice(h * GLA_DK, (h + 1) * GLA_DK)
        for g in gs:
            vb = pcol(g, OFF_GV + h * GLA_DV, GLA_DV).astype(BF16)
            st = s_sc[g, :, sl]
            o = _dot_nt(q_in[g][:, sl], st.astype(BF16)) + _dot(sc[g][:, h * c:(h + 1) * c], vb)
            s_sc[g, :, sl] = dec[g][:, sl] * st + _dot_tn(vb, k_dec[g][:, sl])
            r = pcol(g, OFF_GR + h * GLA_DV, GLA_DV)
            mix_sc[slot, rs[g], h * GLA_DV:(h + 1) * GLA_DV] = (
                _rms(o, gn_ref[...]) * (r * jax.nn.sigmoid(r)))
        yield


def _mlstm(bsm_ref, mn_ref, a2_ref, e2_ref, p_sc, ready, mix_sc, slot, c_sc, m_sc, *, c, streams):
    rows = streams * c
    gs = range(streams)
    rs = [slice(g * c, (g + 1) * c) for g in gs]
    pcol = _p_reader(p_sc, ready, rs)
    assert (OFF_SMALL, P_COLS) in ready
    lane = lax.broadcasted_iota(jnp.int32, (c, LANES), 1)
    row = lax.broadcasted_iota(jnp.int32, (c, LANES), 0)
    on_diag = row == lane
    causal = lane <= row
    ones = jnp.ones((c, LANES), F32)
    lane_all = lax.broadcasted_iota(jnp.int32, (rows, LANES), 1)
    is_f = (lane_all >= SM_F) & (lane_all < SM_F + M_HEADS)

    x = p_sc[:, OFF_SMALL:OFF_SMALL + LANES] + bsm_ref[...]
    x = jnp.where(is_f, _log_sigmoid(x), x)
    x_hi, x_lo = _split_hilo(x)
    yield
    xcum = jnp.concatenate(
        [_dot(a2_ref[0:c, :], jnp.concatenate([x_hi[rs[g]], x_lo[rs[g]]], axis=0)) for g in gs],
        axis=0)
    xc = jnp.where(is_f, xcum, x)
    yield
    xe_all = _dot(jnp.concatenate(_split_hilo(xc), axis=1), e2_ref[...])
    xe = [xe_all[rs[g]] for g in gs]
    yield
    for h in range(M_HEADS):
        ie = [xe[g][:, h * LANES:(h + 1) * LANES] for g in gs]
        fe = [xe[g][:, (M_HEADS + h) * LANES:(M_HEADS + h + 1) * LANES] for g in gs]
        a = [ie[g] - fe[g] for g in gs]
        a_row = [jnp.sum(jnp.where(on_diag, a[g], 0.0), axis=0, keepdims=True) for g in gs]
        dm = [jnp.where(causal, fe[g] + a_row[g], -jnp.inf)[:, :c] for g in gs]
        gi = [fe[g] + m_sc[g, h] for g in gs]
        m_t = [jnp.maximum(gi[g], jnp.max(dm[g], axis=-1, keepdims=True)) for g in gs]
        w_int = [jnp.exp(gi[g] - m_t[g]) for g in gs]
        w = [jnp.exp(dm[g] - m_t[g][:, :c]) for g in gs]
        yield
        qh = [pcol(g, OFF_MQ + h * M_DH, M_DH).astype(BF16) for g in gs]
        kf = [pcol(g, OFF_MK + h * M_DH, M_DH) * (M_DH ** -0.5) for g in gs]
        vaug = [jnp.concatenate([pcol(g, OFF_MV + h * M_DH, M_DH), ones], axis=1).astype(BF16)
                for g in gs]
        qk = [(_dot_nt(qh[g], kf[g].astype(BF16)) * w[g]).astype(BF16) for g in gs]
        yield
        caug = [c_sc[g, h] for g in gs]
        nd = [_dot(qk[g], vaug[g])
              + jnp.concatenate([w_int[g], w_int[g]], axis=1) * _dot(qh[g], caug[g].astype(BF16))
              for g in gs]
        for g in gs:
            den = jnp.maximum(jnp.abs(nd[g][:, M_DH:]), jnp.exp(-m_t[g]))
            og = jax.nn.sigmoid(pcol(g, OFF_MO + h * M_DH, M_DH)) * (nd[g][:, :M_DH] / den)
            mix_sc[slot, rs[g], GLA_V + h * M_DH:GLA_V + (h + 1) * M_DH] = _rms(og, mn_ref[...])
        yield
        for g in gs:
            m_c = m_t[g][c - 1:c, :]
            decay = jnp.exp(gi[g][c - 1:c, :] - m_c)
            ws = jnp.exp(a[g] + (fe[g][c - 1:c, :] - m_c))
            kw = (kf[g] * ws).astype(BF16)
            c_sc[g, h] = jnp.concatenate([decay, decay], axis=1) * caug[g] + _dot_tn(kw, vaug[g])
            m_sc[g, h] = m_c
        yield


def _back(xb_ref, mix_sc, slot_prev, wo_ref, n2_ref, wg_ref, wu_ref, cw_ref, cb_ref, wd_ref, nf_ref,
          y_ref, carry_sc, act_sc, *, c, streams):
    rows = streams * c
    d = xb_ref.shape[-1]
    d_ff = wg_ref.shape[1]
    nslab = d_ff // FFN_SLAB
    x1 = xb_ref[...].reshape(rows, d) + _dot(mix_sc[slot_prev].astype(BF16), wo_ref[...])
    h2 = _rms(x1, n2_ref[...]).astype(BF16)
    yield

    def gate_up(s):
        cs = slice(s * FFN_SLAB, (s + 1) * FFN_SLAB)
        return _dot(h2, wg_ref[:, cs]), _dot(h2, wu_ref[:, cs])

    row8 = lax.broadcasted_iota(jnp.int32, (SUBLANES, FFN_SLAB), 0)
    nxt = gate_up(0)
    yield
    for s in range(nslab):
        cs = slice(s * FFN_SLAB, (s + 1) * FFN_SLAB)
        g, u = nxt
        if s + 1 < nslab:
            nxt = gate_up(s + 1)
        yield
        r1 = pltpu.roll(g, 1, axis=0)
        r2 = pltpu.roll(g, 2, axis=0)
        g1, g2 = [], []
        for st in range(streams):
            b0 = st * c
            p0 = carry_sc[st, 0:1, cs]
            p1 = carry_sc[st, 1:2, cs]
            g1 += [jnp.where(row8 == 0, p1, r1[b0:b0 + SUBLANES]), r1[b0 + SUBLANES:b0 + c]]
            g2 += [jnp.where(row8 == 0, p0, jnp.where(row8 == 1, p1, r2[b0:b0 + SUBLANES])),
                   r2[b0 + SUBLANES:b0 + c]]
            carry_sc[st, 0:1, cs] = g[b0 + c - 2:b0 + c - 1, :]
            carry_sc[st, 1:2, cs] = g[b0 + c - 1:b0 + c, :]
        g1 = jnp.concatenate(g1, axis=0)
        g2 = jnp.concatenate(g2, axis=0)
        gc = cb_ref[:, cs] + (cw_ref[0:1, cs] * g2 + cw_ref[1:2, cs] * g1 + cw_ref[2:3, cs] * g)
        yield
        act = (jax.nn.gelu(gc) * u).astype(BF16)
        act_sc[:, cs] = act
        yield
    down = []
    for lo in range(0, d, DOWN_SLAB):
        down.append(_dot(act_sc[...], wd_ref[:, lo:lo + DOWN_SLAB]))
        yield
    y_ref[...] = _rms(x1 + jnp.concatenate(down, axis=1), nf_ref[...]).reshape(streams, c, d)


def _layer_kernel(xf_ref, xb_ref, n1_ref, win_ref, wsm_ref, wa_ref, ba_ref, gn_ref, bsm_ref, mn_ref, a2_ref,
                  lv_ref, e2_ref, bd_ref, wo_ref, n2_ref, wg_ref, wu_ref, cw_ref, cb_ref, wd_ref, nf_ref,
                  s0_ref, c0_ref, m0_ref, cst_ref,
                  y_ref, s_out, c_out, m_out, cv_out,
                  p_sc, mix_sc, s_sc, c_sc, m_sc, carry_sc, act_sc, *, c, nlev, streams, nj, nsteps,
                  skew):
    t = pl.program_id(0)
    tf = jnp.minimum(t, nsteps - 1)
    jf = lax.rem(tf, nj)
    tb = jnp.maximum(t - skew, 0)
    jb = lax.rem(tb, nj)
    slot = lax.rem(t, 2)
    slot_back = 1 - slot if skew else slot

    if skew:
        @pl.when(t == 0)
        def _():
            mix_sc[1] = jnp.zeros(mix_sc.shape[1:], F32)

    @pl.when(jf == 0)
    def _():
        s_sc[...] = s0_ref[...]
        c_sc[...] = c0_ref[...]
        m_sc[...] = m0_ref[...]

    @pl.when(jb == 0)
    def _():
        carry_sc[...] = cst_ref[...]

    back = _back(xb_ref, mix_sc, slot_back, wo_ref, n2_ref, wg_ref, wu_ref, cw_ref, cb_ref, wd_ref,
                 nf_ref, y_ref, carry_sc, act_sc, c=c, streams=streams)
    if skew:
        next(back)
    ready = []
    for _ in _inproj(xf_ref, n1_ref, win_ref, wsm_ref, p_sc, ready, c=c, streams=streams):
        pass
    mixers = [_gla(wa_ref, ba_ref, gn_ref, a2_ref, lv_ref, bd_ref, p_sc, ready, mix_sc, slot, s_sc,
                   c=c, nlev=nlev, streams=streams),
              _mlstm(bsm_ref, mn_ref, a2_ref, e2_ref, p_sc, ready, mix_sc, slot, c_sc, m_sc,
                     c=c, streams=streams)]
    if skew:
        _interleave(mixers + [back], strides=(1, 1, 3))
    else:
        _interleave(mixers, strides=(1, 1))
        _interleave([back], strides=(1,))

    @pl.when((jf == nj - 1) & (t < nsteps))
    def _():
        s_out[...] = s_sc[...]
        c_out[...] = c_sc[...]
        m_out[...] = m_sc[...]

    @pl.when((jb == nj - 1) & (t >= skew))
    def _():
        cv_out[...] = carry_sc[...]


def _layer(x, s_gla, c_m, n_m, m_m, conv_st, wts, streams):
    batch, length, d = x.shape
    c = min(CHUNK, length)
    nj = length // c
    nsteps = (batch // streams) * nj
    d_ff = wts["w_g"].shape[1]
    a2, lv, e2, bd, nlev = _mixer_consts(c)
    s0t = jnp.transpose(s_gla, (0, 3, 1, 2)).reshape(batch, GLA_DV, GLA_QK)
    caug0 = jnp.concatenate(
        [c_m, jnp.broadcast_to(n_m[..., None], (batch, M_HEADS, M_DH, M_DH))], axis=-1)
    m0e = jnp.broadcast_to(m_m[..., None, None], (batch, M_HEADS, 1, LANES))

    def front_idx(t):
        tf = jnp.minimum(t, nsteps - 1)
        return tf // nj, tf % nj

    skew = 1 if nsteps > 1 else 0

    def back_idx(t):
        tb = jnp.maximum(t - skew, 0)
        return tb // nj, tb % nj

    once = pl.Buffered(1)
    const = lambda a: pl.BlockSpec(a.shape, lambda t: (0,) * a.ndim, pipeline_mode=once)
    fstate = lambda *blk, **kw: pl.BlockSpec(
        (streams,) + blk, lambda t: (front_idx(t)[0],) + (0,) * len(blk), **kw)
    bstate = lambda *blk, **kw: pl.BlockSpec(
        (streams,) + blk, lambda t: (back_idx(t)[0],) + (0,) * len(blk), **kw)
    consts = [wts["norm1"], wts["w_in_r"], wts["w_in_sm"], wts["wa_pad"], wts["b_gla_a"], wts["gla_norm"],
              wts["b_small"], wts["m_norm"], a2, lv, e2, bd, wts["w_out"], wts["norm2"], wts["w_g"],
              wts["w_u"], wts["conv_w"], wts["conv_b"], wts["w_d"], wts["norm_f"]]
    outs = pl.pallas_call(
        functools.partial(_layer_kernel, c=c, nlev=nlev, streams=streams, nj=nj, nsteps=nsteps,
                          skew=skew),
        grid=(nsteps + skew,),
        in_specs=[
            pl.BlockSpec((streams, c, d), lambda t: front_idx(t) + (0,)),
            pl.BlockSpec((streams, c, d), lambda t: back_idx(t) + (0,)),
        ] + [const(a) for a in consts] + [
            fstate(GLA_DV, GLA_QK, pipeline_mode=once),
            fstate(M_HEADS, M_DH, 2 * M_DH, pipeline_mode=once),
            fstate(M_HEADS, 1, LANES, pipeline_mode=once),
            bstate(FFN_CONV - 1, d_ff, pipeline_mode=once),
        ],
        out_specs=[
            pl.BlockSpec((streams, c, d), lambda t: back_idx(t) + (0,)),
            fstate(GLA_DV, GLA_QK), fstate(M_HEADS, M_DH, 2 * M_DH), fstate(M_HEADS, 1, LANES),
            bstate(FFN_CONV - 1, d_ff),
        ],
        out_shape=[
            jax.ShapeDtypeStruct((batch, length, d), F32),
            jax.ShapeDtypeStruct((batch, GLA_DV, GLA_QK), F32),
            jax.ShapeDtypeStruct((batch, M_HEADS, M_DH, 2 * M_DH), F32),
            jax.ShapeDtypeStruct((batch, M_HEADS, 1, LANES), F32),
            jax.ShapeDtypeStruct((batch, FFN_CONV - 1, d_ff), F32),
        ],
        scratch_shapes=[
            pltpu.VMEM((streams * c, P_COLS), F32),
            pltpu.VMEM((2, streams * c, D_MIX), F32),
            pltpu.VMEM((streams, GLA_DV, GLA_QK), F32),
            pltpu.VMEM((streams, M_HEADS, M_DH, 2 * M_DH), F32),
            pltpu.VMEM((streams, M_HEADS, 1, LANES), F32),
            pltpu.VMEM((streams, FFN_CONV - 1, d_ff), F32),
            pltpu.VMEM((streams * c, d_ff), BF16),
        ],
        compiler_params=pltpu.CompilerParams(
            dimension_semantics=("arbitrary",), vmem_limit_bytes=VMEM_LIMIT),
        name="layer",
    )(x, x, *consts, s0t, caug0, m0e, conv_st)
    y, s_t, caug, m_e, conv_new = outs
    s_new = jnp.transpose(s_t.reshape(batch, GLA_DV, GLA_HEADS, GLA_DK), (0, 2, 3, 1))
    return (y, s_new, caug[..., :M_DH], caug[..., M_DH], m_e[:, :, 0, 0], conv_new)


def _regroup_kernel(wg_ref, wm_ref, glr_ref, mif_ref, o_ref, osm_ref, *, n_gla):
    j = pl.program_id(0)

    @pl.when(j < n_gla)
    def _():
        o_ref[...] = wg_ref[...].T.astype(BF16)

    @pl.when(j >= n_gla)
    def _():
        o_ref[...] = wm_ref[...].T.astype(BF16)

    @pl.when(j == 0)
    def _():
        d = wg_ref.shape[1]
        small = jnp.concatenate(
            [glr_ref[...], mif_ref[...], jnp.zeros((LANES - GLA_RANK - 2 * M_HEADS, d), F32)], axis=0)
        osm_ref[...] = small.T.astype(BF16)


def _regroup_w_in(w):
    d = w.shape[0]
    wt = w.T
    o_glr = OFF_MQ
    o_m = o_glr + GLA_RANK
    o_if = o_m + 4 * M_W
    nslab = OFF_SMALL // INPROJ_SLAB
    n_gla = OFF_MQ // INPROJ_SLAB
    rows = lambda n: (pl.Element(n), pl.Element(d))
    return pl.pallas_call(
        functools.partial(_regroup_kernel, n_gla=n_gla),
        grid=(nslab,),
        in_specs=[
            pl.BlockSpec(rows(INPROJ_SLAB), lambda j: (jnp.minimum(j, n_gla - 1) * INPROJ_SLAB, 0)),
            pl.BlockSpec(rows(INPROJ_SLAB), lambda j: (
                (o_m // GLA_RANK + jnp.maximum(j - n_gla, 0) * (INPROJ_SLAB // GLA_RANK)) * GLA_RANK, 0)),
            pl.BlockSpec(rows(GLA_RANK), lambda j: (o_glr, 0)),
            pl.BlockSpec(rows(2 * M_HEADS), lambda j: (o_if, 0)),
        ],
        out_specs=[
            pl.BlockSpec((d, INPROJ_SLAB), lambda j: (0, j)),
            pl.BlockSpec((d, LANES), lambda j: (0, 0)),
        ],
        out_shape=[
            jax.ShapeDtypeStruct((d, OFF_SMALL), BF16),
            jax.ShapeDtypeStruct((d, LANES), BF16),
        ],
        compiler_params=pltpu.CompilerParams(dimension_semantics=("arbitrary",)),
        name="regroup_w_in",
    )(wt, wt, wt, wt)


def kernel(x_prompt, x_sample, state_gla, state_mlstm_C, state_mlstm_n, state_mlstm_m, cache_ffn_conv,
           norm1, w_in, w_gla_a, b_gla_a, gla_norm, b_m_i, b_m_f, m_norm, w_out, norm2, w_g, w_u,
           conv_w, conv_b, w_d, norm_f):
    assert w_in.shape[0] == 1, "single-layer stack"
    w_in_r, w_in_sm = _regroup_w_in(w_in[0])
    wa_pad = jnp.concatenate(
        [w_gla_a[0], jnp.zeros((LANES - GLA_RANK, GLA_QK), w_gla_a.dtype)], axis=0).astype(BF16)
    b_small = jnp.concatenate(
        [jnp.zeros((GLA_RANK,), F32), b_m_i[0], b_m_f[0],
         jnp.zeros((LANES - GLA_RANK - 2 * M_HEADS,), F32)])[None]
    wts = dict(
        norm1=norm1, w_in_r=w_in_r, w_in_sm=w_in_sm, wa_pad=wa_pad, b_gla_a=b_gla_a, gla_norm=gla_norm,
        b_small=b_small, m_norm=m_norm, w_out=w_out[0].astype(BF16), norm2=norm2,
        w_g=w_g[0].astype(BF16), w_u=w_u[0].astype(BF16), conv_w=conv_w[0], conv_b=conv_b,
        w_d=w_d[0].astype(BF16), norm_f=norm_f[None])

    bp = x_prompt.shape[0]
    dt = x_prompt.dtype
    d_ff = w_g.shape[-1]
    zp = lambda *s: jnp.zeros((bp,) + s, dt)
    outs_p = _layer(x_prompt, zp(GLA_HEADS, GLA_DK, GLA_DV), zp(M_HEADS, M_DH, M_DH),
                    zp(M_HEADS, M_DH), zp(M_HEADS), zp(FFN_CONV - 1, d_ff), wts, streams=4)
    outs_s = _layer(x_sample, state_gla[0], state_mlstm_C[0], state_mlstm_n[0], state_mlstm_m[0],
                    cache_ffn_conv[0], wts, streams=x_sample.shape[0])
    y_p, *st_p = outs_p
    y_s, *st_s = outs_s
    return (y_p, y_s) + tuple(s[None] for s in st_p) + tuple(s[None] for s in st_s)
```

```python
import functools

import numpy as np
import jax
import jax.numpy as jnp
from jax import lax
from jax.experimental import pallas as pl
from jax.experimental.pallas import tpu as pltpu

F32 = jnp.float32
BF16 = jnp.bfloat16

EPS = 1e-6
CHUNK = 64
GLA_HEADS = 4
GLA_DK = 64
GLA_DV = 128
GLA_RANK = 16
GLA_TAU = 16.0
M_HEADS = 4
M_DH = 128
FFN_CONV = 3
LANES = 128
SUBLANES = 8

GLA_QK = GLA_HEADS * GLA_DK
GLA_V = GLA_HEADS * GLA_DV
M_W = M_HEADS * M_DH
D_MIX = GLA_V + M_W
OFF_GQ = 0
OFF_GK = OFF_GQ + GLA_QK
OFF_GV = OFF_GK + GLA_QK
OFF_GR = OFF_GV + GLA_V
OFF_MQ = OFF_GR + GLA_V
OFF_MK = OFF_MQ + M_W
OFF_MV = OFF_MK + M_W
OFF_MO = OFF_MV + M_W
OFF_SMALL = OFF_MO + M_W
P_COLS = OFF_SMALL + LANES
SM_I = GLA_RANK
SM_F = GLA_RANK + M_HEADS

INPROJ_SLAB = 512
FFN_SLAB = 256
DOWN_SLAB = 256
VMEM_LIMIT = 60 * 1024 * 1024


def _dot(a, b):
    return jnp.dot(a, b, preferred_element_type=F32)


def _dot_nt(a, b):
    return lax.dot_general(a, b, (((1,), (1,)), ((), ())), preferred_element_type=F32)


def _dot_tn(a, b):
    return lax.dot_general(a, b, (((0,), (0,)), ((), ())), preferred_element_type=F32)


def _split_hilo(x):
    hi = x.astype(BF16)
    lo = (x - hi.astype(F32)).astype(BF16)
    return hi, lo


def _log_sigmoid(x):
    return jnp.minimum(x, 0.0) - jnp.log1p(jnp.exp(-jnp.abs(x)))


def _rms(x, g):
    return x * lax.rsqrt(jnp.mean(x * x, axis=-1, keepdims=True) + EPS) * g


def _interleave(tasks, strides):
    alive = list(zip(tasks, strides))
    while alive:
        for item in list(alive):
            task, stride = item
            for _ in range(stride):
                try:
                    next(task)
                except StopIteration:
                    alive.remove(item)
                    break


def _mixer_consts(c):
    nlev = int(np.log2(c))
    assert 2 ** nlev == c
    t = np.arange(c)[:, None]
    u = np.arange(c)[None, :]
    blocks = [(u <= t), (u > t)]
    for l in range(1, nlev + 1):
        bsz, half = 2 ** l, 2 ** (l - 1)
        mid = (t // bsz) * bsz + half
        right = (t % bsz) >= half
        blocks.append(np.where(right, (u >= mid) & (u <= t), (u > t) & (u < mid)))
    a = np.concatenate(blocks, axis=0).astype(np.float32)
    a2 = np.concatenate([a, a], axis=1)
    x = np.bitwise_xor(t, u)
    lv = np.where(u > t, -1, np.where(x == 0, 0, np.floor(np.log2(np.maximum(x, 1))) + 1))
    e = np.zeros((LANES, 2 * M_HEADS * LANES), np.float32)
    for h in range(M_HEADS):
        e[SM_I + h, h * LANES:(h + 1) * LANES] = 1.0
        e[SM_F + h, (M_HEADS + h) * LANES:(M_HEADS + h + 1) * LANES] = 1.0
    e2 = np.concatenate([e, e], axis=0)
    lv4 = np.tile(lv, (1, GLA_HEADS))
    bd = (np.arange(GLA_HEADS * c)[:, None] // c == np.arange(GLA_QK)[None, :] // GLA_DK)
    return (jnp.asarray(a2, BF16), jnp.asarray(lv4, jnp.int32), jnp.asarray(e2, BF16),
            jnp.asarray(bd, BF16), nlev)


def _inproj(xf_ref, n1_ref, win_ref, wsm_ref, p_sc, ready, *, c, streams):
    rows = streams * c
    d = xf_ref.shape[-1]
    h_in = _rms(xf_ref[...].reshape(rows, d), n1_ref[...]).astype(BF16)
    for lo in range(0, OFF_SMALL, INPROJ_SLAB):
        hi = lo + INPROJ_SLAB
        p_sc[:, lo:hi] = _dot(h_in, win_ref[:, lo:hi])
        ready.append((lo, hi))
        yield
    p_sc[:, OFF_SMALL:] = _dot(h_in, wsm_ref[...])
    ready.append((OFF_SMALL, P_COLS))
    yield


def _p_reader(p_sc, ready, rs):
    def pcol(g, off, width):
        assert any(lo <= off and off + width <= hi for lo, hi in ready), (off, width)
        return p_sc[rs[g], off:off + width]
    return pcol


def _gla(wa_ref, ba_ref, gn_ref, a2_ref, lv_ref, bd_ref, p_sc, ready, mix_sc, slot, s_sc, *, c, nlev,
         streams):
    gs = range(streams)
    rs = [slice(g * c, (g + 1) * c) for g in gs]
    pcol = _p_reader(p_sc, ready, rs)
    assert (OFF_SMALL, P_COLS) in ready
    small = p_sc[:, OFF_SMALL:OFF_SMALL + LANES]
    xa = _dot(small.astype(BF16), wa_ref[...]) + ba_ref[...]
    la_hi, la_lo = _split_hilo(_log_sigmoid(xa) * (1.0 / GLA_TAU))
    yield
    z = [_dot(a2_ref[...], jnp.concatenate([la_hi[rs[g]], la_lo[rs[g]]], axis=0))
         for g in gs]
    yield
    ez = [jnp.exp(z[g]) for g in gs]
    yield
    q = [pcol(g, OFF_GQ, GLA_QK) * (GLA_DK ** -0.5) for g in gs]
    k = [pcol(g, OFF_GK, GLA_QK) for g in gs]
    q_in = [(q[g] * ez[g][0:c]).astype(BF16) for g in gs]
    k_dec = [(k[g] * ez[g][c:2 * c]).astype(BF16) for g in gs]
    dec = [ez[g][c - 1:c, :] for g in gs]
    q_lv = [[q[g].astype(BF16)] + [(q[g] * ez[g][(2 + l) * c:(3 + l) * c]).astype(BF16)
                                   for l in range(nlev)] for g in gs]
    k_lv = [[k[g].astype(BF16)] + [(k[g] * ez[g][(2 + l) * c:(3 + l) * c]).astype(BF16)
                                   for l in range(nlev)] for g in gs]
    yield
    lv = lv_ref[...]
    bd = bd_ref[...]
    sc = [jnp.zeros((c, GLA_HEADS * c), F32) for g in gs]
    for l in range(nlev + 1):
        for g in gs:
            k_bd = jnp.concatenate([k_lv[g][l]] * GLA_HEADS, axis=0) * bd
            sc[g] = jnp.where(lv == l, _dot_nt(q_lv[g][l], k_bd), sc[g])
        yield
    sc = [sc[g].astype(BF16) for g in gs]
    for h in range(GLA_HEADS):
        sl = slice(h * GLA_DK, (h + 1) * GLA_DK)
        for g in gs:
            vb = pcol(g, OFF_GV + h * GLA_DV, GLA_DV).astype(BF16)
            st = s_sc[g, :, sl]
            o = _dot_nt(q_in[g][:, sl], st.astype(BF16)) + _dot(sc[g][:, h * c:(h + 1) * c], vb)
            s_sc[g, :, sl] = dec[g][:, sl] * st + _dot_tn(vb, k_dec[g][:, sl])
            r = pcol(g, OFF_GR + h * GLA_DV, GLA_DV)
            mix_sc[slot, rs[g], h * GLA_DV:(h + 1) * GLA_DV] = (
                _rms(o, gn_ref[...]) * (r * jax.nn.sigmoid(r)))
        yield


def _mlstm(bsm_ref, mn_ref, a2_ref, e2_ref, p_sc, ready, mix_sc, slot, c_sc, m_sc, *, c, streams):
    rows = streams * c
    gs = range(streams)
    rs = [slice(g * c, (g + 1) * c) for g in gs]
    pcol = _p_reader(p_sc, ready, rs)
    assert (OFF_SMALL, P_COLS) in ready
    lane = lax.broadcasted_iota(jnp.int32, (c, LANES), 1)
    row = lax.broadcasted_iota(jnp.int32, (c, LANES), 0)
    on_diag = row == lane
    causal = lane <= row
    ones = jnp.ones((c, LANES), F32)
    lane_all = lax.broadcasted_iota(jnp.int32, (rows, LANES), 1)
    is_f = (lane_all >= SM_F) & (lane_all < SM_F + M_HEADS)

    x = p_sc[:, OFF_SMALL:OFF_SMALL + LANES] + bsm_ref[...]
    x = jnp.where(is_f, _log_sigmoid(x), x)
    x_hi, x_lo = _split_hilo(x)
    yield
    xcum = jnp.concatenate(
        [_dot(a2_ref[0:c, :], jnp.concatenate([x_hi[rs[g]], x_lo[rs[g]]], axis=0)) for g in gs],
        axis=0)
    xc = jnp.where(is_f, xcum, x)
    yield
    xe_all = _dot(jnp.concatenate(_split_hilo(xc), axis=1), e2_ref[...])
    xe = [xe_all[rs[g]] for g in gs]
    yield
    for h in range(M_HEADS):
        ie = [xe[g][:, h * LANES:(h + 1) * LANES] for g in gs]
        fe = [xe[g][:, (M_HEADS + h) * LANES:(M_HEADS + h + 1) * LANES] for g in gs]
        a = [ie[g] - fe[g] for g in gs]
        a_row = [jnp.sum(jnp.where(on_diag, a[g], 0.0), axis=0, keepdims=True) for g in gs]
        dm = [jnp.where(causal, fe[g] + a_row[g], -jnp.inf)[:, :c] for g in gs]
        gi = [fe[g] + m_sc[g, h] for g in gs]
        m_t = [jnp.maximum(gi[g], jnp.max(dm[g], axis=-1, keepdims=True)) for g in gs]
        w_int = [jnp.exp(gi[g] - m_t[g]) for g in gs]
        w = [jnp.exp(dm[g] - m_t[g][:, :c]) for g in gs]
        yield
        qh = [pcol(g, OFF_MQ + h * M_DH, M_DH).astype(BF16) for g in gs]
        kf = [pcol(g, OFF_MK + h * M_DH, M_DH) * (M_DH ** -0.5) for g in gs]
        vaug = [jnp.concatenate([pcol(g, OFF_MV + h * M_DH, M_DH), ones], axis=1).astype(BF16)
                for g in gs]
        qk = [(_dot_nt(qh[g], kf[g].astype(BF16)) * w[g]).astype(BF16) for g in gs]
        yield
        caug = [c_sc[g, h] for g in gs]
        nd = [_dot(qk[g], vaug[g])
              + jnp.concatenate([w_int[g], w_int[g]], axis=1) * _dot(qh[g], caug[g].astype(BF16))
              for g in gs]
        for g in gs:
            den = jnp.maximum(jnp.abs(nd[g][:, M_DH:]), jnp.exp(-m_t[g]))
            og = jax.nn.sigmoid(pcol(g, OFF_MO + h * M_DH, M_DH)) * (nd[g][:, :M_DH] / den)
            mix_sc[slot, rs[g], GLA_V + h * M_DH:GLA_V + (h + 1) * M_DH] = _rms(og, mn_ref[...])
        yield
        for g in gs:
            m_c = m_t[g][c - 1:c, :]
            decay = jnp.exp(gi[g][c - 1:c, :] - m_c)
            ws = jnp.exp(a[g] + (fe[g][c - 1:c, :] - m_c))
            kw = (kf[g] * ws).astype(BF16)
            c_sc[g, h] = jnp.concatenate([decay, decay], axis=1) * caug[g] + _dot_tn(kw, vaug[g])
            m_sc[g, h] = m_c
        yield


def _back(xb_ref, mix_sc, slot_prev, wo_ref, n2_ref, wg_ref, wu_ref, cw_ref, cb_ref, wd_ref, nf_ref,
          y_ref, carry_sc, act_sc, *, c, streams):
    rows = streams * c
    d = xb_ref.shape[-1]
    d_ff = wg_ref.shape[1]
    nslab = d_ff // FFN_SLAB
    x1 = xb_ref[...].reshape(rows, d) + _dot(mix_sc[slot_prev].astype(BF16), wo_ref[...])
    h2 = _rms(x1, n2_ref[...]).astype(BF16)
    yield

    def gate_up(s):
        cs = slice(s * FFN_SLAB, (s + 1) * FFN_SLAB)
        return _dot(h2, wg_ref[:, cs]), _dot(h2, wu_ref[:, cs])

    row8 = lax.broadcasted_iota(jnp.int32, (SUBLANES, FFN_SLAB), 0)
    nxt = gate_up(0)
    yield
    for s in range(nslab):
        cs = slice(s * FFN_SLAB, (s + 1) * FFN_SLAB)
        g, u = nxt
        if s + 1 < nslab:
            nxt = gate_up(s + 1)
        yield
        r1 = pltpu.roll(g, 1, axis=0)
        r2 = pltpu.roll(g, 2, axis=0)
        g1, g2 = [], []
        for st in range(streams):
            b0 = st * c
            p0 = carry_sc[st, 0:1, cs]
            p1 = carry_sc[st, 1:2, cs]
            g1 += [jnp.where(row8 == 0, p1, r1[b0:b0 + SUBLANES]), r1[b0 + SUBLANES:b0 + c]]
            g2 += [jnp.where(row8 == 0, p0, jnp.where(row8 == 1, p1, r2[b0:b0 + SUBLANES])),
                   r2[b0 + SUBLANES:b0 + c]]
            carry_sc[st, 0:1, cs] = g[b0 + c - 2:b0 + c - 1, :]
            carry_sc[st, 1:2, cs] = g[b0 + c - 1:b0 + c, :]
        g1 = jnp.concatenate(g1, axis=0)
        g2 = jnp.concatenate(g2, axis=0)
        gc = cb_ref[:, cs] + (cw_ref[0:1, cs] * g2 + cw_ref[1:2, cs] * g1 + cw_ref[2:3, cs] * g)
        yield
        act = (jax.nn.gelu(gc) * u).astype(BF16)
        act_sc[:, cs] = act
        yield
    down = []
    for lo in range(0, d, DOWN_SLAB):
        down.append(_dot(act_sc[...], wd_ref[:, lo:lo + DOWN_SLAB]))
        yield
    y_ref[...] = _rms(x1 + jnp.concatenate(down, axis=1), nf_ref[...]).reshape(streams, c, d)


def _layer_kernel(xf_ref, xb_ref, n1_ref, win_ref, wsm_ref, wa_ref, ba_ref, gn_ref, bsm_ref, mn_ref, a2_ref,
                  lv_ref, e2_ref, bd_ref, wo_ref, n2_ref, wg_ref, wu_ref, cw_ref, cb_ref, wd_ref, nf_ref,
                  s0_ref, c0_ref, m0_ref, cst_ref,
                  y_ref, s_out, c_out, m_out, cv_out,
                  p_sc, mix_sc, s_sc, c_sc, m_sc, carry_sc, act_sc, *, c, nlev, streams, nj, nsteps,
                  skew):
    t = pl.program_id(0)
    tf = jnp.minimum(t, nsteps - 1)
    jf = lax.rem(tf, nj)
    tb = jnp.maximum(t - skew, 0)
    jb = lax.rem(tb, nj)
    slot = lax.rem(t, 2)
    slot_back = 1 - slot if skew else slot

    if skew:
        @pl.when(t == 0)
        def _():
            mix_sc[1] = jnp.zeros(mix_sc.shape[1:], F32)

    @pl.when(jf == 0)
    def _():
        s_sc[...] = s0_ref[...]
        c_sc[...] = c0_ref[...]
        m_sc[...] = m0_ref[...]

    @pl.when(jb == 0)
    def _():
        carry_sc[...] = cst_ref[...]

    back = _back(xb_ref, mix_sc, slot_back, wo_ref, n2_ref, wg_ref, wu_ref, cw_ref, cb_ref, wd_ref,
                 nf_ref, y_ref, carry_sc, act_sc, c=c, streams=streams)
    if skew:
        next(back)
    ready = []
    for _ in _inproj(xf_ref, n1_ref, win_ref, wsm_ref, p_sc, ready, c=c, streams=streams):
        pass
    mixers = [_gla(wa_ref, ba_ref, gn_ref, a2_ref, lv_ref, bd_ref, p_sc, ready, mix_sc, slot, s_sc,
                   c=c, nlev=nlev, streams=streams),
              _mlstm(bsm_ref, mn_ref, a2_ref, e2_ref, p_sc, ready, mix_sc, slot, c_sc, m_sc,
                     c=c, streams=streams)]
    if skew:
        _interleave(mixers + [back], strides=(1, 1, 3))
    else:
        _interleave(mixers, strides=(1, 1))
        _interleave([back], strides=(1,))

    @pl.when((jf == nj - 1) & (t < nsteps))
    def _():
        s_out[...] = s_sc[...]
        c_out[...] = c_sc[...]
        m_out[...] = m_sc[...]

    @pl.when((jb == nj - 1) & (t >= skew))
    def _():
        cv_out[...] = carry_sc[...]


def _layer(x, s_gla, c_m, n_m, m_m, conv_st, wts, streams):
    batch, length, d = x.shape
    c = min(CHUNK, length)
    nj = length // c
    nsteps = (batch // streams) * nj
    d_ff = wts["w_g"].shape[1]
    a2, lv, e2, bd, nlev = _mixer_consts(c)
    s0t = jnp.transpose(s_gla, (0, 3, 1, 2)).reshape(batch, GLA_DV, GLA_QK)
    caug0 = jnp.concatenate(
        [c_m, jnp.broadcast_to(n_m[..., None], (batch, M_HEADS, M_DH, M_DH))], axis=-1)
    m0e = jnp.broadcast_to(m_m[..., None, None], (batch, M_HEADS, 1, LANES))

    def front_idx(t):
        tf = jnp.minimum(t, nsteps - 1)
        return tf // nj, tf % nj

    skew = 1 if nsteps > 1 else 0

    def back_idx(t):
        tb = jnp.maximum(t - skew, 0)
        return tb // nj, tb % nj

    once = pl.Buffered(1)
    const = lambda a: pl.BlockSpec(a.shape, lambda t: (0,) * a.ndim, pipeline_mode=once)
    fstate = lambda *blk, **kw: pl.BlockSpec(
        (streams,) + blk, lambda t: (front_idx(t)[0],) + (0,) * len(blk), **kw)
    bstate = lambda *blk, **kw: pl.BlockSpec(
        (streams,) + blk, lambda t: (back_idx(t)[0],) + (0,) * len(blk), **kw)
    consts = [wts["norm1"], wts["w_in_r"], wts["w_in_sm"], wts["wa_pad"], wts["b_gla_a"], wts["gla_norm"],
              wts["b_small"], wts["m_norm"], a2, lv, e2, bd, wts["w_out"], wts["norm2"], wts["w_g"],
              wts["w_u"], wts["conv_w"], wts["conv_b"], wts["w_d"], wts["norm_f"]]
    outs = pl.pallas_call(
        functools.partial(_layer_kernel, c=c, nlev=nlev, streams=streams, nj=nj, nsteps=nsteps,
                          skew=skew),
        grid=(nsteps + skew,),
        in_specs=[
            pl.BlockSpec((streams, c, d), lambda t: front_idx(t) + (0,)),
            pl.BlockSpec((streams, c, d), lambda t: back_idx(t) + (0,)),
        ] + [const(a) for a in consts] + [
            fstate(GLA_DV, GLA_QK, pipeline_mode=once),
            fstate(M_HEADS, M_DH, 2 * M_DH, pipeline_mode=once),
            fstate(M_HEADS, 1, LANES, pipeline_mode=once),
            bstate(FFN_CONV - 1, d_ff, pipeline_mode=once),
        ],
        out_specs=[
            pl.BlockSpec((streams, c, d), lambda t: back_idx(t) + (0,)),
            fstate(GLA_DV, GLA_QK), fstate(M_HEADS, M_DH, 2 * M_DH), fstate(M_HEADS, 1, LANES),
            bstate(FFN_CONV - 1, d_ff),
        ],
        out_shape=[
            jax.ShapeDtypeStruct((batch, length, d), F32),
            jax.ShapeDtypeStruct((batch, GLA_DV, GLA_QK), F32),
            jax.ShapeDtypeStruct((batch, M_HEADS, M_DH, 2 * M_DH), F32),
            jax.ShapeDtypeStruct((batch, M_HEADS, 1, LANES), F32),
            jax.ShapeDtypeStruct((batch, FFN_CONV - 1, d_ff), F32),
        ],
        scratch_shapes=[
            pltpu.VMEM((streams * c, P_COLS), F32),
            pltpu.VMEM((2, streams * c, D_MIX), F32),
            pltpu.VMEM((streams, GLA_DV, GLA_QK), F32),
            pltpu.VMEM((streams, M_HEADS, M_DH, 2 * M_DH), F32),
            pltpu.VMEM((streams, M_HEADS, 1, LANES), F32),
            pltpu.VMEM((streams, FFN_CONV - 1, d_ff), F32),
            pltpu.VMEM((streams * c, d_ff), BF16),
        ],
        compiler_params=pltpu.CompilerParams(
            dimension_semantics=("arbitrary",), vmem_limit_bytes=VMEM_LIMIT),
        name="layer",
    )(x, x, *consts, s0t, caug0, m0e, conv_st)
    y, s_t, caug, m_e, conv_new = outs
    s_new = jnp.transpose(s_t.reshape(batch, GLA_DV, GLA_HEADS, GLA_DK), (0, 2, 3, 1))
    return (y, s_new, caug[..., :M_DH], caug[..., M_DH], m_e[:, :, 0, 0], conv_new)


def _regroup_kernel(wg_ref, wm_ref, glr_ref, mif_ref, o_ref, osm_ref, *, n_gla):
    j = pl.program_id(0)

    @pl.when(j < n_gla)
    def _():
        o_ref[...] = wg_ref[...].T.astype(BF16)

    @pl.when(j >= n_gla)
    def _():
        o_ref[...] = wm_ref[...].T.astype(BF16)

    @pl.when(j == 0)
    def _():
        d = wg_ref.shape[1]
        small = jnp.concatenate(
            [glr_ref[...], mif_ref[...], jnp.zeros((LANES - GLA_RANK - 2 * M_HEADS, d), F32)], axis=0)
        osm_ref[...] = small.T.astype(BF16)


def _regroup_w_in(w):
    d = w.shape[0]
    wt = w.T
    o_glr = OFF_MQ
    o_m = o_glr + GLA_RANK
    o_if = o_m + 4 * M_W
    nslab = OFF_SMALL // INPROJ_SLAB
    n_gla = OFF_MQ // INPROJ_SLAB
    rows = lambda n: (pl.Element(n), pl.Element(d))
    return pl.pallas_call(
        functools.partial(_regroup_kernel, n_gla=n_gla),
        grid=(nslab,),
        in_specs=[
            pl.BlockSpec(rows(INPROJ_SLAB), lambda j: (jnp.minimum(j, n_gla - 1) * INPROJ_SLAB, 0)),
            pl.BlockSpec(rows(INPROJ_SLAB), lambda j: (
                (o_m // GLA_RANK + jnp.maximum(j - n_gla, 0) * (INPROJ_SLAB // GLA_RANK)) * GLA_RANK, 0)),
            pl.BlockSpec(rows(GLA_RANK), lambda j: (o_glr, 0)),
            pl.BlockSpec(rows(2 * M_HEADS), lambda j: (o_if, 0)),
        ],
        out_specs=[
            pl.BlockSpec((d, INPROJ_SLAB), lambda j: (0, j)),
            pl.BlockSpec((d, LANES), lambda j: (0, 0)),
        ],
        out_shape=[
            jax.ShapeDtypeStruct((d, OFF_SMALL), BF16),
            jax.ShapeDtypeStruct((d, LANES), BF16),
        ],
        compiler_params=pltpu.CompilerParams(dimension_semantics=("arbitrary",)),
        name="regroup_w_in",
    )(wt, wt, wt, wt)


def kernel(x_prompt, x_sample, state_gla, state_mlstm_C, state_mlstm_n, state_mlstm_m, cache_ffn_conv,
           norm1, w_in, w_gla_a, b_gla_a, gla_norm, b_m_i, b_m_f, m_norm, w_out, norm2, w_g, w_u,
           conv_w, conv_b, w_d, norm_f):
    assert w_in.shape[0] == 1, "single-layer stack"
    w_in_r, w_in_sm = _regroup_w_in(w_in[0])
    wa_pad = jnp.concatenate(
        [w_gla_a[0], jnp.zeros((LANES - GLA_RANK, GLA_QK), w_gla_a.dtype)], axis=0).astype(BF16)
    b_small = jnp.concatenate(
        [jnp.zeros((GLA_RANK,), F32), b_m_i[0], b_m_f[0],
         jnp.zeros((LANES - GLA_RANK - 2 * M_HEADS,), F32)])[None]
    wts = dict(
        norm1=norm1, w_in_r=w_in_r, w_in_sm=w_in_sm, wa_pad=wa_pad, b_gla_a=b_gla_a, gla_norm=gla_norm,
        b_small=b_small, m_norm=m_norm, w_out=w_out[0].astype(BF16), norm2=norm2,
        w_g=w_g[0].astype(BF16), w_u=w_u[0].astype(BF16), conv_w=conv_w[0], conv_b=conv_b,
        w_d=w_d[0].astype(BF16), norm_f=norm_f[None])

    bp = x_prompt.shape[0]
    dt = x_prompt.dtype
    d_ff = w_g.shape[-1]
    zp = lambda *s: jnp.zeros((bp,) + s, dt)
    outs_p = _layer(x_prompt, zp(GLA_HEADS, GLA_DK, GLA_DV), zp(M_HEADS, M_DH, M_DH),
                    zp(M_HEADS, M_DH), zp(M_HEADS), zp(FFN_CONV - 1, d_ff), wts, streams=4)
    outs_s = _layer(x_sample, state_gla[0], state_mlstm_C[0], state_mlstm_n[0], state_mlstm_m[0],
                    cache_ffn_conv[0], wts, streams=4)
    y_p, *st_p = outs_p
    y_s, *st_s = outs_s
    return (y_p, y_s) + tuple(s[None] for s in st_p) + tuple(s[None] for s in st_s)
```

```python
import functools

import numpy as np
import jax
import jax.numpy as jnp
from jax import lax
from jax.experimental import pallas as pl
from jax.experimental.pallas import tpu as pltpu

F32 = jnp.float32
BF16 = jnp.bfloat16

EPS = 1e-6
CHUNK = 64
GLA_HEADS = 4
GLA_DK = 64
GLA_DV = 128
GLA_RANK = 16
GLA_TAU = 16.0
M_HEADS = 4
M_DH = 128
FFN_CONV = 3
LANES = 128
SUBLANES = 8

GLA_QK = GLA_HEADS * GLA_DK
GLA_V = GLA_HEADS * GLA_DV
M_W = M_HEADS * M_DH
D_MIX = GLA_V + M_W
OFF_GQ = 0
OFF_GK = OFF_GQ + GLA_QK
OFF_GV = OFF_GK + GLA_QK
OFF_GR = OFF_GV + GLA_V
OFF_MQ = OFF_GR + GLA_V
OFF_MK = OFF_MQ + M_W
OFF_MV = OFF_MK + M_W
OFF_MO = OFF_MV + M_W
OFF_SMALL = OFF_MO + M_W
P_COLS = OFF_SMALL + LANES
SM_I = GLA_RANK
SM_F = GLA_RANK + M_HEADS

INPROJ_SLAB = 512
FFN_SLAB = 256
DOWN_SLAB = 256
VMEM_LIMIT = 60 * 1024 * 1024
N_WEIGHTS = 20


def _dot(a, b):
    return jnp.dot(a, b, preferred_element_type=F32)


def _dot_nt(a, b):
    return lax.dot_general(a, b, (((1,), (1,)), ((), ())), preferred_element_type=F32)


def _dot_tn(a, b):
    return lax.dot_general(a, b, (((0,), (0,)), ((), ())), preferred_element_type=F32)


def _split_hilo(x):
    hi = x.astype(BF16)
    lo = (x - hi.astype(F32)).astype(BF16)
    return hi, lo


def _log_sigmoid(x):
    return jnp.minimum(x, 0.0) - jnp.log1p(jnp.exp(-jnp.abs(x)))


def _rms(x, g):
    return x * lax.rsqrt(jnp.mean(x * x, axis=-1, keepdims=True) + EPS) * g


def _interleave(tasks, strides):
    alive = list(zip(tasks, strides))
    while alive:
        for item in list(alive):
            task, stride = item
            for _ in range(stride):
                try:
                    next(task)
                except StopIteration:
                    alive.remove(item)
                    break


def _mixer_consts(c):
    nlev = int(np.log2(c))
    assert 2 ** nlev == c
    t = np.arange(c)[:, None]
    u = np.arange(c)[None, :]
    blocks = [(u <= t), (u > t)]
    for l in range(1, nlev + 1):
        bsz, half = 2 ** l, 2 ** (l - 1)
        mid = (t // bsz) * bsz + half
        right = (t % bsz) >= half
        blocks.append(np.where(right, (u >= mid) & (u <= t), (u > t) & (u < mid)))
    a = np.concatenate(blocks, axis=0).astype(np.float32)
    a2 = np.concatenate([a, a], axis=1)
    x = np.bitwise_xor(t, u)
    lv = np.where(u > t, -1, np.where(x == 0, 0, np.floor(np.log2(np.maximum(x, 1))) + 1))
    e = np.zeros((LANES, 2 * M_HEADS * LANES), np.float32)
    for h in range(M_HEADS):
        e[SM_I + h, h * LANES:(h + 1) * LANES] = 1.0
        e[SM_F + h, (M_HEADS + h) * LANES:(M_HEADS + h + 1) * LANES] = 1.0
    e2 = np.concatenate([e, e], axis=0)
    lv4 = np.tile(lv, (1, GLA_HEADS))
    bd = (np.arange(GLA_HEADS * c)[:, None] // c == np.arange(GLA_QK)[None, :] // GLA_DK)
    return (jnp.asarray(a2, BF16), jnp.asarray(lv4, jnp.int32), jnp.asarray(e2, BF16),
            jnp.asarray(bd, BF16), nlev)


def _inproj(xf_ref, n1_ref, win_ref, wsm_ref, p_sc, ready, *, c, streams):
    rows = streams * c
    d = xf_ref.shape[-1]
    h_in = _rms(xf_ref[...].reshape(rows, d), n1_ref[...]).astype(BF16)
    for lo in range(0, OFF_SMALL, INPROJ_SLAB):
        hi = lo + INPROJ_SLAB
        p_sc[:, lo:hi] = _dot(h_in, win_ref[:, lo:hi])
        ready.append((lo, hi))
        yield
    p_sc[:, OFF_SMALL:] = _dot(h_in, wsm_ref[...])
    ready.append((OFF_SMALL, P_COLS))
    yield


def _p_reader(p_sc, ready, rs):
    def pcol(g, off, width):
        assert any(lo <= off and off + width <= hi for lo, hi in ready), (off, width)
        return p_sc[rs[g], off:off + width]
    return pcol


def _gla(wa_ref, ba_ref, gn_ref, a2_ref, lv_ref, bd_ref, p_sc, ready, mix_sc, slot, s_sc, *, c, nlev,
         streams):
    gs = range(streams)
    rs = [slice(g * c, (g + 1) * c) for g in gs]
    pcol = _p_reader(p_sc, ready, rs)
    assert (OFF_SMALL, P_COLS) in ready
    small = p_sc[:, OFF_SMALL:OFF_SMALL + LANES]
    xa = _dot(small.astype(BF16), wa_ref[...]) + ba_ref[...]
    la_hi, la_lo = _split_hilo(_log_sigmoid(xa) * (1.0 / GLA_TAU))
    yield
    z = [_dot(a2_ref[...], jnp.concatenate([la_hi[rs[g]], la_lo[rs[g]]], axis=0))
         for g in gs]
    yield
    ez = [jnp.exp(z[g]) for g in gs]
    yield
    q = [pcol(g, OFF_GQ, GLA_QK) * (GLA_DK ** -0.5) for g in gs]
    k = [pcol(g, OFF_GK, GLA_QK) for g in gs]
    q_in = [(q[g] * ez[g][0:c]).astype(BF16) for g in gs]
    k_dec = [(k[g] * ez[g][c:2 * c]).astype(BF16) for g in gs]
    dec = [ez[g][c - 1:c, :] for g in gs]
    q_lv = [[q[g].astype(BF16)] + [(q[g] * ez[g][(2 + l) * c:(3 + l) * c]).astype(BF16)
                                   for l in range(nlev)] for g in gs]
    k_lv = [[k[g].astype(BF16)] + [(k[g] * ez[g][(2 + l) * c:(3 + l) * c]).astype(BF16)
                                   for l in range(nlev)] for g in gs]
    yield
    lv = lv_ref[...]
    bd = bd_ref[...]
    sc = [jnp.zeros((c, GLA_HEADS * c), F32) for g in gs]
    for l in range(nlev + 1):
        for g in gs:
            k_bd = jnp.concatenate([k_lv[g][l]] * GLA_HEADS, axis=0) * bd
            sc[g] = jnp.where(lv == l, _dot_nt(q_lv[g][l], k_bd), sc[g])
        yield
    sc = [sc[g].astype(BF16) for g in gs]
    for h in range(GLA_HEADS):
        sl = slice(h * GLA_DK, (h + 1) * GLA_DK)
        for g in gs:
            vb = pcol(g, OFF_GV + h * GLA_DV, GLA_DV).astype(BF16)
            st = s_sc[g, :, sl]
            o = _dot_nt(q_in[g][:, sl], st.astype(BF16)) + _dot(sc[g][:, h * c:(h + 1) * c], vb)
            s_sc[g, :, sl] = dec[g][:, sl] * st + _dot_tn(vb, k_dec[g][:, sl])
            r = pcol(g, OFF_GR + h * GLA_DV, GLA_DV)
            mix_sc[slot, rs[g], h * GLA_DV:(h + 1) * GLA_DV] = (
                _rms(o, gn_ref[...]) * (r * jax.nn.sigmoid(r)))
        yield


def _mlstm(bsm_ref, mn_ref, a2_ref, e2_ref, p_sc, ready, mix_sc, slot, c_sc, m_sc, *, c, streams):
    rows = streams * c
    gs = range(streams)
    rs = [slice(g * c, (g + 1) * c) for g in gs]
    pcol = _p_reader(p_sc, ready, rs)
    assert (OFF_SMALL, P_COLS) in ready
    lane = lax.broadcasted_iota(jnp.int32, (c, LANES), 1)
    row = lax.broadcasted_iota(jnp.int32, (c, LANES), 0)
    on_diag = row == lane
    causal = lane <= row
    ones = jnp.ones((c, LANES), F32)
    lane_all = lax.broadcasted_iota(jnp.int32, (rows, LANES), 1)
    is_f = (lane_all >= SM_F) & (lane_all < SM_F + M_HEADS)

    x = p_sc[:, OFF_SMALL:OFF_SMALL + LANES] + bsm_ref[...]
    x = jnp.where(is_f, _log_sigmoid(x), x)
    x_hi, x_lo = _split_hilo(x)
    yield
    xcum = jnp.concatenate(
        [_dot(a2_ref[0:c, :], jnp.concatenate([x_hi[rs[g]], x_lo[rs[g]]], axis=0)) for g in gs],
        axis=0)
    xc = jnp.where(is_f, xcum, x)
    yield
    xe_all = _dot(jnp.concatenate(_split_hilo(xc), axis=1), e2_ref[...])
    xe = [xe_all[rs[g]] for g in gs]
    yield
    for h in range(M_HEADS):
        ie = [xe[g][:, h * LANES:(h + 1) * LANES] for g in gs]
        fe = [xe[g][:, (M_HEADS + h) * LANES:(M_HEADS + h + 1) * LANES] for g in gs]
        a = [ie[g] - fe[g] for g in gs]
        a_row = [jnp.sum(jnp.where(on_diag, a[g], 0.0), axis=0, keepdims=True) for g in gs]
        dm = [jnp.where(causal, fe[g] + a_row[g], -jnp.inf)[:, :c] for g in gs]
        gi = [fe[g] + m_sc[g, h] for g in gs]
        m_t = [jnp.maximum(gi[g], jnp.max(dm[g], axis=-1, keepdims=True)) for g in gs]
        w_int = [jnp.exp(gi[g] - m_t[g]) for g in gs]
        w = [jnp.exp(dm[g] - m_t[g][:, :c]) for g in gs]
        yield
        qh = [pcol(g, OFF_MQ + h * M_DH, M_DH).astype(BF16) for g in gs]
        kf = [pcol(g, OFF_MK + h * M_DH, M_DH) * (M_DH ** -0.5) for g in gs]
        vaug = [jnp.concatenate([pcol(g, OFF_MV + h * M_DH, M_DH), ones], axis=1).astype(BF16)
                for g in gs]
        qk = [(_dot_nt(qh[g], kf[g].astype(BF16)) * w[g]).astype(BF16) for g in gs]
        yield
        caug = [c_sc[g, h] for g in gs]
        nd = [_dot(qk[g], vaug[g])
              + jnp.concatenate([w_int[g], w_int[g]], axis=1) * _dot(qh[g], caug[g].astype(BF16))
              for g in gs]
        for g in gs:
            den = jnp.maximum(jnp.abs(nd[g][:, M_DH:]), jnp.exp(-m_t[g]))
            og = jax.nn.sigmoid(pcol(g, OFF_MO + h * M_DH, M_DH)) * (nd[g][:, :M_DH] / den)
            mix_sc[slot, rs[g], GLA_V + h * M_DH:GLA_V + (h + 1) * M_DH] = _rms(og, mn_ref[...])
        yield
        for g in gs:
            m_c = m_t[g][c - 1:c, :]
            decay = jnp.exp(gi[g][c - 1:c, :] - m_c)
            ws = jnp.exp(a[g] + (fe[g][c - 1:c, :] - m_c))
            kw = (kf[g] * ws).astype(BF16)
            c_sc[g, h] = jnp.concatenate([decay, decay], axis=1) * caug[g] + _dot_tn(kw, vaug[g])
            m_sc[g, h] = m_c
        yield


def _back(xb_ref, mix_sc, slot_prev, wo_ref, n2_ref, wg_ref, wu_ref, cw_ref, cb_ref, wd_ref, nf_ref,
          y_ref, carry_sc, act_sc, *, c, streams):
    rows = streams * c
    d = xb_ref.shape[-1]
    d_ff = wg_ref.shape[1]
    nslab = d_ff // FFN_SLAB
    x1 = xb_ref[...].reshape(rows, d) + _dot(mix_sc[slot_prev].astype(BF16), wo_ref[...])
    h2 = _rms(x1, n2_ref[...]).astype(BF16)
    yield

    def gate_up(s):
        cs = slice(s * FFN_SLAB, (s + 1) * FFN_SLAB)
        return _dot(h2, wg_ref[:, cs]), _dot(h2, wu_ref[:, cs])

    row8 = lax.broadcasted_iota(jnp.int32, (SUBLANES, FFN_SLAB), 0)
    nxt = gate_up(0)
    yield
    for s in range(nslab):
        cs = slice(s * FFN_SLAB, (s + 1) * FFN_SLAB)
        g, u = nxt
        if s + 1 < nslab:
            nxt = gate_up(s + 1)
        yield
        r1 = pltpu.roll(g, 1, axis=0)
        r2 = pltpu.roll(g, 2, axis=0)
        g1, g2 = [], []
        for st in range(streams):
            b0 = st * c
            p0 = carry_sc[st, 0:1, cs]
            p1 = carry_sc[st, 1:2, cs]
            g1 += [jnp.where(row8 == 0, p1, r1[b0:b0 + SUBLANES]), r1[b0 + SUBLANES:b0 + c]]
            g2 += [jnp.where(row8 == 0, p0, jnp.where(row8 == 1, p1, r2[b0:b0 + SUBLANES])),
                   r2[b0 + SUBLANES:b0 + c]]
            carry_sc[st, 0:1, cs] = g[b0 + c - 2:b0 + c - 1, :]
            carry_sc[st, 1:2, cs] = g[b0 + c - 1:b0 + c, :]
        g1 = jnp.concatenate(g1, axis=0)
        g2 = jnp.concatenate(g2, axis=0)
        gc = cb_ref[:, cs] + (cw_ref[0:1, cs] * g2 + cw_ref[1:2, cs] * g1 + cw_ref[2:3, cs] * g)
        yield
        act = (jax.nn.gelu(gc) * u).astype(BF16)
        act_sc[:, cs] = act
        yield
    down = []
    for lo in range(0, d, DOWN_SLAB):
        down.append(_dot(act_sc[...], wd_ref[:, lo:lo + DOWN_SLAB]))
        yield
    y_ref[...] = _rms(x1 + jnp.concatenate(down, axis=1), nf_ref[...]).reshape(streams, c, d)


def _layer_kernel(*refs, c, nlev, streams, nj, nsteps, skew, zero_state):
    n_in = 2 + N_WEIGHTS + (0 if zero_state else 5)
    (xf_ref, xb_ref, n1_ref, win_ref, wsm_ref, wa_ref, ba_ref, gn_ref, bsm_ref, mn_ref, a2_ref, lv_ref,
     e2_ref, bd_ref, wo_ref, n2_ref, wg_ref, wu_ref, cw_ref, cb_ref, wd_ref, nf_ref) = refs[:2 + N_WEIGHTS]
    s0_ref, c0_ref, n0_ref, m0_ref, cst_ref = (None,) * 5 if zero_state else refs[2 + N_WEIGHTS:n_in]
    y_ref, s_out, c_out, n_out, m_out, cv_out = refs[n_in:n_in + 6]
    p_sc, mix_sc, s_sc, c_sc, m_sc, carry_sc, act_sc = refs[n_in + 6:]

    t = pl.program_id(0)
    tf = jnp.minimum(t, nsteps - 1)
    jf = lax.rem(tf, nj)
    tb = jnp.maximum(t - skew, 0)
    jb = lax.rem(tb, nj)
    slot = lax.rem(t, 2)
    slot_back = 1 - slot if skew else slot

    if skew:
        @pl.when(t == 0)
        def _():
            mix_sc[1] = jnp.zeros(mix_sc.shape[1:], F32)

    @pl.when(jf == 0)
    def _():
        if zero_state:
            s_sc[...] = jnp.zeros(s_sc.shape, F32)
            c_sc[...] = jnp.zeros(c_sc.shape, F32)
            m_sc[...] = jnp.zeros(m_sc.shape, F32)
        else:
            s_sc[...] = s0_ref[...]
            c_sc[:, :, :, :M_DH] = c0_ref[...]
            c_sc[:, :, :, M_DH:] = jnp.broadcast_to(n0_ref[...], c0_ref.shape)
            m_sc[...] = m0_ref[...]

    @pl.when(jb == 0)
    def _():
        carry_sc[...] = jnp.zeros(carry_sc.shape, F32) if zero_state else cst_ref[...]

    back = _back(xb_ref, mix_sc, slot_back, wo_ref, n2_ref, wg_ref, wu_ref, cw_ref, cb_ref, wd_ref,
                 nf_ref, y_ref, carry_sc, act_sc, c=c, streams=streams)
    if skew:
        next(back)
    ready = []
    for _ in _inproj(xf_ref, n1_ref, win_ref, wsm_ref, p_sc, ready, c=c, streams=streams):
        pass
    mixers = [_gla(wa_ref, ba_ref, gn_ref, a2_ref, lv_ref, bd_ref, p_sc, ready, mix_sc, slot, s_sc,
                   c=c, nlev=nlev, streams=streams),
              _mlstm(bsm_ref, mn_ref, a2_ref, e2_ref, p_sc, ready, mix_sc, slot, c_sc, m_sc,
                     c=c, streams=streams)]
    if skew:
        _interleave(mixers + [back], strides=(1, 1, 3))
    else:
        _interleave(mixers, strides=(1, 1))
        _interleave([back], strides=(1,))

    @pl.when((jf == nj - 1) & (t < nsteps))
    def _():
        s_out[...] = s_sc[...]
        c_out[...] = c_sc[:, :, :, :M_DH]
        on_diag = (lax.broadcasted_iota(jnp.int32, (M_DH, M_DH), 0)
                   == lax.broadcasted_iota(jnp.int32, (M_DH, M_DH), 1))
        for g in range(streams):
            for h in range(M_HEADS):
                n_out[g, h] = jnp.sum(jnp.where(on_diag, c_sc[g, h, :, M_DH:], 0.0), axis=0, keepdims=True)
        m_out[...] = m_sc[...]

    @pl.when((jb == nj - 1) & (t >= skew))
    def _():
        cv_out[...] = carry_sc[...]


def _layer(x, states, wts, streams):
    batch, length, d = x.shape
    c = min(CHUNK, length)
    nj = length // c
    nsteps = (batch // streams) * nj
    d_ff = wts["w_g"].shape[1]
    a2, lv, e2, bd, nlev = _mixer_consts(c)
    skew = 1 if nsteps > 1 else 0

    def front_idx(t):
        tf = jnp.minimum(t, nsteps - 1)
        return tf // nj, tf % nj

    def back_idx(t):
        tb = jnp.maximum(t - skew, 0)
        return tb // nj, tb % nj

    once = pl.Buffered(1)
    const = lambda a: pl.BlockSpec(a.shape, lambda t: (0,) * a.ndim, pipeline_mode=once)
    fstate = lambda *blk, **kw: pl.BlockSpec(
        (streams,) + blk, lambda t: (front_idx(t)[0],) + (0,) * len(blk), **kw)
    bstate = lambda *blk, **kw: pl.BlockSpec(
        (streams,) + blk, lambda t: (back_idx(t)[0],) + (0,) * len(blk), **kw)
    consts = [wts["norm1"], wts["w_in_r"], wts["w_in_sm"], wts["wa_pad"], wts["b_gla_a"], wts["gla_norm"],
              wts["b_small"], wts["m_norm"], a2, lv, e2, bd, wts["w_out"], wts["norm2"], wts["w_g"],
              wts["w_u"], wts["conv_w"], wts["conv_b"], wts["w_d"], wts["norm_f"]]
    assert len(consts) == N_WEIGHTS
    state_ops, state_specs = [], []
    if states is not None:
        s_gla, c_m, n_m, m_m, conv_st = states
        state_ops = [
            jnp.transpose(s_gla, (0, 3, 1, 2)).reshape(batch, GLA_DV, GLA_QK),
            c_m, n_m[..., None],
            jnp.broadcast_to(m_m[..., None, None], (batch, M_HEADS, 1, LANES)), conv_st]
        state_specs = [
            fstate(GLA_DV, GLA_QK, pipeline_mode=once),
            fstate(M_HEADS, M_DH, M_DH, pipeline_mode=once),
            fstate(M_HEADS, M_DH, 1, pipeline_mode=once),
            fstate(M_HEADS, 1, LANES, pipeline_mode=once),
            bstate(FFN_CONV - 1, d_ff, pipeline_mode=once)]
    outs = pl.pallas_call(
        functools.partial(_layer_kernel, c=c, nlev=nlev, streams=streams, nj=nj, nsteps=nsteps,
                          skew=skew, zero_state=states is None),
        grid=(nsteps + skew,),
        in_specs=[
            pl.BlockSpec((streams, c, d), lambda t: front_idx(t) + (0,)),
            pl.BlockSpec((streams, c, d), lambda t: back_idx(t) + (0,)),
        ] + [const(a) for a in consts] + state_specs,
        out_specs=[
            pl.BlockSpec((streams, c, d), lambda t: back_idx(t) + (0,)),
            fstate(GLA_DV, GLA_QK), fstate(M_HEADS, M_DH, M_DH), fstate(M_HEADS, 1, M_DH),
            fstate(M_HEADS, 1, LANES), bstate(FFN_CONV - 1, d_ff),
        ],
        out_shape=[
            jax.ShapeDtypeStruct((batch, length, d), F32),
            jax.ShapeDtypeStruct((batch, GLA_DV, GLA_QK), F32),
            jax.ShapeDtypeStruct((batch, M_HEADS, M_DH, M_DH), F32),
            jax.ShapeDtypeStruct((batch, M_HEADS, 1, M_DH), F32),
            jax.ShapeDtypeStruct((batch, M_HEADS, 1, LANES), F32),
            jax.ShapeDtypeStruct((batch, FFN_CONV - 1, d_ff), F32),
        ],
        scratch_shapes=[
            pltpu.VMEM((streams * c, P_COLS), F32),
            pltpu.VMEM((2, streams * c, D_MIX), F32),
            pltpu.VMEM((streams, GLA_DV, GLA_QK), F32),
            pltpu.VMEM((streams, M_HEADS, M_DH, 2 * M_DH), F32),
            pltpu.VMEM((streams, M_HEADS, 1, LANES), F32),
            pltpu.VMEM((streams, FFN_CONV - 1, d_ff), F32),
            pltpu.VMEM((streams * c, d_ff), BF16),
        ],
        compiler_params=pltpu.CompilerParams(
            dimension_semantics=("arbitrary",), vmem_limit_bytes=VMEM_LIMIT),
        name="layer",
    )(x, x, *consts, *state_ops)
    y, s_t, c_new, n_new, m_e, conv_new = outs
    s_new = jnp.transpose(s_t.reshape(batch, GLA_DV, GLA_HEADS, GLA_DK), (0, 2, 3, 1))
    return (y, s_new, c_new, n_new.reshape(batch, M_HEADS, M_DH), m_e[:, :, 0, 0], conv_new)


def _regroup_kernel(wg_ref, wm_ref, glr_ref, mif_ref, o_ref, osm_ref, *, n_gla):
    j = pl.program_id(0)

    @pl.when(j < n_gla)
    def _():
        o_ref[...] = wg_ref[...].T.astype(BF16)

    @pl.when(j >= n_gla)
    def _():
        o_ref[...] = wm_ref[...].T.astype(BF16)

    @pl.when(j == 0)
    def _():
        d = wg_ref.shape[1]
        small = jnp.concatenate(
            [glr_ref[...], mif_ref[...], jnp.zeros((LANES - GLA_RANK - 2 * M_HEADS, d), F32)], axis=0)
        osm_ref[...] = small.T.astype(BF16)


def _regroup_w_in(w):
    d = w.shape[0]
    wt = w.T
    o_glr = OFF_MQ
    o_m = o_glr + GLA_RANK
    o_if = o_m + 4 * M_W
    nslab = OFF_SMALL // INPROJ_SLAB
    n_gla = OFF_MQ // INPROJ_SLAB
    rows = lambda n: (pl.Element(n), pl.Element(d))
    return pl.pallas_call(
        functools.partial(_regroup_kernel, n_gla=n_gla),
        grid=(nslab,),
        in_specs=[
            pl.BlockSpec(rows(INPROJ_SLAB), lambda j: (jnp.minimum(j, n_gla - 1) * INPROJ_SLAB, 0)),
            pl.BlockSpec(rows(INPROJ_SLAB), lambda j: (
                (o_m // GLA_RANK + jnp.maximum(j - n_gla, 0) * (INPROJ_SLAB // GLA_RANK)) * GLA_RANK, 0)),
            pl.BlockSpec(rows(GLA_RANK), lambda j: (o_glr, 0)),
            pl.BlockSpec(rows(2 * M_HEADS), lambda j: (o_if, 0)),
        ],
        out_specs=[
            pl.BlockSpec((d, INPROJ_SLAB), lambda j: (0, j)),
            pl.BlockSpec((d, LANES), lambda j: (0, 0)),
        ],
        out_shape=[
            jax.ShapeDtypeStruct((d, OFF_SMALL), BF16),
            jax.ShapeDtypeStruct((d, LANES), BF16),
        ],
        compiler_params=pltpu.CompilerParams(dimension_semantics=("arbitrary",)),
        name="regroup_w_in",
    )(wt, wt, wt, wt)


def kernel(x_prompt, x_sample, state_gla, state_mlstm_C, state_mlstm_n, state_mlstm_m, cache_ffn_conv,
           norm1, w_in, w_gla_a, b_gla_a, gla_norm, b_m_i, b_m_f, m_norm, w_out, norm2, w_g, w_u,
           conv_w, conv_b, w_d, norm_f):
    assert w_in.shape[0] == 1, "single-layer stack"
    w_in_r, w_in_sm = _regroup_w_in(w_in[0])
    wa_pad = jnp.concatenate(
        [w_gla_a[0], jnp.zeros((LANES - GLA_RANK, GLA_QK), w_gla_a.dtype)], axis=0).astype(BF16)
    b_small = jnp.concatenate(
        [jnp.zeros((GLA_RANK,), F32), b_m_i[0], b_m_f[0],
         jnp.zeros((LANES - GLA_RANK - 2 * M_HEADS,), F32)])[None]
    wts = dict(
        norm1=norm1, w_in_r=w_in_r, w_in_sm=w_in_sm, wa_pad=wa_pad, b_gla_a=b_gla_a, gla_norm=gla_norm,
        b_small=b_small, m_norm=m_norm, w_out=w_out[0].astype(BF16), norm2=norm2,
        w_g=w_g[0].astype(BF16), w_u=w_u[0].astype(BF16), conv_w=conv_w[0], conv_b=conv_b,
        w_d=w_d[0].astype(BF16), norm_f=norm_f[None])

    outs_p = _layer(x_prompt, None, wts, streams=4)
    outs_s = _layer(x_sample, (state_gla[0], state_mlstm_C[0], state_mlstm_n[0], state_mlstm_m[0],
                               cache_ffn_conv[0]), wts, streams=x_sample.shape[0])
    y_p, *st_p = outs_p
    y_s, *st_s = outs_s
    return (y_p, y_s) + tuple(s[None] for s in st_p) + tuple(s[None] for s in st_s)
```

```python
import functools

import numpy as np
import jax
import jax.numpy as jnp
from jax import lax
from jax.experimental import pallas as pl
from jax.experimental.pallas import tpu as pltpu

F32 = jnp.float32
BF16 = jnp.bfloat16

EPS = 1e-6
CHUNK = 64
GLA_HEADS = 4
GLA_DK = 64
GLA_DV = 128
GLA_RANK = 16
GLA_TAU = 16.0
M_HEADS = 4
M_DH = 128
FFN_CONV = 3
LANES = 128
SUBLANES = 8

GLA_QK = GLA_HEADS * GLA_DK
GLA_V = GLA_HEADS * GLA_DV
M_W = M_HEADS * M_DH
D_MIX = GLA_V + M_W
OFF_GQ = 0
OFF_GK = OFF_GQ + GLA_QK
OFF_GV = OFF_GK + GLA_QK
OFF_GR = OFF_GV + GLA_V
OFF_MQ = OFF_GR + GLA_V
OFF_MK = OFF_MQ + M_W
OFF_MV = OFF_MK + M_W
OFF_MO = OFF_MV + M_W
OFF_SMALL = OFF_MO + M_W
P_COLS = OFF_SMALL + LANES
SM_I = GLA_RANK
SM_F = GLA_RANK + M_HEADS

INPROJ_SLAB = 512
FFN_SLAB = 256
DOWN_SLAB = 256
VMEM_LIMIT = 60 * 1024 * 1024
N_WEIGHTS = 20


def _dot(a, b):
    return jnp.dot(a, b, preferred_element_type=F32)


def _dot_nt(a, b):
    return lax.dot_general(a, b, (((1,), (1,)), ((), ())), preferred_element_type=F32)


def _dot_tn(a, b):
    return lax.dot_general(a, b, (((0,), (0,)), ((), ())), preferred_element_type=F32)


def _split_hilo(x):
    hi = x.astype(BF16)
    lo = (x - hi.astype(F32)).astype(BF16)
    return hi, lo


def _log_sigmoid(x):
    return jnp.minimum(x, 0.0) - jnp.log1p(jnp.exp(-jnp.abs(x)))


def _rms(x, g):
    return x * lax.rsqrt(jnp.mean(x * x, axis=-1, keepdims=True) + EPS) * g


def _interleave(tasks, strides):
    alive = list(zip(tasks, strides))
    while alive:
        for item in list(alive):
            task, stride = item
            for _ in range(stride):
                try:
                    next(task)
                except StopIteration:
                    alive.remove(item)
                    break


def _mixer_consts(c):
    nlev = int(np.log2(c))
    assert 2 ** nlev == c
    t = np.arange(c)[:, None]
    u = np.arange(c)[None, :]
    blocks = [(u <= t), (u > t)]
    for l in range(1, nlev + 1):
        bsz, half = 2 ** l, 2 ** (l - 1)
        mid = (t // bsz) * bsz + half
        right = (t % bsz) >= half
        blocks.append(np.where(right, (u >= mid) & (u <= t), (u > t) & (u < mid)))
    a = np.concatenate(blocks, axis=0).astype(np.float32)
    a2 = np.concatenate([a, a], axis=1)
    x = np.bitwise_xor(t, u)
    lv = np.where(u > t, -1, np.where(x == 0, 0, np.floor(np.log2(np.maximum(x, 1))) + 1))
    e = np.zeros((LANES, 2 * M_HEADS * LANES), np.float32)
    for h in range(M_HEADS):
        e[SM_I + h, h * LANES:(h + 1) * LANES] = 1.0
        e[SM_F + h, (M_HEADS + h) * LANES:(M_HEADS + h + 1) * LANES] = 1.0
    e2 = np.concatenate([e, e], axis=0)
    lv4 = np.tile(lv, (1, GLA_HEADS))
    bd = (np.arange(GLA_HEADS * c)[:, None] // c == np.arange(GLA_QK)[None, :] // GLA_DK)
    return (jnp.asarray(a2, BF16), jnp.asarray(lv4, jnp.int32), jnp.asarray(e2, BF16),
            jnp.asarray(bd, BF16), nlev)


def _inproj(xf_ref, n1_ref, win_ref, wsm_ref, p_sc, ready, *, c, streams):
    rows = streams * c
    d = xf_ref.shape[-1]
    h_in = _rms(xf_ref[...].reshape(rows, d), n1_ref[...]).astype(BF16)
    for lo in range(0, OFF_SMALL, INPROJ_SLAB):
        hi = lo + INPROJ_SLAB
        p_sc[:, lo:hi] = _dot(h_in, win_ref[:, lo:hi])
        ready.append((lo, hi))
        yield
    p_sc[:, OFF_SMALL:] = _dot(h_in, wsm_ref[...])
    ready.append((OFF_SMALL, P_COLS))
    yield


def _p_reader(p_sc, ready, rs):
    def pcol(g, off, width):
        assert any(lo <= off and off + width <= hi for lo, hi in ready), (off, width)
        return p_sc[rs[g], off:off + width]
    return pcol


def _gla(wa_ref, ba_ref, gn_ref, a2_ref, lv_ref, bd_ref, p_sc, ready, mix_sc, slot, s_sc, *, c, nlev,
         streams):
    gs = range(streams)
    rs = [slice(g * c, (g + 1) * c) for g in gs]
    pcol = _p_reader(p_sc, ready, rs)
    assert (OFF_SMALL, P_COLS) in ready
    small = p_sc[:, OFF_SMALL:OFF_SMALL + LANES]
    xa = _dot(small.astype(BF16), wa_ref[...]) + ba_ref[...]
    la_hi, la_lo = _split_hilo(_log_sigmoid(xa) * (1.0 / GLA_TAU))
    yield
    z = [_dot(a2_ref[...], jnp.concatenate([la_hi[rs[g]], la_lo[rs[g]]], axis=0))
         for g in gs]
    yield
    ez = [jnp.exp(z[g]) for g in gs]
    yield
    q = [pcol(g, OFF_GQ, GLA_QK) * (GLA_DK ** -0.5) for g in gs]
    k = [pcol(g, OFF_GK, GLA_QK) for g in gs]
    q_in = [(q[g] * ez[g][0:c]).astype(BF16) for g in gs]
    k_dec = [(k[g] * ez[g][c:2 * c]).astype(BF16) for g in gs]
    dec = [ez[g][c - 1:c, :] for g in gs]
    q_lv = [[q[g].astype(BF16)] + [(q[g] * ez[g][(2 + l) * c:(3 + l) * c]).astype(BF16)
                                   for l in range(nlev)] for g in gs]
    k_lv = [[k[g].astype(BF16)] + [(k[g] * ez[g][(2 + l) * c:(3 + l) * c]).astype(BF16)
                                   for l in range(nlev)] for g in gs]
    yield
    lv = lv_ref[...]
    bd = bd_ref[...]
    sc = [jnp.zeros((c, GLA_HEADS * c), F32) for g in gs]
    for l in range(nlev + 1):
        for g in gs:
            k_bd = jnp.concatenate([k_lv[g][l]] * GLA_HEADS, axis=0) * bd
            sc[g] = jnp.where(lv == l, _dot_nt(q_lv[g][l], k_bd), sc[g])
        yield
    sc = [sc[g].astype(BF16) for g in gs]
    for h in range(GLA_HEADS):
        sl = slice(h * GLA_DK, (h + 1) * GLA_DK)
        for g in gs:
            vb = pcol(g, OFF_GV + h * GLA_DV, GLA_DV).astype(BF16)
            st = s_sc[g, :, sl]
            o = _dot_nt(q_in[g][:, sl], st.astype(BF16)) + _dot(sc[g][:, h * c:(h + 1) * c], vb)
            s_sc[g, :, sl] = dec[g][:, sl] * st + _dot_tn(vb, k_dec[g][:, sl])
            r = pcol(g, OFF_GR + h * GLA_DV, GLA_DV)
            mix_sc[slot, rs[g], h * GLA_DV:(h + 1) * GLA_DV] = (
                _rms(o, gn_ref[...]) * (r * jax.nn.sigmoid(r)))
        yield


def _mlstm(bsm_ref, mn_ref, a2_ref, e2_ref, p_sc, ready, mix_sc, slot, c_sc, m_sc, *, c, streams):
    rows = streams * c
    gs = range(streams)
    rs = [slice(g * c, (g + 1) * c) for g in gs]
    pcol = _p_reader(p_sc, ready, rs)
    assert (OFF_SMALL, P_COLS) in ready
    lane = lax.broadcasted_iota(jnp.int32, (c, LANES), 1)
    row = lax.broadcasted_iota(jnp.int32, (c, LANES), 0)
    on_diag = row == lane
    causal = lane <= row
    ones = jnp.ones((c, LANES), F32)
    lane_all = lax.broadcasted_iota(jnp.int32, (rows, LANES), 1)
    is_f = (lane_all >= SM_F) & (lane_all < SM_F + M_HEADS)

    x = p_sc[:, OFF_SMALL:OFF_SMALL + LANES] + bsm_ref[...]
    x = jnp.where(is_f, _log_sigmoid(x), x)
    x_hi, x_lo = _split_hilo(x)
    yield
    xcum = jnp.concatenate(
        [_dot(a2_ref[0:c, :], jnp.concatenate([x_hi[rs[g]], x_lo[rs[g]]], axis=0)) for g in gs],
        axis=0)
    xc = jnp.where(is_f, xcum, x)
    yield
    xe_all = _dot(jnp.concatenate(_split_hilo(xc), axis=1), e2_ref[...])
    xe = [xe_all[rs[g]] for g in gs]
    yield
    for h in range(M_HEADS):
        ie = [xe[g][:, h * LANES:(h + 1) * LANES] for g in gs]
        fe = [xe[g][:, (M_HEADS + h) * LANES:(M_HEADS + h + 1) * LANES] for g in gs]
        a = [ie[g] - fe[g] for g in gs]
        a_row = [jnp.sum(jnp.where(on_diag, a[g], 0.0), axis=0, keepdims=True) for g in gs]
        dm = [jnp.where(causal, fe[g] + a_row[g], -jnp.inf)[:, :c] for g in gs]
        gi = [fe[g] + m_sc[g, h] for g in gs]
        m_t = [jnp.maximum(gi[g], jnp.max(dm[g], axis=-1, keepdims=True)) for g in gs]
        w_int = [jnp.exp(gi[g] - m_t[g]) for g in gs]
        w = [jnp.exp(dm[g] - m_t[g][:, :c]) for g in gs]
        yield
        qh = [pcol(g, OFF_MQ + h * M_DH, M_DH).astype(BF16) for g in gs]
        kf = [pcol(g, OFF_MK + h * M_DH, M_DH) * (M_DH ** -0.5) for g in gs]
        vaug = [jnp.concatenate([pcol(g, OFF_MV + h * M_DH, M_DH), ones], axis=1).astype(BF16)
                for g in gs]
        qk = [(_dot_nt(qh[g], kf[g].astype(BF16)) * w[g]).astype(BF16) for g in gs]
        yield
        caug = [c_sc[g, h] for g in gs]
        nd = [_dot(qk[g], vaug[g])
              + jnp.concatenate([w_int[g], w_int[g]], axis=1) * _dot(qh[g], caug[g].astype(BF16))
              for g in gs]
        for g in gs:
            den = jnp.maximum(jnp.abs(nd[g][:, M_DH:]), jnp.exp(-m_t[g]))
            og = jax.nn.sigmoid(pcol(g, OFF_MO + h * M_DH, M_DH)) * (nd[g][:, :M_DH] / den)
            mix_sc[slot, rs[g], GLA_V + h * M_DH:GLA_V + (h + 1) * M_DH] = _rms(og, mn_ref[...])
        yield
        for g in gs:
            m_c = m_t[g][c - 1:c, :]
            decay = jnp.exp(gi[g][c - 1:c, :] - m_c)
            ws = jnp.exp(a[g] + (fe[g][c - 1:c, :] - m_c))
            kw = (kf[g] * ws).astype(BF16)
            c_sc[g, h] = jnp.concatenate([decay, decay], axis=1) * caug[g] + _dot_tn(kw, vaug[g])
            m_sc[g, h] = m_c
        yield


def _back(xb_ref, mix_sc, slot_prev, wo_ref, n2_ref, wg_ref, wu_ref, cw_ref, cb_ref, wd_ref, nf_ref,
          y_ref, carry_sc, act_sc, *, c, streams):
    rows = streams * c
    d = xb_ref.shape[-1]
    d_ff = wg_ref.shape[1]
    nslab = d_ff // FFN_SLAB
    x1 = xb_ref[...].reshape(rows, d) + _dot(mix_sc[slot_prev].astype(BF16), wo_ref[...])
    h2 = _rms(x1, n2_ref[...]).astype(BF16)
    yield

    def gate_up(s):
        cs = slice(s * FFN_SLAB, (s + 1) * FFN_SLAB)
        return _dot(h2, wg_ref[:, cs]), _dot(h2, wu_ref[:, cs])

    row8 = lax.broadcasted_iota(jnp.int32, (SUBLANES, FFN_SLAB), 0)
    nxt = gate_up(0)
    yield
    for s in range(nslab):
        cs = slice(s * FFN_SLAB, (s + 1) * FFN_SLAB)
        g, u = nxt
        if s + 1 < nslab:
            nxt = gate_up(s + 1)
        yield
        r1 = pltpu.roll(g, 1, axis=0)
        r2 = pltpu.roll(g, 2, axis=0)
        g1, g2 = [], []
        for st in range(streams):
            b0 = st * c
            p0 = carry_sc[st, 0:1, cs]
            p1 = carry_sc[st, 1:2, cs]
            g1 += [jnp.where(row8 == 0, p1, r1[b0:b0 + SUBLANES]), r1[b0 + SUBLANES:b0 + c]]
            g2 += [jnp.where(row8 == 0, p0, jnp.where(row8 == 1, p1, r2[b0:b0 + SUBLANES])),
                   r2[b0 + SUBLANES:b0 + c]]
            carry_sc[st, 0:1, cs] = g[b0 + c - 2:b0 + c - 1, :]
            carry_sc[st, 1:2, cs] = g[b0 + c - 1:b0 + c, :]
        g1 = jnp.concatenate(g1, axis=0)
        g2 = jnp.concatenate(g2, axis=0)
        gc = cb_ref[:, cs] + (cw_ref[0:1, cs] * g2 + cw_ref[1:2, cs] * g1 + cw_ref[2:3, cs] * g)
        yield
        act = (jax.nn.gelu(gc) * u).astype(BF16)
        act_sc[:, cs] = act
        yield
    down = []
    for lo in range(0, d, DOWN_SLAB):
        down.append(_dot(act_sc[...], wd_ref[:, lo:lo + DOWN_SLAB]))
        yield
    y_ref[...] = _rms(x1 + jnp.concatenate(down, axis=1), nf_ref[...]).reshape(streams, c, d)


def _layer_kernel(*refs, c, nlev, streams, nj, nsteps, skew, zero_state):
    n_in = 2 + N_WEIGHTS + (0 if zero_state else 5)
    (xf_ref, xb_ref, n1_ref, win_ref, wsm_ref, wa_ref, ba_ref, gn_ref, bsm_ref, mn_ref, a2_ref, lv_ref,
     e2_ref, bd_ref, wo_ref, n2_ref, wg_ref, wu_ref, cw_ref, cb_ref, wd_ref, nf_ref) = refs[:2 + N_WEIGHTS]
    s0_ref, c0_ref, n0_ref, m0_ref, cst_ref = (None,) * 5 if zero_state else refs[2 + N_WEIGHTS:n_in]
    y_ref, s_out, c_out, n_out, m_out, cv_out = refs[n_in:n_in + 6]
    p_sc, mix_sc, s_sc, c_sc, m_sc, carry_sc, act_sc = refs[n_in + 6:]

    t = pl.program_id(0)
    tf = jnp.minimum(t, nsteps - 1)
    jf = lax.rem(tf, nj)
    tb = jnp.maximum(t - skew, 0)
    jb = lax.rem(tb, nj)
    slot = lax.rem(t, 2)
    slot_back = 1 - slot if skew else slot

    if skew:
        @pl.when(t == 0)
        def _():
            mix_sc[1] = jnp.zeros(mix_sc.shape[1:], F32)

    @pl.when(jf == 0)
    def _():
        if zero_state:
            s_sc[...] = jnp.zeros(s_sc.shape, F32)
            c_sc[...] = jnp.zeros(c_sc.shape, F32)
            m_sc[...] = jnp.zeros(m_sc.shape, F32)
        else:
            for g in range(streams):
                for lo in range(0, GLA_QK, LANES):
                    s_sc[g, :, lo:lo + LANES] = s0_ref[g, lo:lo + LANES, :].T
            c_sc[:, :, :, :M_DH] = c0_ref[...]
            c_sc[:, :, :, M_DH:] = jnp.broadcast_to(n0_ref[...], c0_ref.shape)
            m_sc[...] = m0_ref[...]

    @pl.when(jb == 0)
    def _():
        carry_sc[...] = jnp.zeros(carry_sc.shape, F32) if zero_state else cst_ref[...]

    back = _back(xb_ref, mix_sc, slot_back, wo_ref, n2_ref, wg_ref, wu_ref, cw_ref, cb_ref, wd_ref,
                 nf_ref, y_ref, carry_sc, act_sc, c=c, streams=streams)
    if skew:
        next(back)
    ready = []
    for _ in _inproj(xf_ref, n1_ref, win_ref, wsm_ref, p_sc, ready, c=c, streams=streams):
        pass
    mixers = [_gla(wa_ref, ba_ref, gn_ref, a2_ref, lv_ref, bd_ref, p_sc, ready, mix_sc, slot, s_sc,
                   c=c, nlev=nlev, streams=streams),
              _mlstm(bsm_ref, mn_ref, a2_ref, e2_ref, p_sc, ready, mix_sc, slot, c_sc, m_sc,
                     c=c, streams=streams)]
    if skew:
        _interleave(mixers + [back], strides=(1, 1, 3))
    else:
        _interleave(mixers, strides=(1, 1))
        _interleave([back], strides=(1,))

    @pl.when((jf == nj - 1) & (t < nsteps))
    def _():
        for g in range(streams):
            for lo in range(0, GLA_QK, LANES):
                s_out[g, lo:lo + LANES, :] = s_sc[g, :, lo:lo + LANES].T
        c_out[...] = c_sc[:, :, :, :M_DH]
        on_diag = (lax.broadcasted_iota(jnp.int32, (M_DH, M_DH), 0)
                   == lax.broadcasted_iota(jnp.int32, (M_DH, M_DH), 1))
        for g in range(streams):
            for h in range(M_HEADS):
                n_out[g, h] = jnp.sum(jnp.where(on_diag, c_sc[g, h, :, M_DH:], 0.0), axis=0, keepdims=True)
        m_out[...] = m_sc[...]

    @pl.when((jb == nj - 1) & (t >= skew))
    def _():
        cv_out[...] = carry_sc[...]


def _layer(x, states, wts, streams):
    batch, length, d = x.shape
    c = min(CHUNK, length)
    nj = length // c
    nsteps = (batch // streams) * nj
    d_ff = wts["w_g"].shape[1]
    a2, lv, e2, bd, nlev = _mixer_consts(c)
    skew = 1 if nsteps > 1 else 0

    def front_idx(t):
        tf = jnp.minimum(t, nsteps - 1)
        return tf // nj, tf % nj

    def back_idx(t):
        tb = jnp.maximum(t - skew, 0)
        return tb // nj, tb % nj

    once = pl.Buffered(1)
    const = lambda a: pl.BlockSpec(a.shape, lambda t: (0,) * a.ndim, pipeline_mode=once)
    fstate = lambda *blk, **kw: pl.BlockSpec(
        (streams,) + blk, lambda t: (front_idx(t)[0],) + (0,) * len(blk), **kw)
    bstate = lambda *blk, **kw: pl.BlockSpec(
        (streams,) + blk, lambda t: (back_idx(t)[0],) + (0,) * len(blk), **kw)
    consts = [wts["norm1"], wts["w_in_r"], wts["w_in_sm"], wts["wa_pad"], wts["b_gla_a"], wts["gla_norm"],
              wts["b_small"], wts["m_norm"], a2, lv, e2, bd, wts["w_out"], wts["norm2"], wts["w_g"],
              wts["w_u"], wts["conv_w"], wts["conv_b"], wts["w_d"], wts["norm_f"]]
    assert len(consts) == N_WEIGHTS
    state_ops, state_specs = [], []
    if states is not None:
        s_gla, c_m, n_m, m_m, conv_st = states
        state_ops = [
            s_gla.reshape(batch, GLA_QK, GLA_DV),
            c_m, n_m[..., None],
            jnp.broadcast_to(m_m[..., None, None], (batch, M_HEADS, 1, LANES)), conv_st]
        state_specs = [
            fstate(GLA_QK, GLA_DV, pipeline_mode=once),
            fstate(M_HEADS, M_DH, M_DH, pipeline_mode=once),
            fstate(M_HEADS, M_DH, 1, pipeline_mode=once),
            fstate(M_HEADS, 1, LANES, pipeline_mode=once),
            bstate(FFN_CONV - 1, d_ff, pipeline_mode=once)]
    outs = pl.pallas_call(
        functools.partial(_layer_kernel, c=c, nlev=nlev, streams=streams, nj=nj, nsteps=nsteps,
                          skew=skew, zero_state=states is None),
        grid=(nsteps + skew,),
        in_specs=[
            pl.BlockSpec((streams, c, d), lambda t: front_idx(t) + (0,)),
            pl.BlockSpec((streams, c, d), lambda t: back_idx(t) + (0,)),
        ] + [const(a) for a in consts] + state_specs,
        out_specs=[
            pl.BlockSpec((streams, c, d), lambda t: back_idx(t) + (0,)),
            fstate(GLA_QK, GLA_DV), fstate(M_HEADS, M_DH, M_DH), fstate(M_HEADS, 1, M_DH),
            fstate(M_HEADS, 1, LANES), bstate(FFN_CONV - 1, d_ff),
        ],
        out_shape=[
            jax.ShapeDtypeStruct((batch, length, d), F32),
            jax.ShapeDtypeStruct((batch, GLA_QK, GLA_DV), F32),
            jax.ShapeDtypeStruct((batch, M_HEADS, M_DH, M_DH), F32),
            jax.ShapeDtypeStruct((batch, M_HEADS, 1, M_DH), F32),
            jax.ShapeDtypeStruct((batch, M_HEADS, 1, LANES), F32),
            jax.ShapeDtypeStruct((batch, FFN_CONV - 1, d_ff), F32),
        ],
        scratch_shapes=[
            pltpu.VMEM((streams * c, P_COLS), F32),
            pltpu.VMEM((2, streams * c, D_MIX), F32),
            pltpu.VMEM((streams, GLA_DV, GLA_QK), F32),
            pltpu.VMEM((streams, M_HEADS, M_DH, 2 * M_DH), F32),
            pltpu.VMEM((streams, M_HEADS, 1, LANES), F32),
            pltpu.VMEM((streams, FFN_CONV - 1, d_ff), F32),
            pltpu.VMEM((streams * c, d_ff), BF16),
        ],
        compiler_params=pltpu.CompilerParams(
            dimension_semantics=("arbitrary",), vmem_limit_bytes=VMEM_LIMIT),
        name="layer",
    )(x, x, *consts, *state_ops)
    y, s_new, c_new, n_new, m_e, conv_new = outs
    return (y, s_new.reshape(batch, GLA_HEADS, GLA_DK, GLA_DV), c_new,
            n_new.reshape(batch, M_HEADS, M_DH), m_e[:, :, 0, 0], conv_new)


def _regroup_kernel(wg_ref, wm_ref, glr_ref, mif_ref, o_ref, osm_ref, *, n_gla):
    j = pl.program_id(0)

    @pl.when(j < n_gla)
    def _():
        o_ref[...] = wg_ref[...].T.astype(BF16)

    @pl.when(j >= n_gla)
    def _():
        o_ref[...] = wm_ref[...].T.astype(BF16)

    @pl.when(j == 0)
    def _():
        d = wg_ref.shape[1]
        small = jnp.concatenate(
            [glr_ref[...], mif_ref[...], jnp.zeros((LANES - GLA_RANK - 2 * M_HEADS, d), F32)], axis=0)
        osm_ref[...] = small.T.astype(BF16)


def _regroup_w_in(w):
    d = w.shape[0]
    wt = w.T
    o_glr = OFF_MQ
    o_m = o_glr + GLA_RANK
    o_if = o_m + 4 * M_W
    nslab = OFF_SMALL // INPROJ_SLAB
    n_gla = OFF_MQ // INPROJ_SLAB
    rows = lambda n: (pl.Element(n), pl.Element(d))
    return pl.pallas_call(
        functools.partial(_regroup_kernel, n_gla=n_gla),
        grid=(nslab,),
        in_specs=[
            pl.BlockSpec(rows(INPROJ_SLAB), lambda j: (jnp.minimum(j, n_gla - 1) * INPROJ_SLAB, 0)),
            pl.BlockSpec(rows(INPROJ_SLAB), lambda j: (
                (o_m // GLA_RANK + jnp.maximum(j - n_gla, 0) * (INPROJ_SLAB // GLA_RANK)) * GLA_RANK, 0)),
            pl.BlockSpec(rows(GLA_RANK), lambda j: (o_glr, 0)),
            pl.BlockSpec(rows(2 * M_HEADS), lambda j: (o_if, 0)),
        ],
        out_specs=[
            pl.BlockSpec((d, INPROJ_SLAB), lambda j: (0, j)),
            pl.BlockSpec((d, LANES), lambda j: (0, 0)),
        ],
        out_shape=[
            jax.ShapeDtypeStruct((d, OFF_SMALL), BF16),
            jax.ShapeDtypeStruct((d, LANES), BF16),
        ],
        compiler_params=pltpu.CompilerParams(dimension_semantics=("arbitrary",)),
        name="regroup_w_in",
    )(wt, wt, wt, wt)


def kernel(x_prompt, x_sample, state_gla, state_mlstm_C, state_mlstm_n, state_mlstm_m, cache_ffn_conv,
           norm1, w_in, w_gla_a, b_gla_a, gla_norm, b_m_i, b_m_f, m_norm, w_out, norm2, w_g, w_u,
           conv_w, conv_b, w_d, norm_f):
    assert w_in.shape[0] == 1, "single-layer stack"
    w_in_r, w_in_sm = _regroup_w_in(w_in[0])
    wa_pad = jnp.concatenate(
        [w_gla_a[0], jnp.zeros((LANES - GLA_RANK, GLA_QK), w_gla_a.dtype)], axis=0).astype(BF16)
    b_small = jnp.concatenate(
        [jnp.zeros((GLA_RANK,), F32), b_m_i[0], b_m_f[0],
         jnp.zeros((LANES - GLA_RANK - 2 * M_HEADS,), F32)])[None]
    wts = dict(
        norm1=norm1, w_in_r=w_in_r, w_in_sm=w_in_sm, wa_pad=wa_pad, b_gla_a=b_gla_a, gla_norm=gla_norm,
        b_small=b_small, m_norm=m_norm, w_out=w_out[0].astype(BF16), norm2=norm2,
        w_g=w_g[0].astype(BF16), w_u=w_u[0].astype(BF16), conv_w=conv_w[0], conv_b=conv_b,
        w_d=w_d[0].astype(BF16), norm_f=norm_f[None])

    outs_p = _layer(x_prompt, None, wts, streams=4)
    outs_s = _layer(x_sample, (state_gla[0], state_mlstm_C[0], state_mlstm_n[0], state_mlstm_m[0],
                               cache_ffn_conv[0]), wts, streams=x_sample.shape[0])
    y_p, *st_p = outs_p
    y_s, *st_s = outs_s
    return (y_p, y_s) + tuple(s[None] for s in st_p) + tuple(s[None] for s in st_s)
```

```python
import functools

import numpy as np
import jax
import jax.numpy as jnp
from jax import lax
from jax.experimental import pallas as pl
from jax.experimental.pallas import tpu as pltpu

F32 = jnp.float32
BF16 = jnp.bfloat16

EPS = 1e-6
CHUNK = 64
GLA_HEADS = 4
GLA_DK = 64
GLA_DV = 128
GLA_RANK = 16
GLA_TAU = 16.0
M_HEADS = 4
M_DH = 128
FFN_CONV = 3
LANES = 128
SUBLANES = 8

GLA_QK = GLA_HEADS * GLA_DK
GLA_V = GLA_HEADS * GLA_DV
M_W = M_HEADS * M_DH
D_MIX = GLA_V + M_W
OFF_GQ = 0
OFF_GK = OFF_GQ + GLA_QK
OFF_GV = OFF_GK + GLA_QK
OFF_GR = OFF_GV + GLA_V
OFF_MQ = OFF_GR + GLA_V
OFF_MK = OFF_MQ + M_W
OFF_MV = OFF_MK + M_W
OFF_MO = OFF_MV + M_W
OFF_SMALL = OFF_MO + M_W
P_COLS = OFF_SMALL + LANES
SM_I = GLA_RANK
SM_F = GLA_RANK + M_HEADS

INPROJ_SLAB = 512
FFN_SLAB = 256
DOWN_SLAB = 256
VMEM_LIMIT = 60 * 1024 * 1024
N_WEIGHTS = 20


def _dot(a, b):
    return jnp.dot(a, b, preferred_element_type=F32)


def _dot_nt(a, b):
    return lax.dot_general(a, b, (((1,), (1,)), ((), ())), preferred_element_type=F32)


def _dot_tn(a, b):
    return lax.dot_general(a, b, (((0,), (0,)), ((), ())), preferred_element_type=F32)


def _split_hilo(x):
    hi = x.astype(BF16)
    lo = (x - hi.astype(F32)).astype(BF16)
    return hi, lo


def _log_sigmoid(x):
    return jnp.minimum(x, 0.0) - jnp.log1p(jnp.exp(-jnp.abs(x)))


def _rms(x, g):
    return x * lax.rsqrt(jnp.mean(x * x, axis=-1, keepdims=True) + EPS) * g


def _interleave(tasks, strides):
    alive = list(zip(tasks, strides))
    while alive:
        for item in list(alive):
            task, stride = item
            for _ in range(stride):
                try:
                    next(task)
                except StopIteration:
                    alive.remove(item)
                    break


def _mixer_consts(c):
    nlev = int(np.log2(c))
    assert 2 ** nlev == c
    t = np.arange(c)[:, None]
    u = np.arange(c)[None, :]
    blocks = [(u <= t), (u > t)]
    for l in range(1, nlev + 1):
        bsz, half = 2 ** l, 2 ** (l - 1)
        mid = (t // bsz) * bsz + half
        right = (t % bsz) >= half
        blocks.append(np.where(right, (u >= mid) & (u <= t), (u > t) & (u < mid)))
    a = np.concatenate(blocks, axis=0).astype(np.float32)
    a2 = np.concatenate([a, a], axis=1)
    x = np.bitwise_xor(t, u)
    lv = np.where(u > t, -1, np.where(x == 0, 0, np.floor(np.log2(np.maximum(x, 1))) + 1))
    e = np.zeros((LANES, 2 * M_HEADS * LANES), np.float32)
    for h in range(M_HEADS):
        e[SM_I + h, h * LANES:(h + 1) * LANES] = 1.0
        e[SM_F + h, (M_HEADS + h) * LANES:(M_HEADS + h + 1) * LANES] = 1.0
    e2 = np.concatenate([e, e], axis=0)
    lv4 = np.tile(lv, (1, GLA_HEADS))
    bd = (np.arange(GLA_HEADS * c)[:, None] // c == np.arange(GLA_QK)[None, :] // GLA_DK)
    return (jnp.asarray(a2, BF16), jnp.asarray(lv4, jnp.int32), jnp.asarray(e2, BF16),
            jnp.asarray(bd, BF16), nlev)


def _inproj(xf_ref, n1_ref, win_ref, wsm_ref, p_sc, ready, *, c, streams):
    rows = streams * c
    d = xf_ref.shape[-1]
    h_in = _rms(xf_ref[...].reshape(rows, d), n1_ref[...]).astype(BF16)
    for lo in range(0, OFF_SMALL, INPROJ_SLAB):
        hi = lo + INPROJ_SLAB
        p_sc[:, lo:hi] = _dot(h_in, win_ref[:, lo:hi])
        ready.append((lo, hi))
        yield
    p_sc[:, OFF_SMALL:] = _dot(h_in, wsm_ref[...])
    ready.append((OFF_SMALL, P_COLS))
    yield


def _p_reader(p_sc, ready, rs):
    def pcol(g, off, width):
        assert any(lo <= off and off + width <= hi for lo, hi in ready), (off, width)
        return p_sc[rs[g], off:off + width]
    return pcol


def _gla(wa_ref, ba_ref, gn_ref, a2_ref, lv_ref, bd_ref, p_sc, ready, mix_sc, slot, s_sc, *, c, nlev,
         streams):
    gs = range(streams)
    rs = [slice(g * c, (g + 1) * c) for g in gs]
    pcol = _p_reader(p_sc, ready, rs)
    assert (OFF_SMALL, P_COLS) in ready
    small = p_sc[:, OFF_SMALL:OFF_SMALL + LANES]
    xa = _dot(small.astype(BF16), wa_ref[...]) + ba_ref[...]
    la_hi, la_lo = _split_hilo(_log_sigmoid(xa) * (1.0 / GLA_TAU))
    yield
    z = [_dot(a2_ref[...], jnp.concatenate([la_hi[rs[g]], la_lo[rs[g]]], axis=0))
         for g in gs]
    yield
    ez = [jnp.exp(z[g]) for g in gs]
    yield
    q = [pcol(g, OFF_GQ, GLA_QK) * (GLA_DK ** -0.5) for g in gs]
    k = [pcol(g, OFF_GK, GLA_QK) for g in gs]
    q_in = [(q[g] * ez[g][0:c]).astype(BF16) for g in gs]
    k_dec = [(k[g] * ez[g][c:2 * c]).astype(BF16) for g in gs]
    dec = [ez[g][c - 1:c, :] for g in gs]
    q_lv = [[q[g].astype(BF16)] + [(q[g] * ez[g][(2 + l) * c:(3 + l) * c]).astype(BF16)
                                   for l in range(nlev)] for g in gs]
    k_lv = [[k[g].astype(BF16)] + [(k[g] * ez[g][(2 + l) * c:(3 + l) * c]).astype(BF16)
                                   for l in range(nlev)] for g in gs]
    yield
    lv = lv_ref[...]
    bd = bd_ref[...]
    sc = [jnp.zeros((c, GLA_HEADS * c), F32) for g in gs]
    for l in range(nlev + 1):
        for g in gs:
            k_bd = jnp.concatenate([k_lv[g][l]] * GLA_HEADS, axis=0) * bd
            sc[g] = jnp.where(lv == l, _dot_nt(q_lv[g][l], k_bd), sc[g])
        yield
    sc = [sc[g].astype(BF16) for g in gs]
    for h in range(GLA_HEADS):
        sl = slice(h * GLA_DK, (h + 1) * GLA_DK)
        for g in gs:
            vb = pcol(g, OFF_GV + h * GLA_DV, GLA_DV).astype(BF16)
            st = s_sc[g, :, sl]
            o = _dot_nt(q_in[g][:, sl], st.astype(BF16)) + _dot(sc[g][:, h * c:(h + 1) * c], vb)
            s_sc[g, :, sl] = dec[g][:, sl] * st + _dot_tn(vb, k_dec[g][:, sl])
            r = pcol(g, OFF_GR + h * GLA_DV, GLA_DV)
            mix_sc[slot, rs[g], h * GLA_DV:(h + 1) * GLA_DV] = (
                _rms(o, gn_ref[...]) * (r * jax.nn.sigmoid(r)))
        yield


def _mlstm(bsm_ref, mn_ref, a2_ref, e2_ref, p_sc, ready, mix_sc, slot, c_sc, m_sc, *, c, streams):
    rows = streams * c
    gs = range(streams)
    rs = [slice(g * c, (g + 1) * c) for g in gs]
    pcol = _p_reader(p_sc, ready, rs)
    assert (OFF_SMALL, P_COLS) in ready
    lane = lax.broadcasted_iota(jnp.int32, (c, LANES), 1)
    row = lax.broadcasted_iota(jnp.int32, (c, LANES), 0)
    on_diag = row == lane
    causal = lane <= row
    ones = jnp.ones((c, LANES), F32)
    lane_all = lax.broadcasted_iota(jnp.int32, (rows, LANES), 1)
    is_f = (lane_all >= SM_F) & (lane_all < SM_F + M_HEADS)

    x = p_sc[:, OFF_SMALL:OFF_SMALL + LANES] + bsm_ref[...]
    x = jnp.where(is_f, _log_sigmoid(x), x)
    x_hi, x_lo = _split_hilo(x)
    yield
    xcum = jnp.concatenate(
        [_dot(a2_ref[0:c, :], jnp.concatenate([x_hi[rs[g]], x_lo[rs[g]]], axis=0)) for g in gs],
        axis=0)
    xc = jnp.where(is_f, xcum, x)
    yield
    xe_all = _dot(jnp.concatenate(_split_hilo(xc), axis=1), e2_ref[...])
    xe = [xe_all[rs[g]] for g in gs]
    yield
    for h in range(M_HEADS):
        ie = [xe[g][:, h * LANES:(h + 1) * LANES] for g in gs]
        fe = [xe[g][:, (M_HEADS + h) * LANES:(M_HEADS + h + 1) * LANES] for g in gs]
        a = [ie[g] - fe[g] for g in gs]
        a_row = [jnp.sum(jnp.where(on_diag, a[g], 0.0), axis=0, keepdims=True) for g in gs]
        dm = [jnp.where(causal, fe[g] + a_row[g], -jnp.inf)[:, :c] for g in gs]
        gi = [fe[g] + m_sc[g, h] for g in gs]
        m_t = [jnp.maximum(gi[g], jnp.max(dm[g], axis=-1, keepdims=True)) for g in gs]
        w_int = [jnp.exp(gi[g] - m_t[g]) for g in gs]
        w = [jnp.exp(dm[g] - m_t[g][:, :c]) for g in gs]
        yield
        qh = [pcol(g, OFF_MQ + h * M_DH, M_DH).astype(BF16) for g in gs]
        kf = [pcol(g, OFF_MK + h * M_DH, M_DH) * (M_DH ** -0.5) for g in gs]
        vaug = [jnp.concatenate([pcol(g, OFF_MV + h * M_DH, M_DH), ones], axis=1).astype(BF16)
                for g in gs]
        qk = [(_dot_nt(qh[g], kf[g].astype(BF16)) * w[g]).astype(BF16) for g in gs]
        yield
        caug = [c_sc[g, h] for g in gs]
        nd = [_dot(qk[g], vaug[g])
              + jnp.concatenate([w_int[g], w_int[g]], axis=1) * _dot(qh[g], caug[g].astype(BF16))
              for g in gs]
        for g in gs:
            den = jnp.maximum(jnp.abs(nd[g][:, M_DH:]), jnp.exp(-m_t[g]))
            og = jax.nn.sigmoid(pcol(g, OFF_MO + h * M_DH, M_DH)) * (nd[g][:, :M_DH] / den)
            mix_sc[slot, rs[g], GLA_V + h * M_DH:GLA_V + (h + 1) * M_DH] = _rms(og, mn_ref[...])
        yield
        for g in gs:
            m_c = m_t[g][c - 1:c, :]
            decay = jnp.exp(gi[g][c - 1:c, :] - m_c)
            ws = jnp.exp(a[g] + (fe[g][c - 1:c, :] - m_c))
            kw = (kf[g] * ws).astype(BF16)
            c_sc[g, h] = jnp.concatenate([decay, decay], axis=1) * caug[g] + _dot_tn(kw, vaug[g])
            m_sc[g, h] = m_c
        yield


def _back(xb_ref, mix_sc, slot_prev, wo_ref, n2_ref, wg_ref, wu_ref, cw_ref, cb_ref, wd_ref, nf_ref,
          y_ref, carry_sc, *, c, streams):
    rows = streams * c
    d = xb_ref.shape[-1]
    d_ff = wg_ref.shape[1]
    nslab = d_ff // FFN_SLAB
    x1 = xb_ref[...].reshape(rows, d) + _dot(mix_sc[slot_prev].astype(BF16), wo_ref[...])
    h2 = _rms(x1, n2_ref[...]).astype(BF16)
    yield

    def gate_up(s):
        cs = slice(s * FFN_SLAB, (s + 1) * FFN_SLAB)
        return _dot(h2, wg_ref[:, cs]), _dot(h2, wu_ref[:, cs])

    row8 = lax.broadcasted_iota(jnp.int32, (SUBLANES, FFN_SLAB), 0)
    nxt = gate_up(0)
    acts = []
    yield
    for s in range(nslab):
        cs = slice(s * FFN_SLAB, (s + 1) * FFN_SLAB)
        g, u = nxt
        if s + 1 < nslab:
            nxt = gate_up(s + 1)
        yield
        r1 = pltpu.roll(g, 1, axis=0)
        r2 = pltpu.roll(g, 2, axis=0)
        g1, g2 = [], []
        for st in range(streams):
            b0 = st * c
            p0 = carry_sc[st, 0:1, cs]
            p1 = carry_sc[st, 1:2, cs]
            g1 += [jnp.where(row8 == 0, p1, r1[b0:b0 + SUBLANES]), r1[b0 + SUBLANES:b0 + c]]
            g2 += [jnp.where(row8 == 0, p0, jnp.where(row8 == 1, p1, r2[b0:b0 + SUBLANES])),
                   r2[b0 + SUBLANES:b0 + c]]
            carry_sc[st, 0:1, cs] = g[b0 + c - 2:b0 + c - 1, :]
            carry_sc[st, 1:2, cs] = g[b0 + c - 1:b0 + c, :]
        g1 = jnp.concatenate(g1, axis=0)
        g2 = jnp.concatenate(g2, axis=0)
        gc = cb_ref[:, cs] + (cw_ref[0:1, cs] * g2 + cw_ref[1:2, cs] * g1 + cw_ref[2:3, cs] * g)
        yield
        acts.append((jax.nn.gelu(gc) * u).astype(BF16))
        yield
    act = jnp.concatenate(acts, axis=1)
    down = []
    for lo in range(0, d, DOWN_SLAB):
        down.append(_dot(act, wd_ref[:, lo:lo + DOWN_SLAB]))
        yield
    y_ref[...] = _rms(x1 + jnp.concatenate(down, axis=1), nf_ref[...]).reshape(streams, c, d)


def _layer_kernel(*refs, c, nlev, streams, nj, nsteps, skew, zero_state):
    n_in = 2 + N_WEIGHTS + (0 if zero_state else 5)
    (xf_ref, xb_ref, n1_ref, win_ref, wsm_ref, wa_ref, ba_ref, gn_ref, bsm_ref, mn_ref, a2_ref, lv_ref,
     e2_ref, bd_ref, wo_ref, n2_ref, wg_ref, wu_ref, cw_ref, cb_ref, wd_ref, nf_ref) = refs[:2 + N_WEIGHTS]
    s0_ref, c0_ref, n0_ref, m0_ref, cst_ref = (None,) * 5 if zero_state else refs[2 + N_WEIGHTS:n_in]
    y_ref, s_out, c_out, n_out, m_out, cv_out = refs[n_in:n_in + 6]
    p_sc, mix_sc, s_sc, c_sc, m_sc, carry_sc = refs[n_in + 6:]

    t = pl.program_id(0)
    tf = jnp.minimum(t, nsteps - 1)
    jf = lax.rem(tf, nj)
    tb = jnp.maximum(t - skew, 0)
    jb = lax.rem(tb, nj)
    slot = lax.rem(t, 2)
    slot_back = 1 - slot if skew else slot

    if skew:
        @pl.when(t == 0)
        def _():
            mix_sc[1] = jnp.zeros(mix_sc.shape[1:], F32)

    @pl.when(jf == 0)
    def _():
        if zero_state:
            s_sc[...] = jnp.zeros(s_sc.shape, F32)
            c_sc[...] = jnp.zeros(c_sc.shape, F32)
            m_sc[...] = jnp.zeros(m_sc.shape, F32)
        else:
            for g in range(streams):
                for lo in range(0, GLA_QK, LANES):
                    s_sc[g, :, lo:lo + LANES] = s0_ref[g, lo:lo + LANES, :].T
            c_sc[:, :, :, :M_DH] = c0_ref[...]
            c_sc[:, :, :, M_DH:] = jnp.broadcast_to(n0_ref[...], c0_ref.shape)
            m_sc[...] = m0_ref[...]

    @pl.when(jb == 0)
    def _():
        carry_sc[...] = jnp.zeros(carry_sc.shape, F32) if zero_state else cst_ref[...]

    back = _back(xb_ref, mix_sc, slot_back, wo_ref, n2_ref, wg_ref, wu_ref, cw_ref, cb_ref, wd_ref,
                 nf_ref, y_ref, carry_sc, c=c, streams=streams)
    if skew:
        next(back)
    ready = []
    for _ in _inproj(xf_ref, n1_ref, win_ref, wsm_ref, p_sc, ready, c=c, streams=streams):
        pass
    mixers = [_gla(wa_ref, ba_ref, gn_ref, a2_ref, lv_ref, bd_ref, p_sc, ready, mix_sc, slot, s_sc,
                   c=c, nlev=nlev, streams=streams),
              _mlstm(bsm_ref, mn_ref, a2_ref, e2_ref, p_sc, ready, mix_sc, slot, c_sc, m_sc,
                     c=c, streams=streams)]
    if skew:
        _interleave(mixers + [back], strides=(1, 1, 3))
    else:
        _interleave(mixers, strides=(1, 1))
        _interleave([back], strides=(1,))

    @pl.when((jf == nj - 1) & (t < nsteps))
    def _():
        for g in range(streams):
            for lo in range(0, GLA_QK, LANES):
                s_out[g, lo:lo + LANES, :] = s_sc[g, :, lo:lo + LANES].T
        c_out[...] = c_sc[:, :, :, :M_DH]
        on_diag = (lax.broadcasted_iota(jnp.int32, (M_DH, M_DH), 0)
                   == lax.broadcasted_iota(jnp.int32, (M_DH, M_DH), 1))
        for g in range(streams):
            for h in range(M_HEADS):
                n_out[g, h] = jnp.sum(jnp.where(on_diag, c_sc[g, h, :, M_DH:], 0.0), axis=0, keepdims=True)
        m_out[...] = m_sc[...]

    @pl.when((jb == nj - 1) & (t >= skew))
    def _():
        cv_out[...] = carry_sc[...]


def _layer(x, states, wts, streams):
    batch, length, d = x.shape
    c = min(CHUNK, length)
    nj = length // c
    nsteps = (batch // streams) * nj
    d_ff = wts["w_g"].shape[1]
    a2, lv, e2, bd, nlev = _mixer_consts(c)
    skew = 1 if nsteps > 1 else 0

    def front_idx(t):
        tf = jnp.minimum(t, nsteps - 1)
        return tf // nj, tf % nj

    def back_idx(t):
        tb = jnp.maximum(t - skew, 0)
        return tb // nj, tb % nj

    once = pl.Buffered(1)
    const = lambda a: pl.BlockSpec(a.shape, lambda t: (0,) * a.ndim, pipeline_mode=once)
    fstate = lambda *blk, **kw: pl.BlockSpec(
        (streams,) + blk, lambda t: (front_idx(t)[0],) + (0,) * len(blk), **kw)
    bstate = lambda *blk, **kw: pl.BlockSpec(
        (streams,) + blk, lambda t: (back_idx(t)[0],) + (0,) * len(blk), **kw)
    consts = [wts["norm1"], wts["w_in_r"], wts["w_in_sm"], wts["wa_pad"], wts["b_gla_a"], wts["gla_norm"],
              wts["b_small"], wts["m_norm"], a2, lv, e2, bd, wts["w_out"], wts["norm2"], wts["w_g"],
              wts["w_u"], wts["conv_w"], wts["conv_b"], wts["w_d"], wts["norm_f"]]
    assert len(consts) == N_WEIGHTS
    state_ops, state_specs = [], []
    if states is not None:
        s_gla, c_m, n_m, m_m, conv_st = states
        state_ops = [
            s_gla.reshape(batch, GLA_QK, GLA_DV),
            c_m, n_m[..., None],
            jnp.broadcast_to(m_m[..., None, None], (batch, M_HEADS, 1, LANES)), conv_st]
        state_specs = [
            fstate(GLA_QK, GLA_DV, pipeline_mode=once),
            fstate(M_HEADS, M_DH, M_DH, pipeline_mode=once),
            fstate(M_HEADS, M_DH, 1, pipeline_mode=once),
            fstate(M_HEADS, 1, LANES, pipeline_mode=once),
            bstate(FFN_CONV - 1, d_ff, pipeline_mode=once)]
    outs = pl.pallas_call(
        functools.partial(_layer_kernel, c=c, nlev=nlev, streams=streams, nj=nj, nsteps=nsteps,
                          skew=skew, zero_state=states is None),
        grid=(nsteps + skew,),
        in_specs=[
            pl.BlockSpec((streams, c, d), lambda t: front_idx(t) + (0,)),
            pl.BlockSpec((streams, c, d), lambda t: back_idx(t) + (0,)),
        ] + [const(a) for a in consts] + state_specs,
        out_specs=[
            pl.BlockSpec((streams, c, d), lambda t: back_idx(t) + (0,)),
            fstate(GLA_QK, GLA_DV), fstate(M_HEADS, M_DH, M_DH), fstate(M_HEADS, 1, M_DH),
            fstate(M_HEADS, 1, LANES), bstate(FFN_CONV - 1, d_ff),
        ],
        out_shape=[
            jax.ShapeDtypeStruct((batch, length, d), F32),
            jax.ShapeDtypeStruct((batch, GLA_QK, GLA_DV), F32),
            jax.ShapeDtypeStruct((batch, M_HEADS, M_DH, M_DH), F32),
            jax.ShapeDtypeStruct((batch, M_HEADS, 1, M_DH), F32),
            jax.ShapeDtypeStruct((batch, M_HEADS, 1, LANES), F32),
            jax.ShapeDtypeStruct((batch, FFN_CONV - 1, d_ff), F32),
        ],
        scratch_shapes=[
            pltpu.VMEM((streams * c, P_COLS), F32),
            pltpu.VMEM((2, streams * c, D_MIX), F32),
            pltpu.VMEM((streams, GLA_DV, GLA_QK), F32),
            pltpu.VMEM((streams, M_HEADS, M_DH, 2 * M_DH), F32),
            pltpu.VMEM((streams, M_HEADS, 1, LANES), F32),
            pltpu.VMEM((streams, FFN_CONV - 1, d_ff), F32),
        ],
        compiler_params=pltpu.CompilerParams(
            dimension_semantics=("arbitrary",), vmem_limit_bytes=VMEM_LIMIT),
        name="layer",
    )(x, x, *consts, *state_ops)
    y, s_new, c_new, n_new, m_e, conv_new = outs
    return (y, s_new.reshape(batch, GLA_HEADS, GLA_DK, GLA_DV), c_new,
            n_new.reshape(batch, M_HEADS, M_DH), m_e[:, :, 0, 0], conv_new)


def _regroup_kernel(wg_ref, wm_ref, glr_ref, mif_ref, o_ref, osm_ref, *, n_gla):
    j = pl.program_id(0)

    @pl.when(j < n_gla)
    def _():
        o_ref[...] = wg_ref[...].T.astype(BF16)

    @pl.when(j >= n_gla)
    def _():
        o_ref[...] = wm_ref[...].T.astype(BF16)

    @pl.when(j == 0)
    def _():
        d = wg_ref.shape[1]
        small = jnp.concatenate(
            [glr_ref[...], mif_ref[...], jnp.zeros((LANES - GLA_RANK - 2 * M_HEADS, d), F32)], axis=0)
        osm_ref[...] = small.T.astype(BF16)


def _regroup_w_in(w):
    d = w.shape[0]
    wt = w.T
    o_glr = OFF_MQ
    o_m = o_glr + GLA_RANK
    o_if = o_m + 4 * M_W
    nslab = OFF_SMALL // INPROJ_SLAB
    n_gla = OFF_MQ // INPROJ_SLAB
    rows = lambda n: (pl.Element(n), pl.Element(d))
    return pl.pallas_call(
        functools.partial(_regroup_kernel, n_gla=n_gla),
        grid=(nslab,),
        in_specs=[
            pl.BlockSpec(rows(INPROJ_SLAB), lambda j: (jnp.minimum(j, n_gla - 1) * INPROJ_SLAB, 0)),
            pl.BlockSpec(rows(INPROJ_SLAB), lambda j: (
                (o_m // GLA_RANK + jnp.maximum(j - n_gla, 0) * (INPROJ_SLAB // GLA_RANK)) * GLA_RANK, 0)),
            pl.BlockSpec(rows(GLA_RANK), lambda j: (o_glr, 0)),
            pl.BlockSpec(rows(2 * M_HEADS), lambda j: (o_if, 0)),
        ],
        out_specs=[
            pl.BlockSpec((d, INPROJ_SLAB), lambda j: (0, j)),
            pl.BlockSpec((d, LANES), lambda j: (0, 0)),
        ],
        out_shape=[
            jax.ShapeDtypeStruct((d, OFF_SMALL), BF16),
            jax.ShapeDtypeStruct((d, LANES), BF16),
        ],
        compiler_params=pltpu.CompilerParams(dimension_semantics=("arbitrary",)),
        name="regroup_w_in",
    )(wt, wt, wt, wt)


def kernel(x_prompt, x_sample, state_gla, state_mlstm_C, state_mlstm_n, state_mlstm_m, cache_ffn_conv,
           norm1, w_in, w_gla_a, b_gla_a, gla_norm, b_m_i, b_m_f, m_norm, w_out, norm2, w_g, w_u,
           conv_w, conv_b, w_d, norm_f):
    assert w_in.shape[0] == 1, "single-layer stack"
    w_in_r, w_in_sm = _regroup_w_in(w_in[0])
    wa_pad = jnp.concatenate(
        [w_gla_a[0], jnp.zeros((LANES - GLA_RANK, GLA_QK), w_gla_a.dtype)], axis=0).astype(BF16)
    b_small = jnp.concatenate(
        [jnp.zeros((GLA_RANK,), F32), b_m_i[0], b_m_f[0],
         jnp.zeros((LANES - GLA_RANK - 2 * M_HEADS,), F32)])[None]
    wts = dict(
        norm1=norm1, w_in_r=w_in_r, w_in_sm=w_in_sm, wa_pad=wa_pad, b_gla_a=b_gla_a, gla_norm=gla_norm,
        b_small=b_small, m_norm=m_norm, w_out=w_out[0].astype(BF16), norm2=norm2,
        w_g=w_g[0].astype(BF16), w_u=w_u[0].astype(BF16), conv_w=conv_w[0], conv_b=conv_b,
        w_d=w_d[0].astype(BF16), norm_f=norm_f[None])

    outs_p = _layer(x_prompt, None, wts, streams=4)
    outs_s = _layer(x_sample, (state_gla[0], state_mlstm_C[0], state_mlstm_n[0], state_mlstm_m[0],
                               cache_ffn_conv[0]), wts, streams=x_sample.shape[0])
    y_p, *st_p = outs_p
    y_s, *st_s = outs_s
    return (y_p, y_s) + tuple(s[None] for s in st_p) + tuple(s[None] for s in st_s)
```

```python
import functools

import numpy as np
import jax
import jax.numpy as jnp
from jax import lax
from jax.experimental import pallas as pl
from jax.experimental.pallas import tpu as pltpu

F32 = jnp.float32
BF16 = jnp.bfloat16

EPS = 1e-6
CHUNK = 64
GLA_HEADS = 4
GLA_DK = 64
GLA_DV = 128
GLA_RANK = 16
GLA_TAU = 16.0
M_HEADS = 4
M_DH = 128
FFN_CONV = 3
LANES = 128
SUBLANES = 8
MXU_ROWS = 256

GLA_QK = GLA_HEADS * GLA_DK
GLA_V = GLA_HEADS * GLA_DV
M_W = M_HEADS * M_DH
D_MIX = GLA_V + M_W
OFF_GQ = 0
OFF_GK = OFF_GQ + GLA_QK
OFF_GV = OFF_GK + GLA_QK
OFF_GR = OFF_GV + GLA_V
OFF_MQ = OFF_GR + GLA_V
OFF_MK = OFF_MQ + M_W
OFF_MV = OFF_MK + M_W
OFF_MO = OFF_MV + M_W
OFF_SMALL = OFF_MO + M_W
P_COLS = OFF_SMALL + LANES
SM_I = GLA_RANK
SM_F = GLA_RANK + M_HEADS

INPROJ_SLAB = 512
FFN_SLAB = 256
DOWN_SLAB = 256
VMEM_LIMIT = 60 * 1024 * 1024
N_WEIGHTS = 20


def _dot(a, b):
    return jnp.dot(a, b, preferred_element_type=F32)


def _dot_nt(a, b):
    return lax.dot_general(a, b, (((1,), (1,)), ((), ())), preferred_element_type=F32)


def _dot_tn(a, b):
    return lax.dot_general(a, b, (((0,), (0,)), ((), ())), preferred_element_type=F32)


def _split_hilo(x):
    hi = x.astype(BF16)
    lo = (x - hi.astype(F32)).astype(BF16)
    return hi, lo


def _log_sigmoid(x):
    return jnp.minimum(x, 0.0) - jnp.log1p(jnp.exp(-jnp.abs(x)))


def _rms(x, g):
    return x * lax.rsqrt(jnp.mean(x * x, axis=-1, keepdims=True) + EPS) * g


def _interleave(tasks, strides):
    alive = list(zip(tasks, strides))
    while alive:
        for item in list(alive):
            task, stride = item
            for _ in range(stride):
                try:
                    next(task)
                except StopIteration:
                    alive.remove(item)
                    break


def _mixer_consts(c):
    nlev = int(np.log2(c))
    assert 2 ** nlev == c
    t = np.arange(c)[:, None]
    u = np.arange(c)[None, :]
    blocks = [(u <= t), (u > t)]
    for l in range(1, nlev + 1):
        bsz, half = 2 ** l, 2 ** (l - 1)
        mid = (t // bsz) * bsz + half
        right = (t % bsz) >= half
        blocks.append(np.where(right, (u >= mid) & (u <= t), (u > t) & (u < mid)))
    a = np.concatenate(blocks, axis=0).astype(np.float32)
    a2 = np.concatenate([a, a], axis=1)
    x = np.bitwise_xor(t, u)
    lv = np.where(u > t, -1, np.where(x == 0, 0, np.floor(np.log2(np.maximum(x, 1))) + 1))
    e = np.zeros((LANES, 2 * M_HEADS * LANES), np.float32)
    for h in range(M_HEADS):
        e[SM_I + h, h * LANES:(h + 1) * LANES] = 1.0
        e[SM_F + h, (M_HEADS + h) * LANES:(M_HEADS + h + 1) * LANES] = 1.0
    e2 = np.concatenate([e, e], axis=0)
    lv4 = np.tile(lv, (1, GLA_HEADS))
    bd = (np.arange(GLA_HEADS * c)[:, None] // c == np.arange(GLA_QK)[None, :] // GLA_DK)
    return (jnp.asarray(a2, BF16), jnp.asarray(lv4, jnp.int32), jnp.asarray(e2, BF16),
            jnp.asarray(bd, BF16), nlev)


GATE_SLAB = (OFF_SMALL, P_COLS)
INPROJ_ORDER = [(lo, lo + INPROJ_SLAB) for lo in (OFF_GQ, OFF_MQ, OFF_MK, OFF_MV, OFF_MO, OFF_GV, OFF_GR)]


def _inproj(h_in, win_ref, wsm_ref, p_sc, ready, slabs):
    for lo, hi in slabs:
        p_sc[:, lo:hi] = _dot(h_in, wsm_ref[...] if (lo, hi) == GATE_SLAB else win_ref[:, lo:hi])
        ready.append((lo, hi))
        yield


def _p_reader(p_sc, ready, rs):
    def pcol(g, off, width):
        assert any(lo <= off and off + width <= hi for lo, hi in ready), (off, width)
        return p_sc[rs[g], off:off + width]
    return pcol


def _gla(wa_ref, ba_ref, gn_ref, a2_ref, lv_ref, bd_ref, p_sc, ready, mix_sc, slot, s_sc, *, c, nlev,
         streams):
    gs = range(streams)
    rs = [slice(g * c, (g + 1) * c) for g in gs]
    pcol = _p_reader(p_sc, ready, rs)
    assert (OFF_SMALL, P_COLS) in ready
    small = p_sc[:, OFF_SMALL:OFF_SMALL + LANES]
    xa = _dot(small.astype(BF16), wa_ref[...]) + ba_ref[...]
    la_hi, la_lo = _split_hilo(_log_sigmoid(xa) * (1.0 / GLA_TAU))
    yield
    z = [_dot(a2_ref[...], jnp.concatenate([la_hi[rs[g]], la_lo[rs[g]]], axis=0))
         for g in gs]
    yield
    ez = [jnp.exp(z[g]) for g in gs]
    yield
    q = [pcol(g, OFF_GQ, GLA_QK) * (GLA_DK ** -0.5) for g in gs]
    k = [pcol(g, OFF_GK, GLA_QK) for g in gs]
    q_in = [(q[g] * ez[g][0:c]).astype(BF16) for g in gs]
    k_dec = [(k[g] * ez[g][c:2 * c]).astype(BF16) for g in gs]
    dec = [ez[g][c - 1:c, :] for g in gs]
    q_lv = [[q[g].astype(BF16)] + [(q[g] * ez[g][(2 + l) * c:(3 + l) * c]).astype(BF16)
                                   for l in range(nlev)] for g in gs]
    k_lv = [[k[g].astype(BF16)] + [(k[g] * ez[g][(2 + l) * c:(3 + l) * c]).astype(BF16)
                                   for l in range(nlev)] for g in gs]
    yield
    lv = lv_ref[...]
    bd = bd_ref[...]
    sc = [jnp.zeros((c, GLA_HEADS * c), F32) for g in gs]
    for l in range(nlev + 1):
        for g in gs:
            k_bd = jnp.concatenate([k_lv[g][l]] * GLA_HEADS, axis=0) * bd
            sc[g] = jnp.where(lv == l, _dot_nt(q_lv[g][l], k_bd), sc[g])
        yield
    sc = [sc[g].astype(BF16) for g in gs]
    for h in range(GLA_HEADS):
        sl = slice(h * GLA_DK, (h + 1) * GLA_DK)
        for g in gs:
            vb = pcol(g, OFF_GV + h * GLA_DV, GLA_DV).astype(BF16)
            st = s_sc[g, :, sl]
            o = _dot_nt(q_in[g][:, sl], st.astype(BF16)) + _dot(sc[g][:, h * c:(h + 1) * c], vb)
            s_sc[g, :, sl] = dec[g][:, sl] * st + _dot_tn(vb, k_dec[g][:, sl])
            r = pcol(g, OFF_GR + h * GLA_DV, GLA_DV)
            mix_sc[slot, rs[g], h * GLA_DV:(h + 1) * GLA_DV] = (
                _rms(o, gn_ref[...]) * (r * jax.nn.sigmoid(r))).astype(BF16)
        yield


def _mlstm(bsm_ref, mn_ref, a2_ref, e2_ref, p_sc, ready, mix_sc, slot, c_sc, m_sc, *, c, streams):
    rows = streams * c
    gs = range(streams)
    rs = [slice(g * c, (g + 1) * c) for g in gs]
    pcol = _p_reader(p_sc, ready, rs)
    assert (OFF_SMALL, P_COLS) in ready
    lane = lax.broadcasted_iota(jnp.int32, (c, LANES), 1)
    row = lax.broadcasted_iota(jnp.int32, (c, LANES), 0)
    on_diag = row == lane
    causal = lane <= row
    ones = jnp.ones((c, LANES), F32)
    lane_all = lax.broadcasted_iota(jnp.int32, (rows, LANES), 1)
    is_f = (lane_all >= SM_F) & (lane_all < SM_F + M_HEADS)

    x = p_sc[:, OFF_SMALL:OFF_SMALL + LANES] + bsm_ref[...]
    x = jnp.where(is_f, _log_sigmoid(x), x)
    x_hi, x_lo = _split_hilo(x)
    yield
    xcum = jnp.concatenate(
        [_dot(a2_ref[0:c, :], jnp.concatenate([x_hi[rs[g]], x_lo[rs[g]]], axis=0)) for g in gs],
        axis=0)
    xc = jnp.where(is_f, xcum, x)
    yield
    xe_all = _dot(jnp.concatenate(_split_hilo(xc), axis=1), e2_ref[...])
    xe = [xe_all[rs[g]] for g in gs]
    yield
    for h in range(M_HEADS):
        ie = [xe[g][:, h * LANES:(h + 1) * LANES] for g in gs]
        fe = [xe[g][:, (M_HEADS + h) * LANES:(M_HEADS + h + 1) * LANES] for g in gs]
        a = [ie[g] - fe[g] for g in gs]
        a_row = [jnp.sum(jnp.where(on_diag, a[g], 0.0), axis=0, keepdims=True) for g in gs]
        dm = [jnp.where(causal, fe[g] + a_row[g], -jnp.inf)[:, :c] for g in gs]
        gi = [fe[g] + m_sc[g, h] for g in gs]
        m_t = [jnp.maximum(gi[g], jnp.max(dm[g], axis=-1, keepdims=True)) for g in gs]
        w_int = [jnp.exp(gi[g] - m_t[g]) for g in gs]
        w = [jnp.exp(dm[g] - m_t[g][:, :c]) for g in gs]
        yield
        qh = [pcol(g, OFF_MQ + h * M_DH, M_DH).astype(BF16) for g in gs]
        kf = [pcol(g, OFF_MK + h * M_DH, M_DH) * (M_DH ** -0.5) for g in gs]
        vaug = [jnp.concatenate([pcol(g, OFF_MV + h * M_DH, M_DH), ones], axis=1).astype(BF16)
                for g in gs]
        qk = [(_dot_nt(qh[g], kf[g].astype(BF16)) * w[g]).astype(BF16) for g in gs]
        yield
        caug = [c_sc[g, h] for g in gs]
        nd = [_dot(qk[g], vaug[g])
              + jnp.concatenate([w_int[g], w_int[g]], axis=1) * _dot(qh[g], caug[g].astype(BF16))
              for g in gs]
        for g in gs:
            den = jnp.maximum(jnp.abs(nd[g][:, M_DH:]), jnp.exp(-m_t[g]))
            og = jax.nn.sigmoid(pcol(g, OFF_MO + h * M_DH, M_DH)) * (nd[g][:, :M_DH] / den)
            mix_sc[slot, rs[g], GLA_V + h * M_DH:GLA_V + (h + 1) * M_DH] = (
                _rms(og, mn_ref[...]).astype(BF16))
        yield
        for g in gs:
            m_c = m_t[g][c - 1:c, :]
            decay = jnp.exp(gi[g][c - 1:c, :] - m_c)
            ws = jnp.exp(a[g] + (fe[g][c - 1:c, :] - m_c))
            kw = (kf[g] * ws).astype(BF16)
            c_sc[g, h] = jnp.concatenate([decay, decay], axis=1) * caug[g] + _dot_tn(kw, vaug[g])
            m_sc[g, h] = m_c
        yield


def _back_open(xb_ref, mix_sc, mix_rd, wo_ref, n2_ref, x1_sc, h2_sc, slot, *, c, streams):
    rows = streams * c
    d = xb_ref.shape[-1]
    x1 = xb_ref[...].reshape(rows, d) + _dot(mix_sc[mix_rd], wo_ref[...])
    x1_sc[slot] = x1
    h2_sc[slot] = _rms(x1, n2_ref[...]).astype(BF16)


def _back_gate(h2_sc, slot, wg_ref, g_sc):
    for lo in range(0, wg_ref.shape[1], FFN_SLAB):
        g_sc[:, lo:lo + FFN_SLAB] = _dot(h2_sc[slot], wg_ref[:, lo:lo + FFN_SLAB])
        yield


def _back_act(h2_sc, slot, wu_ref, cw_ref, cb_ref, g_sc, carry_sc, acts, *, c, streams):
    nslab = wu_ref.shape[1] // FFN_SLAB

    def up(s):
        return _dot(h2_sc[slot], wu_ref[:, s * FFN_SLAB:(s + 1) * FFN_SLAB])

    row8 = lax.broadcasted_iota(jnp.int32, (SUBLANES, FFN_SLAB), 0)
    nxt = up(0)
    for s in range(nslab):
        cs = slice(s * FFN_SLAB, (s + 1) * FFN_SLAB)
        u = nxt
        if s + 1 < nslab:
            nxt = up(s + 1)
        yield
        g = g_sc[:, cs]
        r1 = pltpu.roll(g, 1, axis=0)
        r2 = pltpu.roll(g, 2, axis=0)
        g1, g2 = [], []
        for st in range(streams):
            b0 = st * c
            p0 = carry_sc[st, 0:1, cs]
            p1 = carry_sc[st, 1:2, cs]
            g1 += [jnp.where(row8 == 0, p1, r1[b0:b0 + SUBLANES]), r1[b0 + SUBLANES:b0 + c]]
            g2 += [jnp.where(row8 == 0, p0, jnp.where(row8 == 1, p1, r2[b0:b0 + SUBLANES])),
                   r2[b0 + SUBLANES:b0 + c]]
            carry_sc[st, 0:1, cs] = g[b0 + c - 2:b0 + c - 1, :]
            carry_sc[st, 1:2, cs] = g[b0 + c - 1:b0 + c, :]
        g1 = jnp.concatenate(g1, axis=0)
        g2 = jnp.concatenate(g2, axis=0)
        gc = cb_ref[:, cs] + (cw_ref[0:1, cs] * g2 + cw_ref[1:2, cs] * g1 + cw_ref[2:3, cs] * g)
        yield
        acts.append((jax.nn.gelu(gc) * u).astype(BF16))
        yield


def _back_down(acts, wd_ref, x1_sc, slot, nf_ref, y_ref, *, c, streams):
    d = wd_ref.shape[1]
    act = jnp.concatenate(acts, axis=1)
    down = []
    for lo in range(0, d, DOWN_SLAB):
        down.append(_dot(act, wd_ref[:, lo:lo + DOWN_SLAB]))
        yield
    y_ref[...] = _rms(x1_sc[slot] + jnp.concatenate(down, axis=1), nf_ref[...]).reshape(streams, c, d)


def _chain(*tasks):
    for task in tasks:
        yield from task


def _layer_kernel(*refs, c, nlev, streams, nj, nsteps, skew, zero_state):
    n_in = 2 + N_WEIGHTS + (0 if zero_state else 5)
    (xf_ref, xb_ref, n1_ref, win_ref, wsm_ref, wa_ref, ba_ref, gn_ref, bsm_ref, mn_ref, a2_ref, lv_ref,
     e2_ref, bd_ref, wo_ref, n2_ref, wg_ref, wu_ref, cw_ref, cb_ref, wd_ref, nf_ref) = refs[:2 + N_WEIGHTS]
    s0_ref, c0_ref, n0_ref, m0_ref, cst_ref = (None,) * 5 if zero_state else refs[2 + N_WEIGHTS:n_in]
    y_ref, s_out, c_out, n_out, m_out, cv_out = refs[n_in:n_in + 6]
    p_sc, mix_sc, s_sc, c_sc, m_sc, carry_sc, g_sc, x1_sc, h2_sc = refs[n_in + 6:]

    t = pl.program_id(0)
    tf = jnp.minimum(t, nsteps - 1)
    jf = lax.rem(tf, nj)
    tb = jnp.maximum(t - 2 * skew, 0)
    jb = lax.rem(tb, nj)
    if skew:
        cur = lax.rem(t, 2)
        prv = 1 - cur

        @pl.when(t == 0)
        def _():
            mix_sc[1] = jnp.zeros(mix_sc.shape[1:], BF16)
            g_sc[...] = jnp.zeros(g_sc.shape, F32)
            x1_sc[0] = jnp.zeros(x1_sc.shape[1:], F32)
            h2_sc[0] = jnp.zeros(h2_sc.shape[1:], BF16)
    else:
        cur = prv = 0

    @pl.when(jf == 0)
    def _():
        if zero_state:
            s_sc[...] = jnp.zeros(s_sc.shape, F32)
            c_sc[...] = jnp.zeros(c_sc.shape, F32)
            m_sc[...] = jnp.zeros(m_sc.shape, F32)
        else:
            for g in range(streams):
                for lo in range(0, GLA_QK, LANES):
                    s_sc[g, :, lo:lo + LANES] = s0_ref[g, lo:lo + LANES, :].T
            c_sc[:, :, :, :M_DH] = c0_ref[...]
            c_sc[:, :, :, M_DH:] = jnp.broadcast_to(n0_ref[...], c0_ref.shape)
            m_sc[...] = m0_ref[...]

    @pl.when(jb == 0)
    def _():
        carry_sc[...] = jnp.zeros(carry_sc.shape, F32) if zero_state else cst_ref[...]

    ready = []
    acts = []
    kw = dict(c=c, streams=streams)
    inproj = lambda slabs: _inproj(h_in, win_ref, wsm_ref, p_sc, ready, slabs)
    gla = _gla(wa_ref, ba_ref, gn_ref, a2_ref, lv_ref, bd_ref, p_sc, ready, mix_sc, cur, s_sc,
               nlev=nlev, **kw)
    mlstm = _mlstm(bsm_ref, mn_ref, a2_ref, e2_ref, p_sc, ready, mix_sc, cur, c_sc, m_sc, **kw)
    gate = _back_gate(h2_sc, prv, wg_ref, g_sc)
    act = _back_act(h2_sc, cur, wu_ref, cw_ref, cb_ref, g_sc, carry_sc, acts, **kw)
    down = _back_down(acts, wd_ref, x1_sc, cur, nf_ref, y_ref, **kw)
    if skew:
        _back_open(xb_ref, mix_sc, prv, wo_ref, n2_ref, x1_sc, h2_sc, prv, **kw)
        h_in = _rms(xf_ref[...].reshape(streams * c, xf_ref.shape[-1]), n1_ref[...]).astype(BF16)
        n_lead = 2
        _interleave([inproj([GATE_SLAB] + INPROJ_ORDER[:n_lead - 1]), act], strides=(1, 17))
        _interleave([inproj(INPROJ_ORDER[n_lead - 1:]), gla, mlstm, _chain(gate, down)],
                    strides=(1, 1, 1, 1))
    else:
        h_in = _rms(xf_ref[...].reshape(streams * c, xf_ref.shape[-1]), n1_ref[...]).astype(BF16)
        _interleave([inproj([GATE_SLAB] + INPROJ_ORDER)], strides=(1,))
        _interleave([gla, mlstm], strides=(1, 1))
        _back_open(xb_ref, mix_sc, cur, wo_ref, n2_ref, x1_sc, h2_sc, cur, **kw)
        _interleave([_chain(gate, act, down)], strides=(1,))

    @pl.when((jf == nj - 1) & (t < nsteps))
    def _():
        for g in range(streams):
            for lo in range(0, GLA_QK, LANES):
                s_out[g, lo:lo + LANES, :] = s_sc[g, :, lo:lo + LANES].T
        c_out[...] = c_sc[:, :, :, :M_DH]
        on_diag = (lax.broadcasted_iota(jnp.int32, (M_DH, M_DH), 0)
                   == lax.broadcasted_iota(jnp.int32, (M_DH, M_DH), 1))
        for g in range(streams):
            for h in range(M_HEADS):
                n_out[g, h] = jnp.sum(jnp.where(on_diag, c_sc[g, h, :, M_DH:], 0.0), axis=0, keepdims=True)
        m_out[...] = m_sc[...]

    @pl.when((jb == nj - 1) & (t >= 2 * skew))
    def _():
        cv_out[...] = carry_sc[...]


def _layer(x, states, wts):
    batch, length, d = x.shape
    c = min(CHUNK, length)
    nj = length // c
    streams = min(batch, MXU_ROWS // c)
    assert batch % streams == 0 and length % c == 0
    nsteps = (batch // streams) * nj
    d_ff = wts["w_g"].shape[1]
    a2, lv, e2, bd, nlev = _mixer_consts(c)
    skew = 1 if nsteps > 1 else 0

    def front_idx(t):
        tf = jnp.minimum(t, nsteps - 1)
        return tf // nj, tf % nj

    def mid_idx(t):
        tm = jnp.clip(t - skew, 0, nsteps - 1)
        return tm // nj, tm % nj

    def back_idx(t):
        tb = jnp.maximum(t - 2 * skew, 0)
        return tb // nj, tb % nj

    once = pl.Buffered(1)
    const = lambda a: pl.BlockSpec(a.shape, lambda t: (0,) * a.ndim, pipeline_mode=once)
    fstate = lambda *blk, **kw: pl.BlockSpec(
        (streams,) + blk, lambda t: (front_idx(t)[0],) + (0,) * len(blk), **kw)
    bstate = lambda *blk, **kw: pl.BlockSpec(
        (streams,) + blk, lambda t: (back_idx(t)[0],) + (0,) * len(blk), **kw)
    consts = [wts["norm1"], wts["w_in_r"], wts["w_in_sm"], wts["wa_pad"], wts["b_gla_a"], wts["gla_norm"],
              wts["b_small"], wts["m_norm"], a2, lv, e2, bd, wts["w_out"], wts["norm2"], wts["w_g"],
              wts["w_u"], wts["conv_w"], wts["conv_b"], wts["w_d"], wts["norm_f"]]
    assert len(consts) == N_WEIGHTS
    state_ops, state_specs = [], []
    if states is not None:
        s_gla, c_m, n_m, m_m, conv_st = states
        state_ops = [
            s_gla.reshape(batch, GLA_QK, GLA_DV),
            c_m, n_m[..., None],
            jnp.broadcast_to(m_m[..., None, None], (batch, M_HEADS, 1, LANES)), conv_st]
        state_specs = [
            fstate(GLA_QK, GLA_DV, pipeline_mode=once),
            fstate(M_HEADS, M_DH, M_DH, pipeline_mode=once),
            fstate(M_HEADS, M_DH, 1, pipeline_mode=once),
            fstate(M_HEADS, 1, LANES, pipeline_mode=once),
            bstate(FFN_CONV - 1, d_ff, pipeline_mode=once)]
    outs = pl.pallas_call(
        functools.partial(_layer_kernel, c=c, nlev=nlev, streams=streams, nj=nj, nsteps=nsteps,
                          skew=skew, zero_state=states is None),
        grid=(nsteps + 2 * skew,),
        in_specs=[
            pl.BlockSpec((streams, c, d), lambda t: front_idx(t) + (0,)),
            pl.BlockSpec((streams, c, d), lambda t: mid_idx(t) + (0,)),
        ] + [const(a) for a in consts] + state_specs,
        out_specs=[
            pl.BlockSpec((streams, c, d), lambda t: back_idx(t) + (0,)),
            fstate(GLA_QK, GLA_DV), fstate(M_HEADS, M_DH, M_DH), fstate(M_HEADS, 1, M_DH),
            fstate(M_HEADS, 1, LANES), bstate(FFN_CONV - 1, d_ff),
        ],
        out_shape=[
            jax.ShapeDtypeStruct((batch, length, d), F32),
            jax.ShapeDtypeStruct((batch, GLA_QK, GLA_DV), F32),
            jax.ShapeDtypeStruct((batch, M_HEADS, M_DH, M_DH), F32),
            jax.ShapeDtypeStruct((batch, M_HEADS, 1, M_DH), F32),
            jax.ShapeDtypeStruct((batch, M_HEADS, 1, LANES), F32),
            jax.ShapeDtypeStruct((batch, FFN_CONV - 1, d_ff), F32),
        ],
        scratch_shapes=[
            pltpu.VMEM((streams * c, P_COLS), F32),
            pltpu.VMEM((2, streams * c, D_MIX), BF16),
            pltpu.VMEM((streams, GLA_DV, GLA_QK), F32),
            pltpu.VMEM((streams, M_HEADS, M_DH, 2 * M_DH), F32),
            pltpu.VMEM((streams, M_HEADS, 1, LANES), F32),
            pltpu.VMEM((streams, FFN_CONV - 1, d_ff), F32),
            pltpu.VMEM((streams * c, d_ff), F32),
            pltpu.VMEM((2, streams * c, d), F32),
            pltpu.VMEM((2, streams * c, d), BF16),
        ],
        compiler_params=pltpu.CompilerParams(
            dimension_semantics=("arbitrary",), vmem_limit_bytes=VMEM_LIMIT),
        name="layer",
    )(x, x, *consts, *state_ops)
    y, s_new, c_new, n_new, m_e, conv_new = outs
    return (y, s_new.reshape(batch, GLA_HEADS, GLA_DK, GLA_DV), c_new,
            n_new.reshape(batch, M_HEADS, M_DH), m_e[:, :, 0, 0], conv_new)


def _regroup_kernel(wg_ref, wm_ref, glr_ref, mif_ref, o_ref, osm_ref, *, n_gla):
    j = pl.program_id(0)

    @pl.when(j < n_gla)
    def _():
        o_ref[...] = wg_ref[...].T.astype(BF16)

    @pl.when(j >= n_gla)
    def _():
        o_ref[...] = wm_ref[...].T.astype(BF16)

    @pl.when(j == 0)
    def _():
        d = wg_ref.shape[1]
        small = jnp.concatenate(
            [glr_ref[...], mif_ref[...], jnp.zeros((LANES - GLA_RANK - 2 * M_HEADS, d), F32)], axis=0)
        osm_ref[...] = small.T.astype(BF16)


def _regroup_w_in(w):
    d = w.shape[0]
    wt = w.T
    o_glr = OFF_MQ
    o_m = o_glr + GLA_RANK
    o_if = o_m + 4 * M_W
    nslab = OFF_SMALL // INPROJ_SLAB
    n_gla = OFF_MQ // INPROJ_SLAB
    rows = lambda n: (pl.Element(n), pl.Element(d))
    return pl.pallas_call(
        functools.partial(_regroup_kernel, n_gla=n_gla),
        grid=(nslab,),
        in_specs=[
            pl.BlockSpec(rows(INPROJ_SLAB), lambda j: (jnp.minimum(j, n_gla - 1) * INPROJ_SLAB, 0)),
            pl.BlockSpec(rows(INPROJ_SLAB), lambda j: (
                (o_m // GLA_RANK + jnp.maximum(j - n_gla, 0) * (INPROJ_SLAB // GLA_RANK)) * GLA_RANK, 0)),
            pl.BlockSpec(rows(GLA_RANK), lambda j: (o_glr, 0)),
            pl.BlockSpec(rows(2 * M_HEADS), lambda j: (o_if, 0)),
        ],
        out_specs=[
            pl.BlockSpec((d, INPROJ_SLAB), lambda j: (0, j)),
            pl.BlockSpec((d, LANES), lambda j: (0, 0)),
        ],
        out_shape=[
            jax.ShapeDtypeStruct((d, OFF_SMALL), BF16),
            jax.ShapeDtypeStruct((d, LANES), BF16),
        ],
        compiler_params=pltpu.CompilerParams(dimension_semantics=("arbitrary",)),
        name="regroup_w_in",
    )(wt, wt, wt, wt)


def kernel(x_prompt, x_sample, state_gla, state_mlstm_C, state_mlstm_n, state_mlstm_m, cache_ffn_conv,
           norm1, w_in, w_gla_a, b_gla_a, gla_norm, b_m_i, b_m_f, m_norm, w_out, norm2, w_g, w_u,
           conv_w, conv_b, w_d, norm_f):
    assert w_in.shape[0] == 1, "single-layer stack"
    w_in_r, w_in_sm = _regroup_w_in(w_in[0])
    wa_pad = jnp.concatenate(
        [w_gla_a[0], jnp.zeros((LANES - GLA_RANK, GLA_QK), w_gla_a.dtype)], axis=0).astype(BF16)
    b_small = jnp.concatenate(
        [jnp.zeros((GLA_RANK,), F32), b_m_i[0], b_m_f[0],
         jnp.zeros((LANES - GLA_RANK - 2 * M_HEADS,), F32)])[None]
    wts = dict(
        norm1=norm1, w_in_r=w_in_r, w_in_sm=w_in_sm, wa_pad=wa_pad, b_gla_a=b_gla_a, gla_norm=gla_norm,
        b_small=b_small, m_norm=m_norm, w_out=w_out[0].astype(BF16), norm2=norm2,
        w_g=w_g[0].astype(BF16), w_u=w_u[0].astype(BF16), conv_w=conv_w[0], conv_b=conv_b,
        w_d=w_d[0].astype(BF16), norm_f=norm_f[None])

    outs_p = _layer(x_prompt, None, wts)
    outs_s = _layer(x_sample, (state_gla[0], state_mlstm_C[0], state_mlstm_n[0], state_mlstm_m[0],
                               cache_ffn_conv[0]), wts)
    y_p, *st_p = outs_p
    y_s, *st_s = outs_s
    return (y_p, y_s) + tuple(s[None] for s in st_p) + tuple(s[None] for s in st_s)
```

```python
import functools

import numpy as np
import jax
import jax.numpy as jnp
from jax import lax
from jax.experimental import pallas as pl
from jax.experimental.pallas import tpu as pltpu

F32 = jnp.float32
BF16 = jnp.bfloat16

EPS = 1e-6
CHUNK = 64
GLA_HEADS = 4
GLA_DK = 64
GLA_DV = 128
GLA_RANK = 16
GLA_TAU = 16.0
M_HEADS = 4
M_DH = 128
FFN_CONV = 3
LANES = 128
SUBLANES = 8
MXU_ROWS = 256

GLA_QK = GLA_HEADS * GLA_DK
GLA_V = GLA_HEADS * GLA_DV
M_W = M_HEADS * M_DH
D_MIX = GLA_V + M_W
OFF_GQ = 0
OFF_GK = OFF_GQ + GLA_QK
OFF_GV = OFF_GK + GLA_QK
OFF_GR = OFF_GV + GLA_V
OFF_MQ = OFF_GR + GLA_V
OFF_MK = OFF_MQ + M_W
OFF_MV = OFF_MK + M_W
OFF_MO = OFF_MV + M_W
OFF_SMALL = OFF_MO + M_W
P_COLS = OFF_SMALL + LANES
SM_I = GLA_RANK
SM_F = GLA_RANK + M_HEADS

INPROJ_SLAB = 512
FFN_SLAB = 256
DOWN_SLAB = 256
VMEM_LIMIT = 60 * 1024 * 1024
N_WEIGHTS = 20


def _dot(a, b):
    return jnp.dot(a, b, preferred_element_type=F32)


def _dot_nt(a, b):
    return lax.dot_general(a, b, (((1,), (1,)), ((), ())), preferred_element_type=F32)


def _dot_tn(a, b):
    return lax.dot_general(a, b, (((0,), (0,)), ((), ())), preferred_element_type=F32)


def _split_hilo(x):
    hi = x.astype(BF16)
    lo = (x - hi.astype(F32)).astype(BF16)
    return hi, lo


def _log_sigmoid(x):
    return jnp.minimum(x, 0.0) - jnp.log1p(jnp.exp(-jnp.abs(x)))


def _rms(x, g):
    return x * lax.rsqrt(jnp.mean(x * x, axis=-1, keepdims=True) + EPS) * g


def _interleave(tasks, strides):
    alive = list(zip(tasks, strides))
    while alive:
        for item in list(alive):
            task, stride = item
            for _ in range(stride):
                try:
                    next(task)
                except StopIteration:
                    alive.remove(item)
                    break


def _mixer_consts(c):
    nlev = int(np.log2(c))
    assert 2 ** nlev == c
    t = np.arange(c)[:, None]
    u = np.arange(c)[None, :]
    blocks = [(u <= t), (u > t)]
    for l in range(1, nlev + 1):
        bsz, half = 2 ** l, 2 ** (l - 1)
        mid = (t // bsz) * bsz + half
        right = (t % bsz) >= half
        blocks.append(np.where(right, (u >= mid) & (u <= t), (u > t) & (u < mid)))
    a = np.concatenate(blocks, axis=0).astype(np.float32)
    a2 = np.concatenate([a, a], axis=1)
    x = np.bitwise_xor(t, u)
    lv = np.where(u > t, -1, np.where(x == 0, 0, np.floor(np.log2(np.maximum(x, 1))) + 1))
    e = np.zeros((LANES, 2 * M_HEADS * LANES), np.float32)
    for h in range(M_HEADS):
        e[SM_I + h, h * LANES:(h + 1) * LANES] = 1.0
        e[SM_F + h, (M_HEADS + h) * LANES:(M_HEADS + h + 1) * LANES] = 1.0
    e2 = np.concatenate([e, e], axis=0)
    lv4 = np.tile(lv, (1, GLA_HEADS))
    bd = (np.arange(GLA_HEADS * c)[:, None] // c == np.arange(GLA_QK)[None, :] // GLA_DK)
    return (jnp.asarray(a2, BF16), jnp.asarray(lv4, jnp.int32), jnp.asarray(e2, BF16),
            jnp.asarray(bd, BF16), nlev)


def _inproj(xf_ref, n1_ref, win_ref, wsm_ref, p_sc, ready, *, c, streams):
    rows = streams * c
    d = xf_ref.shape[-1]
    h_in = _rms(xf_ref[...].reshape(rows, d), n1_ref[...]).astype(BF16)
    for lo in range(0, OFF_SMALL, INPROJ_SLAB):
        hi = lo + INPROJ_SLAB
        p_sc[:, lo:hi] = _dot(h_in, win_ref[:, lo:hi])
        ready.append((lo, hi))
        yield
    p_sc[:, OFF_SMALL:] = _dot(h_in, wsm_ref[...])
    ready.append((OFF_SMALL, P_COLS))
    yield


def _p_reader(p_sc, ready, rs):
    def pcol(g, off, width):
        assert any(lo <= off and off + width <= hi for lo, hi in ready), (off, width)
        return p_sc[rs[g], off:off + width]
    return pcol


def _gla(wa_ref, ba_ref, gn_ref, a2_ref, lv_ref, bd_ref, p_sc, ready, mix_sc, slot, s_sc, *, c, nlev,
         streams):
    gs = range(streams)
    rs = [slice(g * c, (g + 1) * c) for g in gs]
    pcol = _p_reader(p_sc, ready, rs)
    assert (OFF_SMALL, P_COLS) in ready
    small = p_sc[:, OFF_SMALL:OFF_SMALL + LANES]
    xa = _dot(small.astype(BF16), wa_ref[...]) + ba_ref[...]
    la_hi, la_lo = _split_hilo(_log_sigmoid(xa) * (1.0 / GLA_TAU))
    yield
    z = [_dot(a2_ref[...], jnp.concatenate([la_hi[rs[g]], la_lo[rs[g]]], axis=0))
         for g in gs]
    yield
    ez = [jnp.exp(z[g]) for g in gs]
    yield
    q = [pcol(g, OFF_GQ, GLA_QK) * (GLA_DK ** -0.5) for g in gs]
    k = [pcol(g, OFF_GK, GLA_QK) for g in gs]
    q_in = [(q[g] * ez[g][0:c]).astype(BF16) for g in gs]
    k_dec = [(k[g] * ez[g][c:2 * c]).astype(BF16) for g in gs]
    dec = [ez[g][c - 1:c, :] for g in gs]
    q_lv = [[q[g].astype(BF16)] + [(q[g] * ez[g][(2 + l) * c:(3 + l) * c]).astype(BF16)
                                   for l in range(nlev)] for g in gs]
    k_lv = [[k[g].astype(BF16)] + [(k[g] * ez[g][(2 + l) * c:(3 + l) * c]).astype(BF16)
                                   for l in range(nlev)] for g in gs]
    yield
    lv = lv_ref[...]
    bd = bd_ref[...]
    sc = [jnp.zeros((c, GLA_HEADS * c), F32) for g in gs]
    for l in range(nlev + 1):
        for g in gs:
            k_bd = jnp.concatenate([k_lv[g][l]] * GLA_HEADS, axis=0) * bd
            sc[g] = jnp.where(lv == l, _dot_nt(q_lv[g][l], k_bd), sc[g])
        yield
    sc = [sc[g].astype(BF16) for g in gs]
    for h in range(GLA_HEADS):
        sl = slice(h * GLA_DK, (h + 1) * GLA_DK)
        for g in gs:
            vb = pcol(g, OFF_GV + h * GLA_DV, GLA_DV).astype(BF16)
            st = s_sc[g, :, sl]
            o = _dot_nt(q_in[g][:, sl], st.astype(BF16)) + _dot(sc[g][:, h * c:(h + 1) * c], vb)
            s_sc[g, :, sl] = dec[g][:, sl] * st + _dot_tn(vb, k_dec[g][:, sl])
            r = pcol(g, OFF_GR + h * GLA_DV, GLA_DV)
            mix_sc[slot, rs[g], h * GLA_DV:(h + 1) * GLA_DV] = (
                _rms(o, gn_ref[...]) * (r * jax.nn.sigmoid(r)))
        yield


def _mlstm(bsm_ref, mn_ref, a2_ref, e2_ref, p_sc, ready, mix_sc, slot, c_sc, m_sc, *, c, streams):
    rows = streams * c
    gs = range(streams)
    rs = [slice(g * c, (g + 1) * c) for g in gs]
    pcol = _p_reader(p_sc, ready, rs)
    assert (OFF_SMALL, P_COLS) in ready
    lane = lax.broadcasted_iota(jnp.int32, (c, LANES), 1)
    row = lax.broadcasted_iota(jnp.int32, (c, LANES), 0)
    on_diag = row == lane
    causal = lane <= row
    ones = jnp.ones((c, LANES), F32)
    lane_all = lax.broadcasted_iota(jnp.int32, (rows, LANES), 1)
    is_f = (lane_all >= SM_F) & (lane_all < SM_F + M_HEADS)

    x = p_sc[:, OFF_SMALL:OFF_SMALL + LANES] + bsm_ref[...]
    x = jnp.where(is_f, _log_sigmoid(x), x)
    x_hi, x_lo = _split_hilo(x)
    yield
    xcum = jnp.concatenate(
        [_dot(a2_ref[0:c, :], jnp.concatenate([x_hi[rs[g]], x_lo[rs[g]]], axis=0)) for g in gs],
        axis=0)
    xc = jnp.where(is_f, xcum, x)
    yield
    xe_all = _dot(jnp.concatenate(_split_hilo(xc), axis=1), e2_ref[...])
    xe = [xe_all[rs[g]] for g in gs]
    yield
    for h in range(M_HEADS):
        ie = [xe[g][:, h * LANES:(h + 1) * LANES] for g in gs]
        fe = [xe[g][:, (M_HEADS + h) * LANES:(M_HEADS + h + 1) * LANES] for g in gs]
        a = [ie[g] - fe[g] for g in gs]
        a_row = [jnp.sum(jnp.where(on_diag, a[g], 0.0), axis=0, keepdims=True) for g in gs]
        dm = [jnp.where(causal, fe[g] + a_row[g], -jnp.inf)[:, :c] for g in gs]
        gi = [fe[g] + m_sc[g, h] for g in gs]
        m_t = [jnp.maximum(gi[g], jnp.max(dm[g], axis=-1, keepdims=True)) for g in gs]
        w_int = [jnp.exp(gi[g] - m_t[g]) for g in gs]
        w = [jnp.exp(dm[g] - m_t[g][:, :c]) for g in gs]
        yield
        qh = [pcol(g, OFF_MQ + h * M_DH, M_DH).astype(BF16) for g in gs]
        kf = [pcol(g, OFF_MK + h * M_DH, M_DH) * (M_DH ** -0.5) for g in gs]
        vaug = [jnp.concatenate([pcol(g, OFF_MV + h * M_DH, M_DH), ones], axis=1).astype(BF16)
                for g in gs]
        qk = [(_dot_nt(qh[g], kf[g].astype(BF16)) * w[g]).astype(BF16) for g in gs]
        yield
        caug = [c_sc[g, h] for g in gs]
        nd = [_dot(qk[g], vaug[g])
              + jnp.concatenate([w_int[g], w_int[g]], axis=1) * _dot(qh[g], caug[g].astype(BF16))
              for g in gs]
        for g in gs:
            den = jnp.maximum(jnp.abs(nd[g][:, M_DH:]), jnp.exp(-m_t[g]))
            og = jax.nn.sigmoid(pcol(g, OFF_MO + h * M_DH, M_DH)) * (nd[g][:, :M_DH] / den)
            mix_sc[slot, rs[g], GLA_V + h * M_DH:GLA_V + (h + 1) * M_DH] = _rms(og, mn_ref[...])
        yield
        for g in gs:
            m_c = m_t[g][c - 1:c, :]
            decay = jnp.exp(gi[g][c - 1:c, :] - m_c)
            ws = jnp.exp(a[g] + (fe[g][c - 1:c, :] - m_c))
            kw = (kf[g] * ws).astype(BF16)
            c_sc[g, h] = jnp.concatenate([decay, decay], axis=1) * caug[g] + _dot_tn(kw, vaug[g])
            m_sc[g, h] = m_c
        yield


def _back(xb_ref, mix_sc, slot_prev, wo_ref, n2_ref, wg_ref, wu_ref, cw_ref, cb_ref, wd_ref, nf_ref,
          y_ref, carry_sc, *, c, streams):
    rows = streams * c
    d = xb_ref.shape[-1]
    d_ff = wg_ref.shape[1]
    nslab = d_ff // FFN_SLAB
    x1 = xb_ref[...].reshape(rows, d) + _dot(mix_sc[slot_prev].astype(BF16), wo_ref[...])
    h2 = _rms(x1, n2_ref[...]).astype(BF16)
    yield

    def gate_up(s):
        cs = slice(s * FFN_SLAB, (s + 1) * FFN_SLAB)
        return _dot(h2, wg_ref[:, cs]), _dot(h2, wu_ref[:, cs])

    row8 = lax.broadcasted_iota(jnp.int32, (SUBLANES, FFN_SLAB), 0)
    nxt = gate_up(0)
    acts = []
    yield
    for s in range(nslab):
        cs = slice(s * FFN_SLAB, (s + 1) * FFN_SLAB)
        g, u = nxt
        if s + 1 < nslab:
            nxt = gate_up(s + 1)
        yield
        r1 = pltpu.roll(g, 1, axis=0)
        r2 = pltpu.roll(g, 2, axis=0)
        g1, g2 = [], []
        for st in range(streams):
            b0 = st * c
            p0 = carry_sc[st, 0:1, cs]
            p1 = carry_sc[st, 1:2, cs]
            g1 += [jnp.where(row8 == 0, p1, r1[b0:b0 + SUBLANES]), r1[b0 + SUBLANES:b0 + c]]
            g2 += [jnp.where(row8 == 0, p0, jnp.where(row8 == 1, p1, r2[b0:b0 + SUBLANES])),
                   r2[b0 + SUBLANES:b0 + c]]
            carry_sc[st, 0:1, cs] = g[b0 + c - 2:b0 + c - 1, :]
            carry_sc[st, 1:2, cs] = g[b0 + c - 1:b0 + c, :]
        g1 = jnp.concatenate(g1, axis=0)
        g2 = jnp.concatenate(g2, axis=0)
        gc = cb_ref[:, cs] + (cw_ref[0:1, cs] * g2 + cw_ref[1:2, cs] * g1 + cw_ref[2:3, cs] * g)
        yield
        acts.append((jax.nn.gelu(gc) * u).astype(BF16))
        yield
    act = jnp.concatenate(acts, axis=1)
    down = []
    for lo in range(0, d, DOWN_SLAB):
        down.append(_dot(act, wd_ref[:, lo:lo + DOWN_SLAB]))
        yield
    y_ref[...] = _rms(x1 + jnp.concatenate(down, axis=1), nf_ref[...]).reshape(streams, c, d)


def _layer_kernel(*refs, c, nlev, streams, nj, nsteps, skew, zero_state):
    n_in = 2 + N_WEIGHTS + (0 if zero_state else 5)
    (xf_ref, xb_ref, n1_ref, win_ref, wsm_ref, wa_ref, ba_ref, gn_ref, bsm_ref, mn_ref, a2_ref, lv_ref,
     e2_ref, bd_ref, wo_ref, n2_ref, wg_ref, wu_ref, cw_ref, cb_ref, wd_ref, nf_ref) = refs[:2 + N_WEIGHTS]
    s0_ref, c0_ref, n0_ref, m0_ref, cst_ref = (None,) * 5 if zero_state else refs[2 + N_WEIGHTS:n_in]
    y_ref, s_out, c_out, n_out, m_out, cv_out = refs[n_in:n_in + 6]
    p_sc, mix_sc, s_sc, c_sc, m_sc, carry_sc = refs[n_in + 6:]

    t = pl.program_id(0)
    tf = jnp.minimum(t, nsteps - 1)
    jf = lax.rem(tf, nj)
    tb = jnp.maximum(t - skew, 0)
    jb = lax.rem(tb, nj)
    slot = lax.rem(t, 2)
    slot_back = 1 - slot if skew else slot

    if skew:
        @pl.when(t == 0)
        def _():
            mix_sc[1] = jnp.zeros(mix_sc.shape[1:], F32)

    @pl.when(jf == 0)
    def _():
        if zero_state:
            s_sc[...] = jnp.zeros(s_sc.shape, F32)
            c_sc[...] = jnp.zeros(c_sc.shape, F32)
            m_sc[...] = jnp.zeros(m_sc.shape, F32)
        else:
            for g in range(streams):
                for lo in range(0, GLA_QK, LANES):
                    s_sc[g, :, lo:lo + LANES] = s0_ref[g, lo:lo + LANES, :].T
            c_sc[:, :, :, :M_DH] = c0_ref[...]
            c_sc[:, :, :, M_DH:] = jnp.broadcast_to(n0_ref[...], c0_ref.shape)
            m_sc[...] = m0_ref[...]

    @pl.when(jb == 0)
    def _():
        carry_sc[...] = jnp.zeros(carry_sc.shape, F32) if zero_state else cst_ref[...]

    back = _back(xb_ref, mix_sc, slot_back, wo_ref, n2_ref, wg_ref, wu_ref, cw_ref, cb_ref, wd_ref,
                 nf_ref, y_ref, carry_sc, c=c, streams=streams)
    if skew:
        next(back)
    ready = []
    for _ in _inproj(xf_ref, n1_ref, win_ref, wsm_ref, p_sc, ready, c=c, streams=streams):
        pass
    mixers = [_gla(wa_ref, ba_ref, gn_ref, a2_ref, lv_ref, bd_ref, p_sc, ready, mix_sc, slot, s_sc,
                   c=c, nlev=nlev, streams=streams),
              _mlstm(bsm_ref, mn_ref, a2_ref, e2_ref, p_sc, ready, mix_sc, slot, c_sc, m_sc,
                     c=c, streams=streams)]
    if skew:
        _interleave(mixers + [back], strides=(1, 1, 3))
    else:
        _interleave(mixers, strides=(1, 1))
        _interleave([back], strides=(1,))

    @pl.when((jf == nj - 1) & (t < nsteps))
    def _():
        for g in range(streams):
            for lo in range(0, GLA_QK, LANES):
                s_out[g, lo:lo + LANES, :] = s_sc[g, :, lo:lo + LANES].T
        c_out[...] = c_sc[:, :, :, :M_DH]
        on_diag = (lax.broadcasted_iota(jnp.int32, (M_DH, M_DH), 0)
                   == lax.broadcasted_iota(jnp.int32, (M_DH, M_DH), 1))
        for g in range(streams):
            for h in range(M_HEADS):
                n_out[g, h] = jnp.sum(jnp.where(on_diag, c_sc[g, h, :, M_DH:], 0.0), axis=0, keepdims=True)
        m_out[...] = m_sc[...]

    @pl.when((jb == nj - 1) & (t >= skew))
    def _():
        cv_out[...] = carry_sc[...]


def _layer(x, states, wts):
    batch, length, d = x.shape
    c = min(CHUNK, length)
    nj = length // c
    streams = min(batch, MXU_ROWS // c)
    assert batch % streams == 0 and length % c == 0
    nsteps = (batch // streams) * nj
    d_ff = wts["w_g"].shape[1]
    a2, lv, e2, bd, nlev = _mixer_consts(c)
    skew = 1 if nsteps > 1 else 0

    def front_idx(t):
        tf = jnp.minimum(t, nsteps - 1)
        return tf // nj, tf % nj

    def back_idx(t):
        tb = jnp.maximum(t - skew, 0)
        return tb // nj, tb % nj

    once = pl.Buffered(1)
    const = lambda a: pl.BlockSpec(a.shape, lambda t: (0,) * a.ndim, pipeline_mode=once)
    fstate = lambda *blk, **kw: pl.BlockSpec(
        (streams,) + blk, lambda t: (front_idx(t)[0],) + (0,) * len(blk), **kw)
    bstate = lambda *blk, **kw: pl.BlockSpec(
        (streams,) + blk, lambda t: (back_idx(t)[0],) + (0,) * len(blk), **kw)
    consts = [wts["norm1"], wts["w_in_r"], wts["w_in_sm"], wts["wa_pad"], wts["b_gla_a"], wts["gla_norm"],
              wts["b_small"], wts["m_norm"], a2, lv, e2, bd, wts["w_out"], wts["norm2"], wts["w_g"],
              wts["w_u"], wts["conv_w"], wts["conv_b"], wts["w_d"], wts["norm_f"]]
    assert len(consts) == N_WEIGHTS
    state_ops, state_specs = [], []
    if states is not None:
        s_gla, c_m, n_m, m_m, conv_st = states
        state_ops = [
            s_gla.reshape(batch, GLA_QK, GLA_DV),
            c_m, n_m[..., None],
            jnp.broadcast_to(m_m[..., None, None], (batch, M_HEADS, 1, LANES)), conv_st]
        state_specs = [
            fstate(GLA_QK, GLA_DV, pipeline_mode=once),
            fstate(M_HEADS, M_DH, M_DH, pipeline_mode=once),
            fstate(M_HEADS, M_DH, 1, pipeline_mode=once),
            fstate(M_HEADS, 1, LANES, pipeline_mode=once),
            bstate(FFN_CONV - 1, d_ff, pipeline_mode=once)]
    outs = pl.pallas_call(
        functools.partial(_layer_kernel, c=c, nlev=nlev, streams=streams, nj=nj, nsteps=nsteps,
                          skew=skew, zero_state=states is None),
        grid=(nsteps + skew,),
        in_specs=[
            pl.BlockSpec((streams, c, d), lambda t: front_idx(t) + (0,)),
            pl.BlockSpec((streams, c, d), lambda t: back_idx(t) + (0,)),
        ] + [const(a) for a in consts] + state_specs,
        out_specs=[
            pl.BlockSpec((streams, c, d), lambda t: back_idx(t) + (0,)),
            fstate(GLA_QK, GLA_DV), fstate(M_HEADS, M_DH, M_DH), fstate(M_HEADS, 1, M_DH),
            fstate(M_HEADS, 1, LANES), bstate(FFN_CONV - 1, d_ff),
        ],
        out_shape=[
            jax.ShapeDtypeStruct((batch, length, d), F32),
            jax.ShapeDtypeStruct((batch, GLA_QK, GLA_DV), F32),
            jax.ShapeDtypeStruct((batch, M_HEADS, M_DH, M_DH), F32),
            jax.ShapeDtypeStruct((batch, M_HEADS, 1, M_DH), F32),
            jax.ShapeDtypeStruct((batch, M_HEADS, 1, LANES), F32),
            jax.ShapeDtypeStruct((batch, FFN_CONV - 1, d_ff), F32),
        ],
        scratch_shapes=[
            pltpu.VMEM((streams * c, P_COLS), F32),
            pltpu.VMEM((2, streams * c, D_MIX), F32),
            pltpu.VMEM((streams, GLA_DV, GLA_QK), F32),
            pltpu.VMEM((streams, M_HEADS, M_DH, 2 * M_DH), F32),
            pltpu.VMEM((streams, M_HEADS, 1, LANES), F32),
            pltpu.VMEM((streams, FFN_CONV - 1, d_ff), F32),
        ],
        compiler_params=pltpu.CompilerParams(
            dimension_semantics=("arbitrary",), vmem_limit_bytes=VMEM_LIMIT),
        name="layer",
    )(x, x, *consts, *state_ops)
    y, s_new, c_new, n_new, m_e, conv_new = outs
    return (y, s_new.reshape(batch, GLA_HEADS, GLA_DK, GLA_DV), c_new,
            n_new.reshape(batch, M_HEADS, M_DH), m_e[:, :, 0, 0], conv_new)


def _regroup_kernel(wg_ref, wm_ref, glr_ref, mif_ref, o_ref, osm_ref, *, n_gla):
    j = pl.program_id(0)

    @pl.when(j < n_gla)
    def _():
        o_ref[...] = wg_ref[...].T.astype(BF16)

    @pl.when(j >= n_gla)
    def _():
        o_ref[...] = wm_ref[...].T.astype(BF16)

    @pl.when(j == 0)
    def _():
        d = wg_ref.shape[1]
        small = jnp.concatenate(
            [glr_ref[...], mif_ref[...], jnp.zeros((LANES - GLA_RANK - 2 * M_HEADS, d), F32)], axis=0)
        osm_ref[...] = small.T.astype(BF16)


def _regroup_w_in(w):
    d = w.shape[0]
    wt = w.T
    o_glr = OFF_MQ
    o_m = o_glr + GLA_RANK
    o_if = o_m + 4 * M_W
    nslab = OFF_SMALL // INPROJ_SLAB
    n_gla = OFF_MQ // INPROJ_SLAB
    rows = lambda n: (pl.Element(n), pl.Element(d))
    return pl.pallas_call(
        functools.partial(_regroup_kernel, n_gla=n_gla),
        grid=(nslab,),
        in_specs=[
            pl.BlockSpec(rows(INPROJ_SLAB), lambda j: (jnp.minimum(j, n_gla - 1) * INPROJ_SLAB, 0)),
            pl.BlockSpec(rows(INPROJ_SLAB), lambda j: (
                (o_m // GLA_RANK + jnp.maximum(j - n_gla, 0) * (INPROJ_SLAB // GLA_RANK)) * GLA_RANK, 0)),
            pl.BlockSpec(rows(GLA_RANK), lambda j: (o_glr, 0)),
            pl.BlockSpec(rows(2 * M_HEADS), lambda j: (o_if, 0)),
        ],
        out_specs=[
            pl.BlockSpec((d, INPROJ_SLAB), lambda j: (0, j)),
            pl.BlockSpec((d, LANES), lambda j: (0, 0)),
        ],
        out_shape=[
            jax.ShapeDtypeStruct((d, OFF_SMALL), BF16),
            jax.ShapeDtypeStruct((d, LANES), BF16),
        ],
        compiler_params=pltpu.CompilerParams(dimension_semantics=("arbitrary",)),
        name="regroup_w_in",
    )(wt, wt, wt, wt)


def kernel(x_prompt, x_sample, state_gla, state_mlstm_C, state_mlstm_n, state_mlstm_m, cache_ffn_conv,
           norm1, w_in, w_gla_a, b_gla_a, gla_norm, b_m_i, b_m_f, m_norm, w_out, norm2, w_g, w_u,
           conv_w, conv_b, w_d, norm_f):
    assert w_in.shape[0] == 1, "single-layer stack"
    w_in_r, w_in_sm = _regroup_w_in(w_in[0])
    wa_pad = jnp.concatenate(
        [w_gla_a[0], jnp.zeros((LANES - GLA_RANK, GLA_QK), w_gla_a.dtype)], axis=0).astype(BF16)
    b_small = jnp.concatenate(
        [jnp.zeros((GLA_RANK,), F32), b_m_i[0], b_m_f[0],
         jnp.zeros((LANES - GLA_RANK - 2 * M_HEADS,), F32)])[None]
    wts = dict(
        norm1=norm1, w_in_r=w_in_r, w_in_sm=w_in_sm, wa_pad=wa_pad, b_gla_a=b_gla_a, gla_norm=gla_norm,
        b_small=b_small, m_norm=m_norm, w_out=w_out[0].astype(BF16), norm2=norm2,
        w_g=w_g[0].astype(BF16), w_u=w_u[0].astype(BF16), conv_w=conv_w[0], conv_b=conv_b,
        w_d=w_d[0].astype(BF16), norm_f=norm_f[None])

    outs_p = _layer(x_prompt, None, wts)
    outs_s = _layer(x_sample, (state_gla[0], state_mlstm_C[0], state_mlstm_n[0], state_mlstm_m[0],
                               cache_ffn_conv[0]), wts)
    y_p, *st_p = outs_p
    y_s, *st_s = outs_s
    return (y_p, y_s) + tuple(s[None] for s in st_p) + tuple(s[None] for s in st_s)
```

```python
import functools

import numpy as np
import jax
import jax.numpy as jnp
from jax import lax
from jax.experimental import pallas as pl
from jax.experimental.pallas import tpu as pltpu

F32 = jnp.float32
BF16 = jnp.bfloat16

EPS = 1e-6
CHUNK = 64
GLA_HEADS = 4
GLA_DK = 64
GLA_DV = 128
GLA_RANK = 16
GLA_TAU = 16.0
M_HEADS = 4
M_DH = 128
FFN_CONV = 3
LANES = 128
SUBLANES = 8
MXU_ROWS = 256

GLA_QK = GLA_HEADS * GLA_DK
GLA_V = GLA_HEADS * GLA_DV
M_W = M_HEADS * M_DH
D_MIX = GLA_V + M_W
OFF_GQ = 0
OFF_GK = OFF_GQ + GLA_QK
OFF_GV = OFF_GK + GLA_QK
OFF_GR = OFF_GV + GLA_V
OFF_MQ = OFF_GR + GLA_V
OFF_MK = OFF_MQ + M_W
OFF_MV = OFF_MK + M_W
OFF_MO = OFF_MV + M_W
OFF_SMALL = OFF_MO + M_W
P_COLS = OFF_SMALL + LANES
SM_I = GLA_RANK
SM_F = GLA_RANK + M_HEADS

INPROJ_SLAB = 512
FFN_SLAB = 256
DOWN_SLAB = 256
VMEM_LIMIT = 60 * 1024 * 1024
N_WEIGHTS = 20
CAST_STEPS = 8


def _dot(a, b):
    return jnp.dot(a, b, preferred_element_type=F32)


def _dot_nt(a, b):
    return lax.dot_general(a, b, (((1,), (1,)), ((), ())), preferred_element_type=F32)


def _dot_tn(a, b):
    return lax.dot_general(a, b, (((0,), (0,)), ((), ())), preferred_element_type=F32)


def _split_hilo(x):
    hi = x.astype(BF16)
    lo = (x - hi.astype(F32)).astype(BF16)
    return hi, lo


def _log_sigmoid(x):
    return jnp.minimum(x, 0.0) - jnp.log1p(jnp.exp(-jnp.abs(x)))


def _rms(x, g):
    return x * lax.rsqrt(jnp.mean(x * x, axis=-1, keepdims=True) + EPS) * g


def _interleave(tasks, strides):
    alive = list(zip(tasks, strides))
    while alive:
        for item in list(alive):
            task, stride = item
            for _ in range(stride):
                try:
                    next(task)
                except StopIteration:
                    alive.remove(item)
                    break


def _mixer_consts(c):
    nlev = int(np.log2(c))
    assert 2 ** nlev == c
    t = np.arange(c)[:, None]
    u = np.arange(c)[None, :]
    blocks = [(u <= t), (u > t)]
    for l in range(1, nlev + 1):
        bsz, half = 2 ** l, 2 ** (l - 1)
        mid = (t // bsz) * bsz + half
        right = (t % bsz) >= half
        blocks.append(np.where(right, (u >= mid) & (u <= t), (u > t) & (u < mid)))
    a = np.concatenate(blocks, axis=0).astype(np.float32)
    a2 = np.concatenate([a, a], axis=1)
    x = np.bitwise_xor(t, u)
    lv = np.where(u > t, -1, np.where(x == 0, 0, np.floor(np.log2(np.maximum(x, 1))) + 1))
    e = np.zeros((LANES, 2 * M_HEADS * LANES), np.float32)
    for h in range(M_HEADS):
        e[SM_I + h, h * LANES:(h + 1) * LANES] = 1.0
        e[SM_F + h, (M_HEADS + h) * LANES:(M_HEADS + h + 1) * LANES] = 1.0
    e2 = np.concatenate([e, e], axis=0)
    lv4 = np.tile(lv, (1, GLA_HEADS))
    bd = (np.arange(GLA_HEADS * c)[:, None] // c == np.arange(GLA_QK)[None, :] // GLA_DK)
    return (jnp.asarray(a2, BF16), jnp.asarray(lv4, jnp.int32), jnp.asarray(e2, BF16),
            jnp.asarray(bd, BF16), nlev)


def _inproj(xf_ref, n1_ref, win_ref, wsm_ref, p_sc, ready, *, c, streams):
    rows = streams * c
    d = xf_ref.shape[-1]
    h_in = _rms(xf_ref[...].reshape(rows, d), n1_ref[...]).astype(BF16)
    for lo in range(0, OFF_SMALL, INPROJ_SLAB):
        hi = lo + INPROJ_SLAB
        p_sc[:, lo:hi] = _dot(h_in, win_ref[:, lo:hi])
        ready.append((lo, hi))
        yield
    p_sc[:, OFF_SMALL:] = _dot(h_in, wsm_ref[...])
    ready.append((OFF_SMALL, P_COLS))
    yield


def _p_reader(p_sc, ready, rs):
    def pcol(g, off, width):
        assert any(lo <= off and off + width <= hi for lo, hi in ready), (off, width)
        return p_sc[rs[g], off:off + width]
    return pcol


def _gla(wa_ref, ba_ref, gn_ref, a2_ref, lv_ref, bd_ref, p_sc, ready, mix_sc, slot, s_sc, *, c, nlev,
         streams):
    gs = range(streams)
    rs = [slice(g * c, (g + 1) * c) for g in gs]
    pcol = _p_reader(p_sc, ready, rs)
    assert (OFF_SMALL, P_COLS) in ready
    small = p_sc[:, OFF_SMALL:OFF_SMALL + LANES]
    xa = _dot(small.astype(BF16), wa_ref[...]) + ba_ref[...]
    la_hi, la_lo = _split_hilo(_log_sigmoid(xa) * (1.0 / GLA_TAU))
    yield
    z = [_dot(a2_ref[...], jnp.concatenate([la_hi[rs[g]], la_lo[rs[g]]], axis=0))
         for g in gs]
    yield
    ez = [jnp.exp(z[g]) for g in gs]
    yield
    q = [pcol(g, OFF_GQ, GLA_QK) * (GLA_DK ** -0.5) for g in gs]
    k = [pcol(g, OFF_GK, GLA_QK) for g in gs]
    q_in = [(q[g] * ez[g][0:c]).astype(BF16) for g in gs]
    k_dec = [(k[g] * ez[g][c:2 * c]).astype(BF16) for g in gs]
    dec = [ez[g][c - 1:c, :] for g in gs]
    q_lv = [[q[g].astype(BF16)] + [(q[g] * ez[g][(2 + l) * c:(3 + l) * c]).astype(BF16)
                                   for l in range(nlev)] for g in gs]
    k_lv = [[k[g].astype(BF16)] + [(k[g] * ez[g][(2 + l) * c:(3 + l) * c]).astype(BF16)
                                   for l in range(nlev)] for g in gs]
    yield
    lv = lv_ref[...]
    bd = bd_ref[...]
    sc = [jnp.zeros((c, GLA_HEADS * c), F32) for g in gs]
    for l in range(nlev + 1):
        for g in gs:
            k_bd = jnp.concatenate([k_lv[g][l]] * GLA_HEADS, axis=0) * bd
            sc[g] = jnp.where(lv == l, _dot_nt(q_lv[g][l], k_bd), sc[g])
        yield
    sc = [sc[g].astype(BF16) for g in gs]
    for h in range(GLA_HEADS):
        sl = slice(h * GLA_DK, (h + 1) * GLA_DK)
        for g in gs:
            vb = pcol(g, OFF_GV + h * GLA_DV, GLA_DV).astype(BF16)
            st = s_sc[g, :, sl]
            o = _dot_nt(q_in[g][:, sl], st.astype(BF16)) + _dot(sc[g][:, h * c:(h + 1) * c], vb)
            s_sc[g, :, sl] = dec[g][:, sl] * st + _dot_tn(vb, k_dec[g][:, sl])
            r = pcol(g, OFF_GR + h * GLA_DV, GLA_DV)
            mix_sc[slot, rs[g], h * GLA_DV:(h + 1) * GLA_DV] = (
                _rms(o, gn_ref[...]) * (r * jax.nn.sigmoid(r)))
        yield


def _mlstm(bsm_ref, mn_ref, a2_ref, e2_ref, p_sc, ready, mix_sc, slot, c_sc, m_sc, *, c, streams):
    rows = streams * c
    gs = range(streams)
    rs = [slice(g * c, (g + 1) * c) for g in gs]
    pcol = _p_reader(p_sc, ready, rs)
    assert (OFF_SMALL, P_COLS) in ready
    lane = lax.broadcasted_iota(jnp.int32, (c, LANES), 1)
    row = lax.broadcasted_iota(jnp.int32, (c, LANES), 0)
    on_diag = row == lane
    causal = lane <= row
    ones = jnp.ones((c, LANES), F32)
    lane_all = lax.broadcasted_iota(jnp.int32, (rows, LANES), 1)
    is_f = (lane_all >= SM_F) & (lane_all < SM_F + M_HEADS)

    x = p_sc[:, OFF_SMALL:OFF_SMALL + LANES] + bsm_ref[...]
    x = jnp.where(is_f, _log_sigmoid(x), x)
    x_hi, x_lo = _split_hilo(x)
    yield
    xcum = jnp.concatenate(
        [_dot(a2_ref[0:c, :], jnp.concatenate([x_hi[rs[g]], x_lo[rs[g]]], axis=0)) for g in gs],
        axis=0)
    xc = jnp.where(is_f, xcum, x)
    yield
    xe_all = _dot(jnp.concatenate(_split_hilo(xc), axis=1), e2_ref[...])
    xe = [xe_all[rs[g]] for g in gs]
    yield
    for h in range(M_HEADS):
        ie = [xe[g][:, h * LANES:(h + 1) * LANES] for g in gs]
        fe = [xe[g][:, (M_HEADS + h) * LANES:(M_HEADS + h + 1) * LANES] for g in gs]
        a = [ie[g] - fe[g] for g in gs]
        a_row = [jnp.sum(jnp.where(on_diag, a[g], 0.0), axis=0, keepdims=True) for g in gs]
        dm = [jnp.where(causal, fe[g] + a_row[g], -jnp.inf)[:, :c] for g in gs]
        gi = [fe[g] + m_sc[g, h] for g in gs]
        m_t = [jnp.maximum(gi[g], jnp.max(dm[g], axis=-1, keepdims=True)) for g in gs]
        w_int = [jnp.exp(gi[g] - m_t[g]) for g in gs]
        w = [jnp.exp(dm[g] - m_t[g][:, :c]) for g in gs]
        yield
        qh = [pcol(g, OFF_MQ + h * M_DH, M_DH).astype(BF16) for g in gs]
        kf = [pcol(g, OFF_MK + h * M_DH, M_DH) * (M_DH ** -0.5) for g in gs]
        vaug = [jnp.concatenate([pcol(g, OFF_MV + h * M_DH, M_DH), ones], axis=1).astype(BF16)
                for g in gs]
        qk = [(_dot_nt(qh[g], kf[g].astype(BF16)) * w[g]).astype(BF16) for g in gs]
        yield
        caug = [c_sc[g, h] for g in gs]
        nd = [_dot(qk[g], vaug[g])
              + jnp.concatenate([w_int[g], w_int[g]], axis=1) * _dot(qh[g], caug[g].astype(BF16))
              for g in gs]
        for g in gs:
            den = jnp.maximum(jnp.abs(nd[g][:, M_DH:]), jnp.exp(-m_t[g]))
            og = jax.nn.sigmoid(pcol(g, OFF_MO + h * M_DH, M_DH)) * (nd[g][:, :M_DH] / den)
            mix_sc[slot, rs[g], GLA_V + h * M_DH:GLA_V + (h + 1) * M_DH] = _rms(og, mn_ref[...])
        yield
        for g in gs:
            m_c = m_t[g][c - 1:c, :]
            decay = jnp.exp(gi[g][c - 1:c, :] - m_c)
            ws = jnp.exp(a[g] + (fe[g][c - 1:c, :] - m_c))
            kw = (kf[g] * ws).astype(BF16)
            c_sc[g, h] = jnp.concatenate([decay, decay], axis=1) * caug[g] + _dot_tn(kw, vaug[g])
            m_sc[g, h] = m_c
        yield


def _back(xb_ref, mix_sc, slot_prev, wo_ref, n2_ref, wg_ref, wu_ref, cw_ref, cb_ref, wd_ref, nf_ref,
          y_ref, carry_sc, *, c, streams):
    rows = streams * c
    d = xb_ref.shape[-1]
    d_ff = wg_ref.shape[1]
    nslab = d_ff // FFN_SLAB
    x1 = xb_ref[...].reshape(rows, d) + _dot(mix_sc[slot_prev].astype(BF16), wo_ref[...])
    h2 = _rms(x1, n2_ref[...]).astype(BF16)
    yield

    def gate_up(s):
        cs = slice(s * FFN_SLAB, (s + 1) * FFN_SLAB)
        return _dot(h2, wg_ref[:, cs]), _dot(h2, wu_ref[:, cs])

    row8 = lax.broadcasted_iota(jnp.int32, (SUBLANES, FFN_SLAB), 0)
    nxt = gate_up(0)
    acts = []
    yield
    for s in range(nslab):
        cs = slice(s * FFN_SLAB, (s + 1) * FFN_SLAB)
        g, u = nxt
        if s + 1 < nslab:
            nxt = gate_up(s + 1)
        yield
        r1 = pltpu.roll(g, 1, axis=0)
        r2 = pltpu.roll(g, 2, axis=0)
        g1, g2 = [], []
        for st in range(streams):
            b0 = st * c
            p0 = carry_sc[st, 0:1, cs]
            p1 = carry_sc[st, 1:2, cs]
            g1 += [jnp.where(row8 == 0, p1, r1[b0:b0 + SUBLANES]), r1[b0 + SUBLANES:b0 + c]]
            g2 += [jnp.where(row8 == 0, p0, jnp.where(row8 == 1, p1, r2[b0:b0 + SUBLANES])),
                   r2[b0 + SUBLANES:b0 + c]]
            carry_sc[st, 0:1, cs] = g[b0 + c - 2:b0 + c - 1, :]
            carry_sc[st, 1:2, cs] = g[b0 + c - 1:b0 + c, :]
        g1 = jnp.concatenate(g1, axis=0)
        g2 = jnp.concatenate(g2, axis=0)
        gc = cb_ref[:, cs] + (cw_ref[0:1, cs] * g2 + cw_ref[1:2, cs] * g1 + cw_ref[2:3, cs] * g)
        yield
        acts.append((jax.nn.gelu(gc) * u).astype(BF16))
        yield
    act = jnp.concatenate(acts, axis=1)
    down = []
    for lo in range(0, d, DOWN_SLAB):
        down.append(_dot(act, wd_ref[:, lo:lo + DOWN_SLAB]))
        yield
    y_ref[...] = _rms(x1 + jnp.concatenate(down, axis=1), nf_ref[...]).reshape(streams, c, d)


def _layer_kernel(*refs, c, nlev, streams, nj, nsteps, skew, zero_state):
    n_in = 2 + N_WEIGHTS + (0 if zero_state else 5)
    (xf_ref, xb_ref, n1_ref, win_ref, wsm_ref, wa_ref, ba_ref, gn_ref, bsm_ref, mn_ref, a2_ref, lv_ref,
     e2_ref, bd_ref, wo_ref, n2_ref, wg_ref, wu_ref, cw_ref, cb_ref, wd_ref, nf_ref) = refs[:2 + N_WEIGHTS]
    s0_ref, c0_ref, n0_ref, m0_ref, cst_ref = (None,) * 5 if zero_state else refs[2 + N_WEIGHTS:n_in]
    y_ref, s_out, c_out, n_out, m_out, cv_out = refs[n_in:n_in + 6]
    p_sc, mix_sc, s_sc, c_sc, m_sc, carry_sc = refs[n_in + 6:]

    t = pl.program_id(0)
    tf = jnp.minimum(t, nsteps - 1)
    jf = lax.rem(tf, nj)
    tb = jnp.maximum(t - skew, 0)
    jb = lax.rem(tb, nj)
    slot = lax.rem(t, 2)
    slot_back = 1 - slot if skew else slot

    if skew:
        @pl.when(t == 0)
        def _():
            mix_sc[1] = jnp.zeros(mix_sc.shape[1:], F32)

    @pl.when(jf == 0)
    def _():
        if zero_state:
            s_sc[...] = jnp.zeros(s_sc.shape, F32)
            c_sc[...] = jnp.zeros(c_sc.shape, F32)
            m_sc[...] = jnp.zeros(m_sc.shape, F32)
        else:
            for g in range(streams):
                for lo in range(0, GLA_QK, LANES):
                    s_sc[g, :, lo:lo + LANES] = s0_ref[g, lo:lo + LANES, :].T
            c_sc[:, :, :, :M_DH] = c0_ref[...]
            c_sc[:, :, :, M_DH:] = jnp.broadcast_to(n0_ref[...], c0_ref.shape)
            m_sc[...] = m0_ref[...]

    @pl.when(jb == 0)
    def _():
        carry_sc[...] = jnp.zeros(carry_sc.shape, F32) if zero_state else cst_ref[...]

    back = _back(xb_ref, mix_sc, slot_back, wo_ref, n2_ref, wg_ref, wu_ref, cw_ref, cb_ref, wd_ref,
                 nf_ref, y_ref, carry_sc, c=c, streams=streams)
    if skew:
        next(back)
    ready = []
    for _ in _inproj(xf_ref, n1_ref, win_ref, wsm_ref, p_sc, ready, c=c, streams=streams):
        pass
    mixers = [_gla(wa_ref, ba_ref, gn_ref, a2_ref, lv_ref, bd_ref, p_sc, ready, mix_sc, slot, s_sc,
                   c=c, nlev=nlev, streams=streams),
              _mlstm(bsm_ref, mn_ref, a2_ref, e2_ref, p_sc, ready, mix_sc, slot, c_sc, m_sc,
                     c=c, streams=streams)]
    if skew:
        _interleave(mixers + [back], strides=(1, 1, 3))
    else:
        _interleave(mixers, strides=(1, 1))
        _interleave([back], strides=(1,))

    @pl.when((jf == nj - 1) & (t < nsteps))
    def _():
        for g in range(streams):
            for lo in range(0, GLA_QK, LANES):
                s_out[g, lo:lo + LANES, :] = s_sc[g, :, lo:lo + LANES].T
        c_out[...] = c_sc[:, :, :, :M_DH]
        on_diag = (lax.broadcasted_iota(jnp.int32, (M_DH, M_DH), 0)
                   == lax.broadcasted_iota(jnp.int32, (M_DH, M_DH), 1))
        for g in range(streams):
            for h in range(M_HEADS):
                n_out[g, h] = jnp.sum(jnp.where(on_diag, c_sc[g, h, :, M_DH:], 0.0), axis=0, keepdims=True)
        m_out[...] = m_sc[...]

    @pl.when((jb == nj - 1) & (t >= skew))
    def _():
        cv_out[...] = carry_sc[...]


def _layer(x, states, wts):
    batch, length, d = x.shape
    c = min(CHUNK, length)
    nj = length // c
    streams = min(batch, MXU_ROWS // c)
    assert batch % streams == 0 and length % c == 0
    nsteps = (batch // streams) * nj
    d_ff = wts["w_g"].shape[1]
    a2, lv, e2, bd, nlev = _mixer_consts(c)
    skew = 1 if nsteps > 1 else 0

    def front_idx(t):
        tf = jnp.minimum(t, nsteps - 1)
        return tf // nj, tf % nj

    def back_idx(t):
        tb = jnp.maximum(t - skew, 0)
        return tb // nj, tb % nj

    once = pl.Buffered(1)
    const = lambda a: pl.BlockSpec(a.shape, lambda t: (0,) * a.ndim, pipeline_mode=once)
    fstate = lambda *blk, **kw: pl.BlockSpec(
        (streams,) + blk, lambda t: (front_idx(t)[0],) + (0,) * len(blk), **kw)
    bstate = lambda *blk, **kw: pl.BlockSpec(
        (streams,) + blk, lambda t: (back_idx(t)[0],) + (0,) * len(blk), **kw)
    consts = [wts["norm1"], wts["w_in_r"], wts["w_in_sm"], wts["wa_pad"], wts["b_gla_a"], wts["gla_norm"],
              wts["b_small"], wts["m_norm"], a2, lv, e2, bd, wts["w_out"], wts["norm2"], wts["w_g"],
              wts["w_u"], wts["conv_w"], wts["conv_b"], wts["w_d"], wts["norm_f"]]
    assert len(consts) == N_WEIGHTS
    state_ops, state_specs = [], []
    if states is not None:
        s_gla, c_m, n_m, m_m, conv_st = states
        state_ops = [
            s_gla.reshape(batch, GLA_QK, GLA_DV),
            c_m, n_m[..., None],
            jnp.broadcast_to(m_m[..., None, None], (batch, M_HEADS, 1, LANES)), conv_st]
        state_specs = [
            fstate(GLA_QK, GLA_DV, pipeline_mode=once),
            fstate(M_HEADS, M_DH, M_DH, pipeline_mode=once),
            fstate(M_HEADS, M_DH, 1, pipeline_mode=once),
            fstate(M_HEADS, 1, LANES, pipeline_mode=once),
            bstate(FFN_CONV - 1, d_ff, pipeline_mode=once)]
    outs = pl.pallas_call(
        functools.partial(_layer_kernel, c=c, nlev=nlev, streams=streams, nj=nj, nsteps=nsteps,
                          skew=skew, zero_state=states is None),
        grid=(nsteps + skew,),
        in_specs=[
            pl.BlockSpec((streams, c, d), lambda t: front_idx(t) + (0,)),
            pl.BlockSpec((streams, c, d), lambda t: back_idx(t) + (0,)),
        ] + [const(a) for a in consts] + state_specs,
        out_specs=[
            pl.BlockSpec((streams, c, d), lambda t: back_idx(t) + (0,)),
            fstate(GLA_QK, GLA_DV), fstate(M_HEADS, M_DH, M_DH), fstate(M_HEADS, 1, M_DH),
            fstate(M_HEADS, 1, LANES), bstate(FFN_CONV - 1, d_ff),
        ],
        out_shape=[
            jax.ShapeDtypeStruct((batch, length, d), F32),
            jax.ShapeDtypeStruct((batch, GLA_QK, GLA_DV), F32),
            jax.ShapeDtypeStruct((batch, M_HEADS, M_DH, M_DH), F32),
            jax.ShapeDtypeStruct((batch, M_HEADS, 1, M_DH), F32),
            jax.ShapeDtypeStruct((batch, M_HEADS, 1, LANES), F32),
            jax.ShapeDtypeStruct((batch, FFN_CONV - 1, d_ff), F32),
        ],
        scratch_shapes=[
            pltpu.VMEM((streams * c, P_COLS), F32),
            pltpu.VMEM((2, streams * c, D_MIX), F32),
            pltpu.VMEM((streams, GLA_DV, GLA_QK), F32),
            pltpu.VMEM((streams, M_HEADS, M_DH, 2 * M_DH), F32),
            pltpu.VMEM((streams, M_HEADS, 1, LANES), F32),
            pltpu.VMEM((streams, FFN_CONV - 1, d_ff), F32),
        ],
        compiler_params=pltpu.CompilerParams(
            dimension_semantics=("arbitrary",), vmem_limit_bytes=VMEM_LIMIT),
        name="layer",
    )(x, x, *consts, *state_ops)
    y, s_new, c_new, n_new, m_e, conv_new = outs
    return (y, s_new.reshape(batch, GLA_HEADS, GLA_DK, GLA_DV), c_new,
            n_new.reshape(batch, M_HEADS, M_DH), m_e[:, :, 0, 0], conv_new)


def _regroup_kernel(wg_ref, wm_ref, glr_ref, mif_ref, o_ref, osm_ref, *, n_gla):
    j = pl.program_id(0)

    @pl.when(j < n_gla)
    def _():
        o_ref[...] = wg_ref[...].T.astype(BF16)

    @pl.when(j >= n_gla)
    def _():
        o_ref[...] = wm_ref[...].T.astype(BF16)

    @pl.when(j == 0)
    def _():
        d = wg_ref.shape[1]
        small = jnp.concatenate(
            [glr_ref[...], mif_ref[...], jnp.zeros((LANES - GLA_RANK - 2 * M_HEADS, d), F32)], axis=0)
        osm_ref[...] = small.T.astype(BF16)


def _regroup_w_in(w):
    d = w.shape[0]
    wt = w.T
    o_glr = OFF_MQ
    o_m = o_glr + GLA_RANK
    o_if = o_m + 4 * M_W
    nslab = OFF_SMALL // INPROJ_SLAB
    n_gla = OFF_MQ // INPROJ_SLAB
    rows = lambda n: (pl.Element(n), pl.Element(d))
    return pl.pallas_call(
        functools.partial(_regroup_kernel, n_gla=n_gla),
        grid=(nslab,),
        in_specs=[
            pl.BlockSpec(rows(INPROJ_SLAB), lambda j: (jnp.minimum(j, n_gla - 1) * INPROJ_SLAB, 0)),
            pl.BlockSpec(rows(INPROJ_SLAB), lambda j: (
                (o_m // GLA_RANK + jnp.maximum(j - n_gla, 0) * (INPROJ_SLAB // GLA_RANK)) * GLA_RANK, 0)),
            pl.BlockSpec(rows(GLA_RANK), lambda j: (o_glr, 0)),
            pl.BlockSpec(rows(2 * M_HEADS), lambda j: (o_if, 0)),
        ],
        out_specs=[
            pl.BlockSpec((d, INPROJ_SLAB), lambda j: (0, j)),
            pl.BlockSpec((d, LANES), lambda j: (0, 0)),
        ],
        out_shape=[
            jax.ShapeDtypeStruct((d, OFF_SMALL), BF16),
            jax.ShapeDtypeStruct((d, LANES), BF16),
        ],
        compiler_params=pltpu.CompilerParams(dimension_semantics=("arbitrary",)),
        name="regroup_w_in",
    )(wt, wt, wt, wt)


def _cast_kernel(*refs):
    n = len(refs) // 2
    for src, dst in zip(refs[:n], refs[n:]):
        dst[...] = src[...].astype(BF16)


def _cast_bf16(*ws):
    steps = CAST_STEPS
    blocks = [(w.shape[0] // steps, w.shape[1]) for w in ws]
    assert all(w.shape[0] % (steps * 2 * SUBLANES) == 0 for w in ws)
    return pl.pallas_call(
        _cast_kernel,
        grid=(steps,),
        in_specs=[pl.BlockSpec(b, lambda i: (i, 0)) for b in blocks],
        out_specs=[pl.BlockSpec(b, lambda i: (i, 0)) for b in blocks],
        out_shape=[jax.ShapeDtypeStruct(w.shape, BF16) for w in ws],
        compiler_params=pltpu.CompilerParams(dimension_semantics=("arbitrary",)),
        name="cast_weights",
    )(*ws)


def kernel(x_prompt, x_sample, state_gla, state_mlstm_C, state_mlstm_n, state_mlstm_m, cache_ffn_conv,
           norm1, w_in, w_gla_a, b_gla_a, gla_norm, b_m_i, b_m_f, m_norm, w_out, norm2, w_g, w_u,
           conv_w, conv_b, w_d, norm_f):
    assert w_in.shape[0] == 1, "single-layer stack"
    w_in_r, w_in_sm = _regroup_w_in(w_in[0])
    wa_pad = jnp.concatenate(
        [w_gla_a[0], jnp.zeros((LANES - GLA_RANK, GLA_QK), w_gla_a.dtype)], axis=0).astype(BF16)
    b_small = jnp.concatenate(
        [jnp.zeros((GLA_RANK,), F32), b_m_i[0], b_m_f[0],
         jnp.zeros((LANES - GLA_RANK - 2 * M_HEADS,), F32)])[None]
    w_out_b, w_g_b, w_u_b, w_d_b = _cast_bf16(w_out[0], w_g[0], w_u[0], w_d[0])
    wts = dict(
        norm1=norm1, w_in_r=w_in_r, w_in_sm=w_in_sm, wa_pad=wa_pad, b_gla_a=b_gla_a, gla_norm=gla_norm,
        b_small=b_small, m_norm=m_norm, w_out=w_out_b, norm2=norm2, w_g=w_g_b, w_u=w_u_b,
        conv_w=conv_w[0], conv_b=conv_b, w_d=w_d_b, norm_f=norm_f[None])

    outs_p = _layer(x_prompt, None, wts)
    outs_s = _layer(x_sample, (state_gla[0], state_mlstm_C[0], state_mlstm_n[0], state_mlstm_m[0],
                               cache_ffn_conv[0]), wts)
    y_p, *st_p = outs_p
    y_s, *st_s = outs_s
    return (y_p, y_s) + tuple(s[None] for s in st_p) + tuple(s[None] for s in st_s)
```

```python
import functools

import numpy as np
import jax
import jax.numpy as jnp
from jax import lax
from jax.experimental import pallas as pl
from jax.experimental.pallas import tpu as pltpu

F32 = jnp.float32
BF16 = jnp.bfloat16

EPS = 1e-6
CHUNK = 64
GLA_HEADS = 4
GLA_DK = 64
GLA_DV = 128
GLA_RANK = 16
GLA_TAU = 16.0
M_HEADS = 4
M_DH = 128
FFN_CONV = 3
LANES = 128
SUBLANES = 8
MXU_ROWS = 256

GLA_QK = GLA_HEADS * GLA_DK
GLA_V = GLA_HEADS * GLA_DV
M_W = M_HEADS * M_DH
D_MIX = GLA_V + M_W
OFF_GQ = 0
OFF_GK = OFF_GQ + GLA_QK
OFF_GV = OFF_GK + GLA_QK
OFF_GR = OFF_GV + GLA_V
OFF_MQ = OFF_GR + GLA_V
OFF_MK = OFF_MQ + M_W
OFF_MV = OFF_MK + M_W
OFF_MO = OFF_MV + M_W
OFF_SMALL = OFF_MO + M_W
P_COLS = OFF_SMALL + LANES
SM_I = GLA_RANK
SM_F = GLA_RANK + M_HEADS

INPROJ_SLAB = 512
FFN_SLAB = 256
DOWN_SLAB = 256
VMEM_LIMIT = 60 * 1024 * 1024
N_WEIGHTS = 20


def _dot(a, b):
    return jnp.dot(a, b, preferred_element_type=F32)


def _dot_nt(a, b):
    return lax.dot_general(a, b, (((1,), (1,)), ((), ())), preferred_element_type=F32)


def _dot_tn(a, b):
    return lax.dot_general(a, b, (((0,), (0,)), ((), ())), preferred_element_type=F32)


def _split_hilo(x):
    hi = x.astype(BF16)
    lo = (x - hi.astype(F32)).astype(BF16)
    return hi, lo


def _log_sigmoid(x):
    return jnp.minimum(x, 0.0) - jnp.log1p(jnp.exp(-jnp.abs(x)))


def _rms(x, g):
    return x * lax.rsqrt(jnp.mean(x * x, axis=-1, keepdims=True) + EPS) * g


def _interleave(tasks, strides):
    alive = list(zip(tasks, strides))
    while alive:
        for item in list(alive):
            task, stride = item
            for _ in range(stride):
                try:
                    next(task)
                except StopIteration:
                    alive.remove(item)
                    break


def _mixer_consts(c):
    nlev = int(np.log2(c))
    assert 2 ** nlev == c
    t = np.arange(c)[:, None]
    u = np.arange(c)[None, :]
    blocks = [(u <= t), (u > t)]
    for l in range(1, nlev + 1):
        bsz, half = 2 ** l, 2 ** (l - 1)
        mid = (t // bsz) * bsz + half
        right = (t % bsz) >= half
        blocks.append(np.where(right, (u >= mid) & (u <= t), (u > t) & (u < mid)))
    a = np.concatenate(blocks, axis=0).astype(np.float32)
    a2 = np.concatenate([a, a], axis=1)
    x = np.bitwise_xor(t, u)
    lv = np.where(u > t, -1, np.where(x == 0, 0, np.floor(np.log2(np.maximum(x, 1))) + 1))
    e = np.zeros((LANES, 2 * M_HEADS * LANES), np.float32)
    for h in range(M_HEADS):
        e[SM_I + h, h * LANES:(h + 1) * LANES] = 1.0
        e[SM_F + h, (M_HEADS + h) * LANES:(M_HEADS + h + 1) * LANES] = 1.0
    e2 = np.concatenate([e, e], axis=0)
    lv4 = np.tile(lv, (1, GLA_HEADS))
    bd = (np.arange(GLA_HEADS * c)[:, None] // c == np.arange(GLA_QK)[None, :] // GLA_DK)
    return (jnp.asarray(a2, BF16), jnp.asarray(lv4, jnp.int32), jnp.asarray(e2, BF16),
            jnp.asarray(bd, BF16), nlev)


def _inproj(xf_ref, n1_ref, win_ref, wsm_ref, p_sc, ready, *, c, streams):
    rows = streams * c
    d = xf_ref.shape[-1]
    h_in = _rms(xf_ref[...].reshape(rows, d), n1_ref[...]).astype(BF16)
    for lo in range(0, OFF_SMALL, INPROJ_SLAB):
        hi = lo + INPROJ_SLAB
        p_sc[:, lo:hi] = _dot(h_in, win_ref[:, lo:hi])
        ready.append((lo, hi))
        yield
    p_sc[:, OFF_SMALL:] = _dot(h_in, wsm_ref[...])
    ready.append((OFF_SMALL, P_COLS))
    yield


def _p_reader(p_sc, ready, rs):
    def pcol(g, off, width):
        assert any(lo <= off and off + width <= hi for lo, hi in ready), (off, width)
        return p_sc[rs[g], off:off + width]
    return pcol


def _gla(wa_ref, ba_ref, gn_ref, a2_ref, lv_ref, bd_ref, p_sc, ready, mix_sc, slot, s_sc, *, c, nlev,
         streams):
    gs = range(streams)
    rs = [slice(g * c, (g + 1) * c) for g in gs]
    pcol = _p_reader(p_sc, ready, rs)
    assert (OFF_SMALL, P_COLS) in ready
    small = p_sc[:, OFF_SMALL:OFF_SMALL + LANES]
    xa = _dot(small.astype(BF16), wa_ref[...]) + ba_ref[...]
    la_hi, la_lo = _split_hilo(_log_sigmoid(xa) * (1.0 / GLA_TAU))
    yield
    z = [_dot(a2_ref[...], jnp.concatenate([la_hi[rs[g]], la_lo[rs[g]]], axis=0))
         for g in gs]
    yield
    ez = [jnp.exp(z[g]) for g in gs]
    yield
    q = [pcol(g, OFF_GQ, GLA_QK) * (GLA_DK ** -0.5) for g in gs]
    k = [pcol(g, OFF_GK, GLA_QK) for g in gs]
    q_in = [(q[g] * ez[g][0:c]).astype(BF16) for g in gs]
    k_dec = [(k[g] * ez[g][c:2 * c]).astype(BF16) for g in gs]
    dec = [ez[g][c - 1:c, :] for g in gs]
    q_lv = [[q[g].astype(BF16)] + [(q[g] * ez[g][(2 + l) * c:(3 + l) * c]).astype(BF16)
                                   for l in range(nlev)] for g in gs]
    k_lv = [[k[g].astype(BF16)] + [(k[g] * ez[g][(2 + l) * c:(3 + l) * c]).astype(BF16)
                                   for l in range(nlev)] for g in gs]
    yield
    lv = lv_ref[...]
    bd = bd_ref[...]
    sc = [jnp.zeros((c, GLA_HEADS * c), F32) for g in gs]
    for l in range(nlev + 1):
        for g in gs:
            k_bd = jnp.concatenate([k_lv[g][l]] * GLA_HEADS, axis=0) * bd
            sc[g] = jnp.where(lv == l, _dot_nt(q_lv[g][l], k_bd), sc[g])
        yield
    sc = [sc[g].astype(BF16) for g in gs]
    for h in range(GLA_HEADS):
        sl = slice(h * GLA_DK, (h + 1) * GLA_DK)
        for g in gs:
            vb = pcol(g, OFF_GV + h * GLA_DV, GLA_DV).astype(BF16)
            st = s_sc[g, :, sl]
            o = _dot_nt(q_in[g][:, sl], st.astype(BF16)) + _dot(sc[g][:, h * c:(h + 1) * c], vb)
            s_sc[g, :, sl] = dec[g][:, sl] * st + _dot_tn(vb, k_dec[g][:, sl])
            r = pcol(g, OFF_GR + h * GLA_DV, GLA_DV)
            mix_sc[slot, rs[g], h * GLA_DV:(h + 1) * GLA_DV] = (
                _rms(o, gn_ref[...]) * (r * jax.nn.sigmoid(r))).astype(BF16)
        yield


def _mlstm(bsm_ref, mn_ref, a2_ref, e2_ref, p_sc, ready, mix_sc, slot, c_sc, m_sc, *, c, streams):
    rows = streams * c
    gs = range(streams)
    rs = [slice(g * c, (g + 1) * c) for g in gs]
    pcol = _p_reader(p_sc, ready, rs)
    assert (OFF_SMALL, P_COLS) in ready
    lane = lax.broadcasted_iota(jnp.int32, (c, LANES), 1)
    row = lax.broadcasted_iota(jnp.int32, (c, LANES), 0)
    on_diag = row == lane
    causal = lane <= row
    ones = jnp.ones((c, LANES), F32)
    lane_all = lax.broadcasted_iota(jnp.int32, (rows, LANES), 1)
    is_f = (lane_all >= SM_F) & (lane_all < SM_F + M_HEADS)

    x = p_sc[:, OFF_SMALL:OFF_SMALL + LANES] + bsm_ref[...]
    x = jnp.where(is_f, _log_sigmoid(x), x)
    x_hi, x_lo = _split_hilo(x)
    yield
    xcum = jnp.concatenate(
        [_dot(a2_ref[0:c, :], jnp.concatenate([x_hi[rs[g]], x_lo[rs[g]]], axis=0)) for g in gs],
        axis=0)
    xc = jnp.where(is_f, xcum, x)
    yield
    xe_all = _dot(jnp.concatenate(_split_hilo(xc), axis=1), e2_ref[...])
    xe = [xe_all[rs[g]] for g in gs]
    yield
    for h in range(M_HEADS):
        ie = [xe[g][:, h * LANES:(h + 1) * LANES] for g in gs]
        fe = [xe[g][:, (M_HEADS + h) * LANES:(M_HEADS + h + 1) * LANES] for g in gs]
        a = [ie[g] - fe[g] for g in gs]
        a_row = [jnp.sum(jnp.where(on_diag, a[g], 0.0), axis=0, keepdims=True) for g in gs]
        dm = [jnp.where(causal, fe[g] + a_row[g], -jnp.inf)[:, :c] for g in gs]
        gi = [fe[g] + m_sc[g, h] for g in gs]
        m_t = [jnp.maximum(gi[g], jnp.max(dm[g], axis=-1, keepdims=True)) for g in gs]
        w_int = [jnp.exp(gi[g] - m_t[g]) for g in gs]
        w = [jnp.exp(dm[g] - m_t[g][:, :c]) for g in gs]
        yield
        qh = [pcol(g, OFF_MQ + h * M_DH, M_DH).astype(BF16) for g in gs]
        kf = [pcol(g, OFF_MK + h * M_DH, M_DH) * (M_DH ** -0.5) for g in gs]
        vaug = [jnp.concatenate([pcol(g, OFF_MV + h * M_DH, M_DH), ones], axis=1).astype(BF16)
                for g in gs]
        qk = [(_dot_nt(qh[g], kf[g].astype(BF16)) * w[g]).astype(BF16) for g in gs]
        yield
        caug = [c_sc[g, h] for g in gs]
        nd = [_dot(qk[g], vaug[g])
              + jnp.concatenate([w_int[g], w_int[g]], axis=1) * _dot(qh[g], caug[g].astype(BF16))
              for g in gs]
        for g in gs:
            den = jnp.maximum(jnp.abs(nd[g][:, M_DH:]), jnp.exp(-m_t[g]))
            og = jax.nn.sigmoid(pcol(g, OFF_MO + h * M_DH, M_DH)) * (nd[g][:, :M_DH] / den)
            mix_sc[slot, rs[g], GLA_V + h * M_DH:GLA_V + (h + 1) * M_DH] = (
                _rms(og, mn_ref[...]).astype(BF16))
        yield
        for g in gs:
            m_c = m_t[g][c - 1:c, :]
            decay = jnp.exp(gi[g][c - 1:c, :] - m_c)
            ws = jnp.exp(a[g] + (fe[g][c - 1:c, :] - m_c))
            kw = (kf[g] * ws).astype(BF16)
            c_sc[g, h] = jnp.concatenate([decay, decay], axis=1) * caug[g] + _dot_tn(kw, vaug[g])
            m_sc[g, h] = m_c
        yield


def _back(xb_ref, mix_sc, slot_prev, wo_ref, n2_ref, wg_ref, wu_ref, cw_ref, cb_ref, wd_ref, nf_ref,
          y_ref, carry_sc, *, c, streams):
    rows = streams * c
    d = xb_ref.shape[-1]
    d_ff = wg_ref.shape[1]
    nslab = d_ff // FFN_SLAB
    x1 = xb_ref[...].reshape(rows, d) + _dot(mix_sc[slot_prev], wo_ref[...])
    h2 = _rms(x1, n2_ref[...]).astype(BF16)
    yield

    def gate_up(s):
        cs = slice(s * FFN_SLAB, (s + 1) * FFN_SLAB)
        return _dot(h2, wg_ref[:, cs]), _dot(h2, wu_ref[:, cs])

    row8 = lax.broadcasted_iota(jnp.int32, (SUBLANES, FFN_SLAB), 0)
    nxt = gate_up(0)
    acts = []
    yield
    for s in range(nslab):
        cs = slice(s * FFN_SLAB, (s + 1) * FFN_SLAB)
        g, u = nxt
        if s + 1 < nslab:
            nxt = gate_up(s + 1)
        yield
        r1 = pltpu.roll(g, 1, axis=0)
        r2 = pltpu.roll(g, 2, axis=0)
        g1, g2 = [], []
        for st in range(streams):
            b0 = st * c
            p0 = carry_sc[st, 0:1, cs]
            p1 = carry_sc[st, 1:2, cs]
            g1 += [jnp.where(row8 == 0, p1, r1[b0:b0 + SUBLANES]), r1[b0 + SUBLANES:b0 + c]]
            g2 += [jnp.where(row8 == 0, p0, jnp.where(row8 == 1, p1, r2[b0:b0 + SUBLANES])),
                   r2[b0 + SUBLANES:b0 + c]]
            carry_sc[st, 0:1, cs] = g[b0 + c - 2:b0 + c - 1, :]
            carry_sc[st, 1:2, cs] = g[b0 + c - 1:b0 + c, :]
        g1 = jnp.concatenate(g1, axis=0)
        g2 = jnp.concatenate(g2, axis=0)
        gc = cb_ref[:, cs] + (cw_ref[0:1, cs] * g2 + cw_ref[1:2, cs] * g1 + cw_ref[2:3, cs] * g)
        yield
        acts.append((jax.nn.gelu(gc) * u).astype(BF16))
        yield
    act = jnp.concatenate(acts, axis=1)
    down = []
    for lo in range(0, d, DOWN_SLAB):
        down.append(_dot(act, wd_ref[:, lo:lo + DOWN_SLAB]))
        yield
    y_ref[...] = _rms(x1 + jnp.concatenate(down, axis=1), nf_ref[...]).reshape(streams, c, d)


def _layer_kernel(*refs, c, nlev, streams, nj, nsteps, skew, zero_state):
    n_in = 2 + N_WEIGHTS + (0 if zero_state else 5)
    (xf_ref, xb_ref, n1_ref, win_ref, wsm_ref, wa_ref, ba_ref, gn_ref, bsm_ref, mn_ref, a2_ref, lv_ref,
     e2_ref, bd_ref, wo_ref, n2_ref, wg_ref, wu_ref, cw_ref, cb_ref, wd_ref, nf_ref) = refs[:2 + N_WEIGHTS]
    s0_ref, c0_ref, n0_ref, m0_ref, cst_ref = (None,) * 5 if zero_state else refs[2 + N_WEIGHTS:n_in]
    y_ref, s_out, c_out, n_out, m_out, cv_out = refs[n_in:n_in + 6]
    p_sc, mix_sc, s_sc, c_sc, m_sc, carry_sc = refs[n_in + 6:]

    t = pl.program_id(0)
    tf = jnp.minimum(t, nsteps - 1)
    jf = lax.rem(tf, nj)
    tb = jnp.maximum(t - skew, 0)
    jb = lax.rem(tb, nj)
    slot = lax.rem(t, 2)
    slot_back = 1 - slot if skew else slot

    if skew:
        @pl.when(t == 0)
        def _():
            mix_sc[1] = jnp.zeros(mix_sc.shape[1:], BF16)

    @pl.when(jf == 0)
    def _():
        if zero_state:
            s_sc[...] = jnp.zeros(s_sc.shape, F32)
            c_sc[...] = jnp.zeros(c_sc.shape, F32)
            m_sc[...] = jnp.zeros(m_sc.shape, F32)
        else:
            for g in range(streams):
                for lo in range(0, GLA_QK, LANES):
                    s_sc[g, :, lo:lo + LANES] = s0_ref[g, lo:lo + LANES, :].T
            c_sc[:, :, :, :M_DH] = c0_ref[...]
            c_sc[:, :, :, M_DH:] = jnp.broadcast_to(n0_ref[...], c0_ref.shape)
            m_sc[...] = m0_ref[...]

    @pl.when(jb == 0)
    def _():
        carry_sc[...] = jnp.zeros(carry_sc.shape, F32) if zero_state else cst_ref[...]

    back = _back(xb_ref, mix_sc, slot_back, wo_ref, n2_ref, wg_ref, wu_ref, cw_ref, cb_ref, wd_ref,
                 nf_ref, y_ref, carry_sc, c=c, streams=streams)
    if skew:
        next(back)
    ready = []
    for _ in _inproj(xf_ref, n1_ref, win_ref, wsm_ref, p_sc, ready, c=c, streams=streams):
        pass
    mixers = [_gla(wa_ref, ba_ref, gn_ref, a2_ref, lv_ref, bd_ref, p_sc, ready, mix_sc, slot, s_sc,
                   c=c, nlev=nlev, streams=streams),
              _mlstm(bsm_ref, mn_ref, a2_ref, e2_ref, p_sc, ready, mix_sc, slot, c_sc, m_sc,
                     c=c, streams=streams)]
    if skew:
        _interleave(mixers + [back], strides=(1, 1, 3))
    else:
        _interleave(mixers, strides=(1, 1))
        _interleave([back], strides=(1,))

    @pl.when((jf == nj - 1) & (t < nsteps))
    def _():
        for g in range(streams):
            for lo in range(0, GLA_QK, LANES):
                s_out[g, lo:lo + LANES, :] = s_sc[g, :, lo:lo + LANES].T
        c_out[...] = c_sc[:, :, :, :M_DH]
        on_diag = (lax.broadcasted_iota(jnp.int32, (M_DH, M_DH), 0)
                   == lax.broadcasted_iota(jnp.int32, (M_DH, M_DH), 1))
        for g in range(streams):
            for h in range(M_HEADS):
                n_out[g, h] = jnp.sum(jnp.where(on_diag, c_sc[g, h, :, M_DH:], 0.0), axis=0, keepdims=True)
        m_out[...] = m_sc[...]

    @pl.when((jb == nj - 1) & (t >= skew))
    def _():
        cv_out[...] = carry_sc[...]


def _layer(x, states, wts):
    batch, length, d = x.shape
    c = min(CHUNK, length)
    nj = length // c
    streams = min(batch, MXU_ROWS // c)
    assert batch % streams == 0 and length % c == 0
    nsteps = (batch // streams) * nj
    d_ff = wts["w_g"].shape[1]
    a2, lv, e2, bd, nlev = _mixer_consts(c)
    skew = 1 if nsteps > 1 else 0

    def front_idx(t):
        tf = jnp.minimum(t, nsteps - 1)
        return tf // nj, tf % nj

    def back_idx(t):
        tb = jnp.maximum(t - skew, 0)
        return tb // nj, tb % nj

    once = pl.Buffered(1)
    const = lambda a: pl.BlockSpec(a.shape, lambda t: (0,) * a.ndim, pipeline_mode=once)
    fstate = lambda *blk, **kw: pl.BlockSpec(
        (streams,) + blk, lambda t: (front_idx(t)[0],) + (0,) * len(blk), **kw)
    bstate = lambda *blk, **kw: pl.BlockSpec(
        (streams,) + blk, lambda t: (back_idx(t)[0],) + (0,) * len(blk), **kw)
    consts = [wts["norm1"], wts["w_in_r"], wts["w_in_sm"], wts["wa_pad"], wts["b_gla_a"], wts["gla_norm"],
              wts["b_small"], wts["m_norm"], a2, lv, e2, bd, wts["w_out"], wts["norm2"], wts["w_g"],
              wts["w_u"], wts["conv_w"], wts["conv_b"], wts["w_d"], wts["norm_f"]]
    assert len(consts) == N_WEIGHTS
    state_ops, state_specs = [], []
    if states is not None:
        s_gla, c_m, n_m, m_m, conv_st = states
        state_ops = [
            s_gla.reshape(batch, GLA_QK, GLA_DV),
            c_m, n_m[..., None],
            jnp.broadcast_to(m_m[..., None, None], (batch, M_HEADS, 1, LANES)), conv_st]
        state_specs = [
            fstate(GLA_QK, GLA_DV, pipeline_mode=once),
            fstate(M_HEADS, M_DH, M_DH, pipeline_mode=once),
            fstate(M_HEADS, M_DH, 1, pipeline_mode=once),
            fstate(M_HEADS, 1, LANES, pipeline_mode=once),
            bstate(FFN_CONV - 1, d_ff, pipeline_mode=once)]
    outs = pl.pallas_call(
        functools.partial(_layer_kernel, c=c, nlev=nlev, streams=streams, nj=nj, nsteps=nsteps,
                          skew=skew, zero_state=states is None),
        grid=(nsteps + skew,),
        in_specs=[
            pl.BlockSpec((streams, c, d), lambda t: front_idx(t) + (0,)),
            pl.BlockSpec((streams, c, d), lambda t: back_idx(t) + (0,)),
        ] + [const(a) for a in consts] + state_specs,
        out_specs=[
            pl.BlockSpec((streams, c, d), lambda t: back_idx(t) + (0,)),
            fstate(GLA_QK, GLA_DV), fstate(M_HEADS, M_DH, M_DH), fstate(M_HEADS, 1, M_DH),
            fstate(M_HEADS, 1, LANES), bstate(FFN_CONV - 1, d_ff),
        ],
        out_shape=[
            jax.ShapeDtypeStruct((batch, length, d), F32),
            jax.ShapeDtypeStruct((batch, GLA_QK, GLA_DV), F32),
            jax.ShapeDtypeStruct((batch, M_HEADS, M_DH, M_DH), F32),
            jax.ShapeDtypeStruct((batch, M_HEADS, 1, M_DH), F32),
            jax.ShapeDtypeStruct((batch, M_HEADS, 1, LANES), F32),
            jax.ShapeDtypeStruct((batch, FFN_CONV - 1, d_ff), F32),
        ],
        scratch_shapes=[
            pltpu.VMEM((streams * c, P_COLS), F32),
            pltpu.VMEM((2, streams * c, D_MIX), BF16),
            pltpu.VMEM((streams, GLA_DV, GLA_QK), F32),
            pltpu.VMEM((streams, M_HEADS, M_DH, 2 * M_DH), F32),
            pltpu.VMEM((streams, M_HEADS, 1, LANES), F32),
            pltpu.VMEM((streams, FFN_CONV - 1, d_ff), F32),
        ],
        compiler_params=pltpu.CompilerParams(
            dimension_semantics=("arbitrary",), vmem_limit_bytes=VMEM_LIMIT),
        name="layer",
    )(x, x, *consts, *state_ops)
    y, s_new, c_new, n_new, m_e, conv_new = outs
    return (y, s_new.reshape(batch, GLA_HEADS, GLA_DK, GLA_DV), c_new,
            n_new.reshape(batch, M_HEADS, M_DH), m_e[:, :, 0, 0], conv_new)


def _regroup_kernel(wg_ref, wm_ref, glr_ref, mif_ref, o_ref, osm_ref, *, n_gla):
    j = pl.program_id(0)

    @pl.when(j < n_gla)
    def _():
        o_ref[...] = wg_ref[...].T.astype(BF16)

    @pl.when(j >= n_gla)
    def _():
        o_ref[...] = wm_ref[...].T.astype(BF16)

    @pl.when(j == 0)
    def _():
        d = wg_ref.shape[1]
        small = jnp.concatenate(
            [glr_ref[...], mif_ref[...], jnp.zeros((LANES - GLA_RANK - 2 * M_HEADS, d), F32)], axis=0)
        osm_ref[...] = small.T.astype(BF16)


def _regroup_w_in(w):
    d = w.shape[0]
    wt = w.T
    o_glr = OFF_MQ
    o_m = o_glr + GLA_RANK
    o_if = o_m + 4 * M_W
    nslab = OFF_SMALL // INPROJ_SLAB
    n_gla = OFF_MQ // INPROJ_SLAB
    rows = lambda n: (pl.Element(n), pl.Element(d))
    return pl.pallas_call(
        functools.partial(_regroup_kernel, n_gla=n_gla),
        grid=(nslab,),
        in_specs=[
            pl.BlockSpec(rows(INPROJ_SLAB), lambda j: (jnp.minimum(j, n_gla - 1) * INPROJ_SLAB, 0)),
            pl.BlockSpec(rows(INPROJ_SLAB), lambda j: (
                (o_m // GLA_RANK + jnp.maximum(j - n_gla, 0) * (INPROJ_SLAB // GLA_RANK)) * GLA_RANK, 0)),
            pl.BlockSpec(rows(GLA_RANK), lambda j: (o_glr, 0)),
            pl.BlockSpec(rows(2 * M_HEADS), lambda j: (o_if, 0)),
        ],
        out_specs=[
            pl.BlockSpec((d, INPROJ_SLAB), lambda j: (0, j)),
            pl.BlockSpec((d, LANES), lambda j: (0, 0)),
        ],
        out_shape=[
            jax.ShapeDtypeStruct((d, OFF_SMALL), BF16),
            jax.ShapeDtypeStruct((d, LANES), BF16),
        ],
        compiler_params=pltpu.CompilerParams(dimension_semantics=("arbitrary",)),
        name="regroup_w_in",
    )(wt, wt, wt, wt)


def kernel(x_prompt, x_sample, state_gla, state_mlstm_C, state_mlstm_n, state_mlstm_m, cache_ffn_conv,
           norm1, w_in, w_gla_a, b_gla_a, gla_norm, b_m_i, b_m_f, m_norm, w_out, norm2, w_g, w_u,
           conv_w, conv_b, w_d, norm_f):
    assert w_in.shape[0] == 1, "single-layer stack"
    w_in_r, w_in_sm = _regroup_w_in(w_in[0])
    wa_pad = jnp.concatenate(
        [w_gla_a[0], jnp.zeros((LANES - GLA_RANK, GLA_QK), w_gla_a.dtype)], axis=0).astype(BF16)
    b_small = jnp.concatenate(
        [jnp.zeros((GLA_RANK,), F32), b_m_i[0], b_m_f[0],
         jnp.zeros((LANES - GLA_RANK - 2 * M_HEADS,), F32)])[None]
    wts = dict(
        norm1=norm1, w_in_r=w_in_r, w_in_sm=w_in_sm, wa_pad=wa_pad, b_gla_a=b_gla_a, gla_norm=gla_norm,
        b_small=b_small, m_norm=m_norm, w_out=w_out[0].astype(BF16), norm2=norm2,
        w_g=w_g[0].astype(BF16), w_u=w_u[0].astype(BF16), conv_w=conv_w[0], conv_b=conv_b,
        w_d=w_d[0].astype(BF16), norm_f=norm_f[None])

    outs_p = _layer(x_prompt, None, wts)
    outs_s = _layer(x_sample, (state_gla[0], state_mlstm_C[0], state_mlstm_n[0], state_mlstm_m[0],
                               cache_ffn_conv[0]), wts)
    y_p, *st_p = outs_p
    y_s, *st_s = outs_s
    return (y_p, y_s) + tuple(s[None] for s in st_p) + tuple(s[None] for s in st_s)
```

```python
import functools

import numpy as np
import jax
import jax.numpy as jnp
from jax import lax
from jax.experimental import pallas as pl
from jax.experimental.pallas import tpu as pltpu

F32 = jnp.float32
BF16 = jnp.bfloat16

EPS = 1e-6
CHUNK = 64
GLA_HEADS = 4
GLA_DK = 64
GLA_DV = 128
GLA_RANK = 16
GLA_TAU = 16.0
M_HEADS = 4
M_DH = 128
FFN_CONV = 3
LANES = 128
SUBLANES = 8
MXU_ROWS = 256

GLA_QK = GLA_HEADS * GLA_DK
GLA_V = GLA_HEADS * GLA_DV
M_W = M_HEADS * M_DH
D_MIX = GLA_V + M_W
OFF_GQ = 0
OFF_GK = OFF_GQ + GLA_QK
OFF_GV = OFF_GK + GLA_QK
OFF_GR = OFF_GV + GLA_V
OFF_MQ = OFF_GR + GLA_V
OFF_MK = OFF_MQ + M_W
OFF_MV = OFF_MK + M_W
OFF_MO = OFF_MV + M_W
OFF_SMALL = OFF_MO + M_W
P_COLS = OFF_SMALL + LANES
SM_I = GLA_RANK
SM_F = GLA_RANK + M_HEADS

INPROJ_SLAB = 512
FFN_SLAB = 256
DOWN_SLAB = 256
VMEM_LIMIT = 60 * 1024 * 1024
N_WEIGHTS = 20


def _dot(a, b):
    return jnp.dot(a, b, preferred_element_type=F32)


def _dot_nt(a, b):
    return lax.dot_general(a, b, (((1,), (1,)), ((), ())), preferred_element_type=F32)


def _dot_tn(a, b):
    return lax.dot_general(a, b, (((0,), (0,)), ((), ())), preferred_element_type=F32)


def _split_hilo(x):
    hi = x.astype(BF16)
    lo = (x - hi.astype(F32)).astype(BF16)
    return hi, lo


def _log_sigmoid(x):
    return jnp.minimum(x, 0.0) - jnp.log1p(jnp.exp(-jnp.abs(x)))


def _rms(x, g):
    return x * lax.rsqrt(jnp.mean(x * x, axis=-1, keepdims=True) + EPS) * g


def _interleave(tasks, strides):
    alive = list(zip(tasks, strides))
    while alive:
        for item in list(alive):
            task, stride = item
            for _ in range(stride):
                try:
                    next(task)
                except StopIteration:
                    alive.remove(item)
                    break


def _mixer_consts(c):
    nlev = int(np.log2(c))
    assert 2 ** nlev == c
    t = np.arange(c)[:, None]
    u = np.arange(c)[None, :]
    blocks = [(u <= t), (u > t)]
    for l in range(1, nlev + 1):
        bsz, half = 2 ** l, 2 ** (l - 1)
        mid = (t // bsz) * bsz + half
        right = (t % bsz) >= half
        blocks.append(np.where(right, (u >= mid) & (u <= t), (u > t) & (u < mid)))
    a = np.concatenate(blocks, axis=0).astype(np.float32)
    a2 = np.concatenate([a, a], axis=1)
    x = np.bitwise_xor(t, u)
    lv = np.where(u > t, -1, np.where(x == 0, 0, np.floor(np.log2(np.maximum(x, 1))) + 1))
    e = np.zeros((LANES, 2 * M_HEADS * LANES), np.float32)
    for h in range(M_HEADS):
        e[SM_I + h, h * LANES:(h + 1) * LANES] = 1.0
        e[SM_F + h, (M_HEADS + h) * LANES:(M_HEADS + h + 1) * LANES] = 1.0
    e2 = np.concatenate([e, e], axis=0)
    lv4 = np.tile(lv, (1, GLA_HEADS))
    bd = (np.arange(GLA_HEADS * c)[:, None] // c == np.arange(GLA_QK)[None, :] // GLA_DK)
    return (jnp.asarray(a2, BF16), jnp.asarray(lv4, jnp.int32), jnp.asarray(e2, BF16),
            jnp.asarray(bd, BF16), nlev)


def _inproj(xf_ref, n1_ref, win_ref, wsm_ref, p_sc, ready, *, c, streams):
    rows = streams * c
    d = xf_ref.shape[-1]
    h_in = _rms(xf_ref[...].reshape(rows, d), n1_ref[...]).astype(BF16)
    for lo in range(0, OFF_SMALL, INPROJ_SLAB):
        hi = lo + INPROJ_SLAB
        p_sc[:, lo:hi] = _dot(h_in, win_ref[:, lo:hi])
        ready.append((lo, hi))
        yield
    p_sc[:, OFF_SMALL:] = _dot(h_in, wsm_ref[...])
    ready.append((OFF_SMALL, P_COLS))
    yield


def _p_reader(p_sc, ready, rs):
    def pcol(g, off, width):
        assert any(lo <= off and off + width <= hi for lo, hi in ready), (off, width)
        return p_sc[rs[g], off:off + width]
    return pcol


def _gla(wa_ref, ba_ref, gn_ref, a2_ref, lv_ref, bd_ref, p_sc, ready, mix_sc, slot, s_sc, *, c, nlev,
         streams):
    gs = range(streams)
    rs = [slice(g * c, (g + 1) * c) for g in gs]
    pcol = _p_reader(p_sc, ready, rs)
    assert (OFF_SMALL, P_COLS) in ready
    small = p_sc[:, OFF_SMALL:OFF_SMALL + LANES]
    xa = _dot(small.astype(BF16), wa_ref[...]) + ba_ref[...]
    la_hi, la_lo = _split_hilo(_log_sigmoid(xa) * (1.0 / GLA_TAU))
    yield
    z = [_dot(a2_ref[...], jnp.concatenate([la_hi[rs[g]], la_lo[rs[g]]], axis=0))
         for g in gs]
    yield
    ez = [jnp.exp(z[g]) for g in gs]
    yield
    q = [pcol(g, OFF_GQ, GLA_QK) * (GLA_DK ** -0.5) for g in gs]
    k = [pcol(g, OFF_GK, GLA_QK) for g in gs]
    q_in = [(q[g] * ez[g][0:c]).astype(BF16) for g in gs]
    k_dec = [(k[g] * ez[g][c:2 * c]).astype(BF16) for g in gs]
    dec = [ez[g][c - 1:c, :] for g in gs]
    q_lv = [[q[g].astype(BF16)] + [(q[g] * ez[g][(2 + l) * c:(3 + l) * c]).astype(BF16)
                                   for l in range(nlev)] for g in gs]
    k_lv = [[k[g].astype(BF16)] + [(k[g] * ez[g][(2 + l) * c:(3 + l) * c]).astype(BF16)
                                   for l in range(nlev)] for g in gs]
    yield
    lv = lv_ref[...]
    bd = bd_ref[...]
    sc = [jnp.zeros((c, GLA_HEADS * c), F32) for g in gs]
    for l in range(nlev + 1):
        for g in gs:
            k_bd = jnp.concatenate([k_lv[g][l]] * GLA_HEADS, axis=0) * bd
            sc[g] = jnp.where(lv == l, _dot_nt(q_lv[g][l], k_bd), sc[g])
        yield
    sc = [sc[g].astype(BF16) for g in gs]
    for h in range(GLA_HEADS):
        sl = slice(h * GLA_DK, (h + 1) * GLA_DK)
        for g in gs:
            vb = pcol(g, OFF_GV + h * GLA_DV, GLA_DV).astype(BF16)
            st = s_sc[g, :, sl]
            o = _dot_nt(q_in[g][:, sl], st.astype(BF16)) + _dot(sc[g][:, h * c:(h + 1) * c], vb)
            s_sc[g, :, sl] = dec[g][:, sl] * st + _dot_tn(vb, k_dec[g][:, sl])
            r = pcol(g, OFF_GR + h * GLA_DV, GLA_DV)
            mix_sc[slot, rs[g], h * GLA_DV:(h + 1) * GLA_DV] = (
                _rms(o, gn_ref[...]) * (r * jax.nn.sigmoid(r)))
        yield


def _mlstm(bsm_ref, mn_ref, a2_ref, e2_ref, p_sc, ready, mix_sc, slot, c_sc, m_sc, *, c, streams):
    rows = streams * c
    gs = range(streams)
    rs = [slice(g * c, (g + 1) * c) for g in gs]
    pcol = _p_reader(p_sc, ready, rs)
    assert (OFF_SMALL, P_COLS) in ready
    lane = lax.broadcasted_iota(jnp.int32, (c, LANES), 1)
    row = lax.broadcasted_iota(jnp.int32, (c, LANES), 0)
    on_diag = row == lane
    causal = lane <= row
    ones = jnp.ones((c, LANES), F32)
    lane_all = lax.broadcasted_iota(jnp.int32, (rows, LANES), 1)
    is_f = (lane_all >= SM_F) & (lane_all < SM_F + M_HEADS)

    x = p_sc[:, OFF_SMALL:OFF_SMALL + LANES] + bsm_ref[...]
    x = jnp.where(is_f, _log_sigmoid(x), x)
    x_hi, x_lo = _split_hilo(x)
    yield
    xcum = jnp.concatenate(
        [_dot(a2_ref[0:c, :], jnp.concatenate([x_hi[rs[g]], x_lo[rs[g]]], axis=0)) for g in gs],
        axis=0)
    xc = jnp.where(is_f, xcum, x)
    yield
    xe_all = _dot(jnp.concatenate(_split_hilo(xc), axis=1), e2_ref[...])
    xe = [xe_all[rs[g]] for g in gs]
    yield
    for h in range(M_HEADS):
        ie = [xe[g][:, h * LANES:(h + 1) * LANES] for g in gs]
        fe = [xe[g][:, (M_HEADS + h) * LANES:(M_HEADS + h + 1) * LANES] for g in gs]
        a = [ie[g] - fe[g] for g in gs]
        a_row = [jnp.sum(jnp.where(on_diag, a[g], 0.0), axis=0, keepdims=True) for g in gs]
        dm = [jnp.where(causal, fe[g] + a_row[g], -jnp.inf)[:, :c] for g in gs]
        gi = [fe[g] + m_sc[g, h] for g in gs]
        m_t = [jnp.maximum(gi[g], jnp.max(dm[g], axis=-1, keepdims=True)) for g in gs]
        w_int = [jnp.exp(gi[g] - m_t[g]) for g in gs]
        w = [jnp.exp(dm[g] - m_t[g][:, :c]) for g in gs]
        yield
        qh = [pcol(g, OFF_MQ + h * M_DH, M_DH).astype(BF16) for g in gs]
        kf = [pcol(g, OFF_MK + h * M_DH, M_DH) * (M_DH ** -0.5) for g in gs]
        vaug = [jnp.concatenate([pcol(g, OFF_MV + h * M_DH, M_DH), ones], axis=1).astype(BF16)
                for g in gs]
        qk = [(_dot_nt(qh[g], kf[g].astype(BF16)) * w[g]).astype(BF16) for g in gs]
        yield
        caug = [c_sc[g, h] for g in gs]
        nd = [_dot(qk[g], vaug[g])
              + jnp.concatenate([w_int[g], w_int[g]], axis=1) * _dot(qh[g], caug[g].astype(BF16))
              for g in gs]
        for g in gs:
            den = jnp.maximum(jnp.abs(nd[g][:, M_DH:]), jnp.exp(-m_t[g]))
            og = jax.nn.sigmoid(pcol(g, OFF_MO + h * M_DH, M_DH)) * (nd[g][:, :M_DH] / den)
            mix_sc[slot, rs[g], GLA_V + h * M_DH:GLA_V + (h + 1) * M_DH] = _rms(og, mn_ref[...])
        yield
        for g in gs:
            m_c = m_t[g][c - 1:c, :]
            decay = jnp.exp(gi[g][c - 1:c, :] - m_c)
            ws = jnp.exp(a[g] + (fe[g][c - 1:c, :] - m_c))
            kw = (kf[g] * ws).astype(BF16)
            c_sc[g, h] = jnp.concatenate([decay, decay], axis=1) * caug[g] + _dot_tn(kw, vaug[g])
            m_sc[g, h] = m_c
        yield


def _back(xb_ref, mix_sc, slot_prev, wo_ref, n2_ref, wg_ref, wu_ref, cw_ref, cb_ref, wd_ref, nf_ref,
          y_ref, carry_sc, *, c, streams):
    rows = streams * c
    d = xb_ref.shape[-1]
    d_ff = wg_ref.shape[1]
    nslab = d_ff // FFN_SLAB
    x1 = xb_ref[...].reshape(rows, d) + _dot(mix_sc[slot_prev].astype(BF16), wo_ref[...])
    h2 = _rms(x1, n2_ref[...]).astype(BF16)
    yield

    def gate_up(s):
        cs = slice(s * FFN_SLAB, (s + 1) * FFN_SLAB)
        return _dot(h2, wg_ref[:, cs]), _dot(h2, wu_ref[:, cs])

    row8 = lax.broadcasted_iota(jnp.int32, (SUBLANES, FFN_SLAB), 0)
    nxt = gate_up(0)
    acts = []
    yield
    for s in range(nslab):
        cs = slice(s * FFN_SLAB, (s + 1) * FFN_SLAB)
        g, u = nxt
        if s + 1 < nslab:
            nxt = gate_up(s + 1)
        yield
        r1 = pltpu.roll(g, 1, axis=0)
        r2 = pltpu.roll(g, 2, axis=0)
        g1, g2 = [], []
        for st in range(streams):
            b0 = st * c
            p0 = carry_sc[st, 0:1, cs]
            p1 = carry_sc[st, 1:2, cs]
            g1 += [jnp.where(row8 == 0, p1, r1[b0:b0 + SUBLANES]), r1[b0 + SUBLANES:b0 + c]]
            g2 += [jnp.where(row8 == 0, p0, jnp.where(row8 == 1, p1, r2[b0:b0 + SUBLANES])),
                   r2[b0 + SUBLANES:b0 + c]]
            carry_sc[st, 0:1, cs] = g[b0 + c - 2:b0 + c - 1, :]
            carry_sc[st, 1:2, cs] = g[b0 + c - 1:b0 + c, :]
        g1 = jnp.concatenate(g1, axis=0)
        g2 = jnp.concatenate(g2, axis=0)
        gc = cb_ref[:, cs] + (cw_ref[0:1, cs] * g2 + cw_ref[1:2, cs] * g1 + cw_ref[2:3, cs] * g)
        yield
        acts.append((jax.nn.gelu(gc) * u).astype(BF16))
        yield
    act = jnp.concatenate(acts, axis=1)
    down = []
    for lo in range(0, d, DOWN_SLAB):
        down.append(_dot(act, wd_ref[:, lo:lo + DOWN_SLAB]))
        yield
    y_ref[...] = _rms(x1 + jnp.concatenate(down, axis=1), nf_ref[...]).reshape(streams, c, d)


def _layer_kernel(*refs, c, nlev, streams, nj, nsteps, skew, zero_state):
    n_in = 1 + N_WEIGHTS + (0 if zero_state else 5)
    (xf_ref, n1_ref, win_ref, wsm_ref, wa_ref, ba_ref, gn_ref, bsm_ref, mn_ref, a2_ref, lv_ref,
     e2_ref, bd_ref, wo_ref, n2_ref, wg_ref, wu_ref, cw_ref, cb_ref, wd_ref, nf_ref) = refs[:1 + N_WEIGHTS]
    s0_ref, c0_ref, n0_ref, m0_ref, cst_ref = (None,) * 5 if zero_state else refs[1 + N_WEIGHTS:n_in]
    y_ref, s_out, c_out, n_out, m_out, cv_out = refs[n_in:n_in + 6]
    p_sc, mix_sc, s_sc, c_sc, m_sc, carry_sc, xprev_sc = refs[n_in + 6:]

    t = pl.program_id(0)
    tf = jnp.minimum(t, nsteps - 1)
    jf = lax.rem(tf, nj)
    tb = jnp.maximum(t - skew, 0)
    jb = lax.rem(tb, nj)
    slot = lax.rem(t, 2)
    slot_back = 1 - slot if skew else slot

    if skew:
        @pl.when(t == 0)
        def _():
            mix_sc[1] = jnp.zeros(mix_sc.shape[1:], F32)
            xprev_sc[...] = jnp.zeros(xprev_sc.shape, F32)

    @pl.when(jf == 0)
    def _():
        if zero_state:
            s_sc[...] = jnp.zeros(s_sc.shape, F32)
            c_sc[...] = jnp.zeros(c_sc.shape, F32)
            m_sc[...] = jnp.zeros(m_sc.shape, F32)
        else:
            for g in range(streams):
                for lo in range(0, GLA_QK, LANES):
                    s_sc[g, :, lo:lo + LANES] = s0_ref[g, lo:lo + LANES, :].T
            c_sc[:, :, :, :M_DH] = c0_ref[...]
            c_sc[:, :, :, M_DH:] = jnp.broadcast_to(n0_ref[...], c0_ref.shape)
            m_sc[...] = m0_ref[...]

    @pl.when(jb == 0)
    def _():
        carry_sc[...] = jnp.zeros(carry_sc.shape, F32) if zero_state else cst_ref[...]

    back = _back(xprev_sc if skew else xf_ref, mix_sc, slot_back, wo_ref, n2_ref, wg_ref, wu_ref, cw_ref,
                 cb_ref, wd_ref, nf_ref, y_ref, carry_sc, c=c, streams=streams)
    if skew:
        next(back)
    ready = []
    for _ in _inproj(xf_ref, n1_ref, win_ref, wsm_ref, p_sc, ready, c=c, streams=streams):
        pass
    mixers = [_gla(wa_ref, ba_ref, gn_ref, a2_ref, lv_ref, bd_ref, p_sc, ready, mix_sc, slot, s_sc,
                   c=c, nlev=nlev, streams=streams),
              _mlstm(bsm_ref, mn_ref, a2_ref, e2_ref, p_sc, ready, mix_sc, slot, c_sc, m_sc,
                     c=c, streams=streams)]
    if skew:
        _interleave(mixers + [back], strides=(1, 1, 3))
        xprev_sc[...] = xf_ref[...]
    else:
        _interleave(mixers, strides=(1, 1))
        _interleave([back], strides=(1,))

    @pl.when((jf == nj - 1) & (t < nsteps))
    def _():
        for g in range(streams):
            for lo in range(0, GLA_QK, LANES):
                s_out[g, lo:lo + LANES, :] = s_sc[g, :, lo:lo + LANES].T
        c_out[...] = c_sc[:, :, :, :M_DH]
        on_diag = (lax.broadcasted_iota(jnp.int32, (M_DH, M_DH), 0)
                   == lax.broadcasted_iota(jnp.int32, (M_DH, M_DH), 1))
        for g in range(streams):
            for h in range(M_HEADS):
                n_out[g, h] = jnp.sum(jnp.where(on_diag, c_sc[g, h, :, M_DH:], 0.0), axis=0, keepdims=True)
        m_out[...] = m_sc[...]

    @pl.when((jb == nj - 1) & (t >= skew))
    def _():
        cv_out[...] = carry_sc[...]


def _layer(x, states, wts):
    batch, length, d = x.shape
    c = min(CHUNK, length)
    nj = length // c
    streams = min(batch, MXU_ROWS // c)
    assert batch % streams == 0 and length % c == 0
    nsteps = (batch // streams) * nj
    d_ff = wts["w_g"].shape[1]
    a2, lv, e2, bd, nlev = _mixer_consts(c)
    skew = 1 if nsteps > 1 else 0

    def front_idx(t):
        tf = jnp.minimum(t, nsteps - 1)
        return tf // nj, tf % nj

    def back_idx(t):
        tb = jnp.maximum(t - skew, 0)
        return tb // nj, tb % nj

    once = pl.Buffered(1)
    const = lambda a: pl.BlockSpec(a.shape, lambda t: (0,) * a.ndim, pipeline_mode=once)
    fstate = lambda *blk, **kw: pl.BlockSpec(
        (streams,) + blk, lambda t: (front_idx(t)[0],) + (0,) * len(blk), **kw)
    bstate = lambda *blk, **kw: pl.BlockSpec(
        (streams,) + blk, lambda t: (back_idx(t)[0],) + (0,) * len(blk), **kw)
    consts = [wts["norm1"], wts["w_in_r"], wts["w_in_sm"], wts["wa_pad"], wts["b_gla_a"], wts["gla_norm"],
              wts["b_small"], wts["m_norm"], a2, lv, e2, bd, wts["w_out"], wts["norm2"], wts["w_g"],
              wts["w_u"], wts["conv_w"], wts["conv_b"], wts["w_d"], wts["norm_f"]]
    assert len(consts) == N_WEIGHTS
    state_ops, state_specs = [], []
    if states is not None:
        s_gla, c_m, n_m, m_m, conv_st = states
        state_ops = [
            s_gla.reshape(batch, GLA_QK, GLA_DV),
            c_m, n_m[..., None],
            jnp.broadcast_to(m_m[..., None, None], (batch, M_HEADS, 1, LANES)), conv_st]
        state_specs = [
            fstate(GLA_QK, GLA_DV, pipeline_mode=once),
            fstate(M_HEADS, M_DH, M_DH, pipeline_mode=once),
            fstate(M_HEADS, M_DH, 1, pipeline_mode=once),
            fstate(M_HEADS, 1, LANES, pipeline_mode=once),
            bstate(FFN_CONV - 1, d_ff, pipeline_mode=once)]
    outs = pl.pallas_call(
        functools.partial(_layer_kernel, c=c, nlev=nlev, streams=streams, nj=nj, nsteps=nsteps,
                          skew=skew, zero_state=states is None),
        grid=(nsteps + skew,),
        in_specs=[
            pl.BlockSpec((streams, c, d), lambda t: front_idx(t) + (0,)),
        ] + [const(a) for a in consts] + state_specs,
        out_specs=[
            pl.BlockSpec((streams, c, d), lambda t: back_idx(t) + (0,)),
            fstate(GLA_QK, GLA_DV), fstate(M_HEADS, M_DH, M_DH), fstate(M_HEADS, 1, M_DH),
            fstate(M_HEADS, 1, LANES), bstate(FFN_CONV - 1, d_ff),
        ],
        out_shape=[
            jax.ShapeDtypeStruct((batch, length, d), F32),
            jax.ShapeDtypeStruct((batch, GLA_QK, GLA_DV), F32),
            jax.ShapeDtypeStruct((batch, M_HEADS, M_DH, M_DH), F32),
            jax.ShapeDtypeStruct((batch, M_HEADS, 1, M_DH), F32),
            jax.ShapeDtypeStruct((batch, M_HEADS, 1, LANES), F32),
            jax.ShapeDtypeStruct((batch, FFN_CONV - 1, d_ff), F32),
        ],
        scratch_shapes=[
            pltpu.VMEM((streams * c, P_COLS), F32),
            pltpu.VMEM((2, streams * c, D_MIX), F32),
            pltpu.VMEM((streams, GLA_DV, GLA_QK), F32),
            pltpu.VMEM((streams, M_HEADS, M_DH, 2 * M_DH), F32),
            pltpu.VMEM((streams, M_HEADS, 1, LANES), F32),
            pltpu.VMEM((streams, FFN_CONV - 1, d_ff), F32),
            pltpu.VMEM((streams, c, d), F32),
        ],
        compiler_params=pltpu.CompilerParams(
            dimension_semantics=("arbitrary",), vmem_limit_bytes=VMEM_LIMIT),
        name="layer",
    )(x, *consts, *state_ops)
    y, s_new, c_new, n_new, m_e, conv_new = outs
    return (y, s_new.reshape(batch, GLA_HEADS, GLA_DK, GLA_DV), c_new,
            n_new.reshape(batch, M_HEADS, M_DH), m_e[:, :, 0, 0], conv_new)


def _regroup_kernel(wg_ref, wm_ref, glr_ref, mif_ref, o_ref, osm_ref, *, n_gla):
    j = pl.program_id(0)

    @pl.when(j < n_gla)
    def _():
        o_ref[...] = wg_ref[...].T.astype(BF16)

    @pl.when(j >= n_gla)
    def _():
        o_ref[...] = wm_ref[...].T.astype(BF16)

    @pl.when(j == 0)
    def _():
        d = wg_ref.shape[1]
        small = jnp.concatenate(
            [glr_ref[...], mif_ref[...], jnp.zeros((LANES - GLA_RANK - 2 * M_HEADS, d), F32)], axis=0)
        osm_ref[...] = small.T.astype(BF16)


def _regroup_w_in(w):
    d = w.shape[0]
    wt = w.T
    o_glr = OFF_MQ
    o_m = o_glr + GLA_RANK
    o_if = o_m + 4 * M_W
    nslab = OFF_SMALL // INPROJ_SLAB
    n_gla = OFF_MQ // INPROJ_SLAB
    rows = lambda n: (pl.Element(n), pl.Element(d))
    return pl.pallas_call(
        functools.partial(_regroup_kernel, n_gla=n_gla),
        grid=(nslab,),
        in_specs=[
            pl.BlockSpec(rows(INPROJ_SLAB), lambda j: (jnp.minimum(j, n_gla - 1) * INPROJ_SLAB, 0)),
            pl.BlockSpec(rows(INPROJ_SLAB), lambda j: (
                (o_m // GLA_RANK + jnp.maximum(j - n_gla, 0) * (INPROJ_SLAB // GLA_RANK)) * GLA_RANK, 0)),
            pl.BlockSpec(rows(GLA_RANK), lambda j: (o_glr, 0)),
            pl.BlockSpec(rows(2 * M_HEADS), lambda j: (o_if, 0)),
        ],
        out_specs=[
            pl.BlockSpec((d, INPROJ_SLAB), lambda j: (0, j)),
            pl.BlockSpec((d, LANES), lambda j: (0, 0)),
        ],
        out_shape=[
            jax.ShapeDtypeStruct((d, OFF_SMALL), BF16),
            jax.ShapeDtypeStruct((d, LANES), BF16),
        ],
        compiler_params=pltpu.CompilerParams(dimension_semantics=("arbitrary",)),
        name="regroup_w_in",
    )(wt, wt, wt, wt)


def kernel(x_prompt, x_sample, state_gla, state_mlstm_C, state_mlstm_n, state_mlstm_m, cache_ffn_conv,
           norm1, w_in, w_gla_a, b_gla_a, gla_norm, b_m_i, b_m_f, m_norm, w_out, norm2, w_g, w_u,
           conv_w, conv_b, w_d, norm_f):
    assert w_in.shape[0] == 1, "single-layer stack"
    w_in_r, w_in_sm = _regroup_w_in(w_in[0])
    wa_pad = jnp.concatenate(
        [w_gla_a[0], jnp.zeros((LANES - GLA_RANK, GLA_QK), w_gla_a.dtype)], axis=0).astype(BF16)
    b_small = jnp.concatenate(
        [jnp.zeros((GLA_RANK,), F32), b_m_i[0], b_m_f[0],
         jnp.zeros((LANES - GLA_RANK - 2 * M_HEADS,), F32)])[None]
    wts = dict(
        norm1=norm1, w_in_r=w_in_r, w_in_sm=w_in_sm, wa_pad=wa_pad, b_gla_a=b_gla_a, gla_norm=gla_norm,
        b_small=b_small, m_norm=m_norm, w_out=w_out[0].astype(BF16), norm2=norm2,
        w_g=w_g[0].astype(BF16), w_u=w_u[0].astype(BF16), conv_w=conv_w[0], conv_b=conv_b,
        w_d=w_d[0].astype(BF16), norm_f=norm_f[None])

    outs_p = _layer(x_prompt, None, wts)
    outs_s = _layer(x_sample, (state_gla[0], state_mlstm_C[0], state_mlstm_n[0], state_mlstm_m[0],
                               cache_ffn_conv[0]), wts)
    y_p, *st_p = outs_p
    y_s, *st_s = outs_s
    return (y_p, y_s) + tuple(s[None] for s in st_p) + tuple(s[None] for s in st_s)
```

```python
import functools

import numpy as np
import jax
import jax.numpy as jnp
from jax import lax
from jax.experimental import pallas as pl
from jax.experimental.pallas import tpu as pltpu

F32 = jnp.float32
BF16 = jnp.bfloat16

EPS = 1e-6
CHUNK = 64
GLA_HEADS = 4
GLA_DK = 64
GLA_DV = 128
GLA_RANK = 16
GLA_TAU = 16.0
M_HEADS = 4
M_DH = 128
FFN_CONV = 3
LANES = 128
SUBLANES = 8
MXU_ROWS = 256

GLA_QK = GLA_HEADS * GLA_DK
GLA_V = GLA_HEADS * GLA_DV
M_W = M_HEADS * M_DH
D_MIX = GLA_V + M_W
OFF_GQ = 0
OFF_GK = OFF_GQ + GLA_QK
OFF_GV = OFF_GK + GLA_QK
OFF_GR = OFF_GV + GLA_V
OFF_MQ = OFF_GR + GLA_V
OFF_MK = OFF_MQ + M_W
OFF_MV = OFF_MK + M_W
OFF_MO = OFF_MV + M_W
OFF_SMALL = OFF_MO + M_W
P_COLS = OFF_SMALL + LANES
SM_I = GLA_RANK
SM_F = GLA_RANK + M_HEADS

INPROJ_SLAB = 512
FFN_SLAB = 256
DOWN_SLAB = 256
VMEM_LIMIT = 60 * 1024 * 1024
N_WEIGHTS = 20


def _dot(a, b):
    return jnp.dot(a, b, preferred_element_type=F32)


def _dot_nt(a, b):
    return lax.dot_general(a, b, (((1,), (1,)), ((), ())), preferred_element_type=F32)


def _dot_tn(a, b):
    return lax.dot_general(a, b, (((0,), (0,)), ((), ())), preferred_element_type=F32)


def _split_hilo(x):
    hi = x.astype(BF16)
    lo = (x - hi.astype(F32)).astype(BF16)
    return hi, lo


def _log_sigmoid(x):
    return jnp.minimum(x, 0.0) - jnp.log1p(jnp.exp(-jnp.abs(x)))


def _rms(x, g):
    return x * lax.rsqrt(jnp.mean(x * x, axis=-1, keepdims=True) + EPS) * g


def _interleave(tasks, strides):
    alive = list(zip(tasks, strides))
    while alive:
        for item in list(alive):
            task, stride = item
            for _ in range(stride):
                try:
                    next(task)
                except StopIteration:
                    alive.remove(item)
                    break


def _mixer_consts(c):
    nlev = int(np.log2(c))
    assert 2 ** nlev == c
    t = np.arange(c)[:, None]
    u = np.arange(c)[None, :]
    blocks = [(u <= t), (u > t)]
    for l in range(1, nlev + 1):
        bsz, half = 2 ** l, 2 ** (l - 1)
        mid = (t // bsz) * bsz + half
        right = (t % bsz) >= half
        blocks.append(np.where(right, (u >= mid) & (u <= t), (u > t) & (u < mid)))
    a = np.concatenate(blocks, axis=0).astype(np.float32)
    a2 = np.concatenate([a, a], axis=1)
    x = np.bitwise_xor(t, u)
    lv = np.where(u > t, -1, np.where(x == 0, 0, np.floor(np.log2(np.maximum(x, 1))) + 1))
    e = np.zeros((LANES, 2 * M_HEADS * LANES), np.float32)
    for h in range(M_HEADS):
        e[SM_I + h, h * LANES:(h + 1) * LANES] = 1.0
        e[SM_F + h, (M_HEADS + h) * LANES:(M_HEADS + h + 1) * LANES] = 1.0
    e2 = np.concatenate([e, e], axis=0)
    lv4 = np.tile(lv, (1, GLA_HEADS))
    bd = (np.arange(GLA_HEADS * c)[:, None] // c == np.arange(GLA_QK)[None, :] // GLA_DK)
    return (jnp.asarray(a2, BF16), jnp.asarray(lv4, jnp.int32), jnp.asarray(e2, BF16),
            jnp.asarray(bd, BF16), nlev)


def _inproj(xf_ref, n1_ref, win_ref, wsm_ref, p_sc, ready, *, c, streams):
    rows = streams * c
    d = xf_ref.shape[-1]
    h_in = _rms(xf_ref[...].reshape(rows, d), n1_ref[...]).astype(BF16)
    for lo in range(0, OFF_SMALL, INPROJ_SLAB):
        hi = lo + INPROJ_SLAB
        p_sc[:, lo:hi] = _dot(h_in, win_ref[:, lo:hi])
        ready.append((lo, hi))
        yield
    p_sc[:, OFF_SMALL:] = _dot(h_in, wsm_ref[...])
    ready.append((OFF_SMALL, P_COLS))
    yield


def _p_reader(p_sc, ready, rs):
    def pcol(g, off, width):
        assert any(lo <= off and off + width <= hi for lo, hi in ready), (off, width)
        return p_sc[rs[g], off:off + width]
    return pcol


def _gla(wa_ref, ba_ref, gn_ref, a2_ref, lv_ref, bd_ref, p_sc, ready, mix_sc, slot, s_sc, *, c, nlev,
         streams):
    gs = range(streams)
    rs = [slice(g * c, (g + 1) * c) for g in gs]
    pcol = _p_reader(p_sc, ready, rs)
    assert (OFF_SMALL, P_COLS) in ready
    small = p_sc[:, OFF_SMALL:OFF_SMALL + LANES]
    xa = _dot(small.astype(BF16), wa_ref[...]) + ba_ref[...]
    la_hi, la_lo = _split_hilo(_log_sigmoid(xa) * (1.0 / GLA_TAU))
    yield
    z = [_dot(a2_ref[...], jnp.concatenate([la_hi[rs[g]], la_lo[rs[g]]], axis=0))
         for g in gs]
    yield
    ez = [jnp.exp(z[g]) for g in gs]
    yield
    q = [pcol(g, OFF_GQ, GLA_QK) * (GLA_DK ** -0.5) for g in gs]
    k = [pcol(g, OFF_GK, GLA_QK) for g in gs]
    q_in = [(q[g] * ez[g][0:c]).astype(BF16) for g in gs]
    k_dec = [(k[g] * ez[g][c:2 * c]).astype(BF16) for g in gs]
    dec = [ez[g][c - 1:c, :] for g in gs]
    q_lv = [[q[g].astype(BF16)] + [(q[g] * ez[g][(2 + l) * c:(3 + l) * c]).astype(BF16)
                                   for l in range(nlev)] for g in gs]
    k_lv = [[k[g].astype(BF16)] + [(k[g] * ez[g][(2 + l) * c:(3 + l) * c]).astype(BF16)
                                   for l in range(nlev)] for g in gs]
    yield
    lv = lv_ref[...]
    bd = bd_ref[...]
    sc = [jnp.zeros((c, GLA_HEADS * c), F32) for g in gs]
    for l in range(nlev + 1):
        for g in gs:
            k_bd = jnp.concatenate([k_lv[g][l]] * GLA_HEADS, axis=0) * bd
            sc[g] = jnp.where(lv == l, _dot_nt(q_lv[g][l], k_bd), sc[g])
        yield
    sc = [sc[g].astype(BF16) for g in gs]
    for h in range(GLA_HEADS):
        sl = slice(h * GLA_DK, (h + 1) * GLA_DK)
        for g in gs:
            vb = pcol(g, OFF_GV + h * GLA_DV, GLA_DV).astype(BF16)
            st = s_sc[g, :, sl]
            o = _dot_nt(q_in[g][:, sl], st.astype(BF16)) + _dot(sc[g][:, h * c:(h + 1) * c], vb)
            s_sc[g, :, sl] = dec[g][:, sl] * st + _dot_tn(vb, k_dec[g][:, sl])
            r = pcol(g, OFF_GR + h * GLA_DV, GLA_DV)
            mix_sc[slot, rs[g], h * GLA_DV:(h + 1) * GLA_DV] = (
                _rms(o, gn_ref[...]) * (r * jax.nn.sigmoid(r)))
        yield


def _mlstm(bsm_ref, mn_ref, a2_ref, e2_ref, p_sc, ready, mix_sc, slot, c_sc, m_sc, *, c, streams):
    rows = streams * c
    gs = range(streams)
    rs = [slice(g * c, (g + 1) * c) for g in gs]
    pcol = _p_reader(p_sc, ready, rs)
    assert (OFF_SMALL, P_COLS) in ready
    lane = lax.broadcasted_iota(jnp.int32, (c, LANES), 1)
    row = lax.broadcasted_iota(jnp.int32, (c, LANES), 0)
    on_diag = row == lane
    causal = lane <= row
    ones = jnp.ones((c, LANES), F32)
    lane_all = lax.broadcasted_iota(jnp.int32, (rows, LANES), 1)
    is_f = (lane_all >= SM_F) & (lane_all < SM_F + M_HEADS)

    x = p_sc[:, OFF_SMALL:OFF_SMALL + LANES] + bsm_ref[...]
    x = jnp.where(is_f, _log_sigmoid(x), x)
    x_hi, x_lo = _split_hilo(x)
    yield
    xcum = jnp.concatenate(
        [_dot(a2_ref[0:c, :], jnp.concatenate([x_hi[rs[g]], x_lo[rs[g]]], axis=0)) for g in gs],
        axis=0)
    xc = jnp.where(is_f, xcum, x)
    yield
    xe_all = _dot(jnp.concatenate(_split_hilo(xc), axis=1), e2_ref[...])
    xe = [xe_all[rs[g]] for g in gs]
    yield
    for h in range(M_HEADS):
        ie = [xe[g][:, h * LANES:(h + 1) * LANES] for g in gs]
        fe = [xe[g][:, (M_HEADS + h) * LANES:(M_HEADS + h + 1) * LANES] for g in gs]
        a = [ie[g] - fe[g] for g in gs]
        a_row = [jnp.sum(jnp.where(on_diag, a[g], 0.0), axis=0, keepdims=True) for g in gs]
        dm = [jnp.where(causal, fe[g] + a_row[g], -jnp.inf)[:, :c] for g in gs]
        gi = [fe[g] + m_sc[g, h] for g in gs]
        m_t = [jnp.maximum(gi[g], jnp.max(dm[g], axis=-1, keepdims=True)) for g in gs]
        w_int = [jnp.exp(gi[g] - m_t[g]) for g in gs]
        w = [jnp.exp(dm[g] - m_t[g][:, :c]) for g in gs]
        yield
        qh = [pcol(g, OFF_MQ + h * M_DH, M_DH).astype(BF16) for g in gs]
        kf = [pcol(g, OFF_MK + h * M_DH, M_DH) * (M_DH ** -0.5) for g in gs]
        vaug = [jnp.concatenate([pcol(g, OFF_MV + h * M_DH, M_DH), ones], axis=1).astype(BF16)
                for g in gs]
        qk = [(_dot_nt(qh[g], kf[g].astype(BF16)) * w[g]).astype(BF16) for g in gs]
        yield
        caug = [c_sc[g, h] for g in gs]
        nd = [_dot(qk[g], vaug[g])
              + jnp.concatenate([w_int[g], w_int[g]], axis=1) * _dot(qh[g], caug[g].astype(BF16))
              for g in gs]
        for g in gs:
            den = jnp.maximum(jnp.abs(nd[g][:, M_DH:]), jnp.exp(-m_t[g]))
            og = jax.nn.sigmoid(pcol(g, OFF_MO + h * M_DH, M_DH)) * (nd[g][:, :M_DH] / den)
            mix_sc[slot, rs[g], GLA_V + h * M_DH:GLA_V + (h + 1) * M_DH] = _rms(og, mn_ref[...])
        yield
        for g in gs:
            m_c = m_t[g][c - 1:c, :]
            decay = jnp.exp(gi[g][c - 1:c, :] - m_c)
            ws = jnp.exp(a[g] + (fe[g][c - 1:c, :] - m_c))
            kw = (kf[g] * ws).astype(BF16)
            c_sc[g, h] = jnp.concatenate([decay, decay], axis=1) * caug[g] + _dot_tn(kw, vaug[g])
            m_sc[g, h] = m_c
        yield


def _back(xb_ref, mix_sc, slot_prev, wo_ref, n2_ref, wg_ref, wu_ref, cw_ref, cb_ref, wd_ref, nf_ref,
          y_ref, carry_sc, *, c, streams):
    rows = streams * c
    d = xb_ref.shape[-1]
    d_ff = wd_ref.shape[0]
    nslab = d_ff // FFN_SLAB
    x1 = xb_ref[...].reshape(rows, d) + _dot(mix_sc[slot_prev].astype(BF16), wo_ref[...])
    h2 = _rms(x1, n2_ref[...]).astype(BF16)
    yield

    def gate_up(s):
        cs = slice(s * FFN_SLAB, (s + 1) * FFN_SLAB)
        return _dot(h2, wg_ref[:, cs]), _dot(h2, wu_ref[:, cs])

    row8 = lax.broadcasted_iota(jnp.int32, (SUBLANES, FFN_SLAB), 0)
    nxt = gate_up(0)
    acts = []
    yield
    for s in range(nslab):
        cs = slice(s * FFN_SLAB, (s + 1) * FFN_SLAB)
        g, u = nxt
        if s + 1 < nslab:
            nxt = gate_up(s + 1)
        yield
        r1 = pltpu.roll(g, 1, axis=0)
        r2 = pltpu.roll(g, 2, axis=0)
        g1, g2 = [], []
        for st in range(streams):
            b0 = st * c
            p0 = carry_sc[st, 0:1, cs]
            p1 = carry_sc[st, 1:2, cs]
            g1 += [jnp.where(row8 == 0, p1, r1[b0:b0 + SUBLANES]), r1[b0 + SUBLANES:b0 + c]]
            g2 += [jnp.where(row8 == 0, p0, jnp.where(row8 == 1, p1, r2[b0:b0 + SUBLANES])),
                   r2[b0 + SUBLANES:b0 + c]]
            carry_sc[st, 0:1, cs] = g[b0 + c - 2:b0 + c - 1, :]
            carry_sc[st, 1:2, cs] = g[b0 + c - 1:b0 + c, :]
        g1 = jnp.concatenate(g1, axis=0)
        g2 = jnp.concatenate(g2, axis=0)
        gc = cb_ref[:, cs] + (cw_ref[0:1, cs] * g2 + cw_ref[1:2, cs] * g1 + cw_ref[2:3, cs] * g)
        yield
        acts.append((jax.nn.gelu(gc) * u).astype(BF16))
        yield
    act = jnp.concatenate(acts, axis=1)
    down = []
    for lo in range(0, d, DOWN_SLAB):
        down.append(_dot(act, wd_ref[:, lo:lo + DOWN_SLAB]))
        yield
    y_ref[...] = _rms(x1 + jnp.concatenate(down, axis=1), nf_ref[...]).reshape(streams, c, d)


def _layer_kernel(*refs, c, nlev, streams, nj, nsteps, skew, zero_state):
    n_in = 2 + N_WEIGHTS + (0 if zero_state else 5)
    (xf_ref, xb_ref, n1_ref, win_ref, wsm_ref, wa_ref, ba_ref, gn_ref, bsm_ref, mn_ref, a2_ref, lv_ref,
     e2_ref, bd_ref, wo_ref, n2_ref, wg_ref, wu_ref, cw_ref, cb_ref, wd_ref, nf_ref) = refs[:2 + N_WEIGHTS]
    s0_ref, c0_ref, n0_ref, m0_ref, cst_ref = (None,) * 5 if zero_state else refs[2 + N_WEIGHTS:n_in]
    y_ref, s_out, c_out, n_out, m_out, cv_out = refs[n_in:n_in + 6]
    p_sc, mix_sc, s_sc, c_sc, m_sc, carry_sc = refs[n_in + 6:]

    t = pl.program_id(0)
    tf = jnp.minimum(t, nsteps - 1)
    jf = lax.rem(tf, nj)
    tb = jnp.maximum(t - skew, 0)
    jb = lax.rem(tb, nj)
    slot = lax.rem(t, 2)
    slot_back = 1 - slot if skew else slot

    if skew:
        @pl.when(t == 0)
        def _():
            mix_sc[1] = jnp.zeros(mix_sc.shape[1:], F32)

    @pl.when(jf == 0)
    def _():
        if zero_state:
            s_sc[...] = jnp.zeros(s_sc.shape, F32)
            c_sc[...] = jnp.zeros(c_sc.shape, F32)
            m_sc[...] = jnp.zeros(m_sc.shape, F32)
        else:
            for g in range(streams):
                for lo in range(0, GLA_QK, LANES):
                    s_sc[g, :, lo:lo + LANES] = s0_ref[g, lo:lo + LANES, :].T
            c_sc[:, :, :, :M_DH] = c0_ref[...]
            c_sc[:, :, :, M_DH:] = jnp.broadcast_to(n0_ref[...], c0_ref.shape)
            m_sc[...] = m0_ref[...]

    @pl.when(jb == 0)
    def _():
        carry_sc[...] = jnp.zeros(carry_sc.shape, F32) if zero_state else cst_ref[...]

    back = _back(xb_ref, mix_sc, slot_back, wo_ref, n2_ref, wg_ref, wu_ref, cw_ref, cb_ref, wd_ref,
                 nf_ref, y_ref, carry_sc, c=c, streams=streams)
    if skew:
        next(back)
    ready = []
    for _ in _inproj(xf_ref, n1_ref, win_ref, wsm_ref, p_sc, ready, c=c, streams=streams):
        pass
    mixers = [_gla(wa_ref, ba_ref, gn_ref, a2_ref, lv_ref, bd_ref, p_sc, ready, mix_sc, slot, s_sc,
                   c=c, nlev=nlev, streams=streams),
              _mlstm(bsm_ref, mn_ref, a2_ref, e2_ref, p_sc, ready, mix_sc, slot, c_sc, m_sc,
                     c=c, streams=streams)]
    if skew:
        _interleave(mixers + [back], strides=(1, 1, 3))
    else:
        _interleave(mixers, strides=(1, 1))
        _interleave([back], strides=(1,))

    @pl.when((jf == nj - 1) & (t < nsteps))
    def _():
        for g in range(streams):
            for lo in range(0, GLA_QK, LANES):
                s_out[g, lo:lo + LANES, :] = s_sc[g, :, lo:lo + LANES].T
        c_out[...] = c_sc[:, :, :, :M_DH]
        on_diag = (lax.broadcasted_iota(jnp.int32, (M_DH, M_DH), 0)
                   == lax.broadcasted_iota(jnp.int32, (M_DH, M_DH), 1))
        for g in range(streams):
            for h in range(M_HEADS):
                n_out[g, h] = jnp.sum(jnp.where(on_diag, c_sc[g, h, :, M_DH:], 0.0), axis=0, keepdims=True)
        m_out[...] = m_sc[...]

    @pl.when((jb == nj - 1) & (t >= skew))
    def _():
        cv_out[...] = carry_sc[...]


def _layer(x, states, wts):
    batch, length, d = x.shape
    c = min(CHUNK, length)
    nj = length // c
    streams = min(batch, MXU_ROWS // c)
    assert batch % streams == 0 and length % c == 0
    nsteps = (batch // streams) * nj
    d_ff = wts["w_d"].shape[0]
    a2, lv, e2, bd, nlev = _mixer_consts(c)
    skew = 1 if nsteps > 1 else 0

    def front_idx(t):
        tf = jnp.minimum(t, nsteps - 1)
        return tf // nj, tf % nj

    def back_idx(t):
        tb = jnp.maximum(t - skew, 0)
        return tb // nj, tb % nj

    once = pl.Buffered(1)
    const = lambda a: pl.BlockSpec(a.shape, lambda t: (0,) * a.ndim, pipeline_mode=once)
    fstate = lambda *blk, **kw: pl.BlockSpec(
        (streams,) + blk, lambda t: (front_idx(t)[0],) + (0,) * len(blk), **kw)
    bstate = lambda *blk, **kw: pl.BlockSpec(
        (streams,) + blk, lambda t: (back_idx(t)[0],) + (0,) * len(blk), **kw)
    consts = [wts["norm1"], wts["w_in_r"], wts["w_in_sm"], wts["wa_pad"], wts["b_gla_a"], wts["gla_norm"],
              wts["b_small"], wts["m_norm"], a2, lv, e2, bd, wts["w_out"], wts["norm2"], wts["w_g"],
              wts["w_u"], wts["conv_w"], wts["conv_b"], wts["w_d"], wts["norm_f"]]
    assert len(consts) == N_WEIGHTS
    state_ops, state_specs = [], []
    if states is not None:
        s_gla, c_m, n_m, m_m, conv_st = states
        state_ops = [
            s_gla.reshape(batch, GLA_QK, GLA_DV),
            c_m, n_m[..., None],
            jnp.broadcast_to(m_m[..., None, None], (batch, M_HEADS, 1, LANES)), conv_st]
        state_specs = [
            fstate(GLA_QK, GLA_DV, pipeline_mode=once),
            fstate(M_HEADS, M_DH, M_DH, pipeline_mode=once),
            fstate(M_HEADS, M_DH, 1, pipeline_mode=once),
            fstate(M_HEADS, 1, LANES, pipeline_mode=once),
            bstate(FFN_CONV - 1, d_ff, pipeline_mode=once)]
    outs = pl.pallas_call(
        functools.partial(_layer_kernel, c=c, nlev=nlev, streams=streams, nj=nj, nsteps=nsteps,
                          skew=skew, zero_state=states is None),
        grid=(nsteps + skew,),
        in_specs=[
            pl.BlockSpec((streams, c, d), lambda t: front_idx(t) + (0,)),
            pl.BlockSpec((streams, c, d), lambda t: back_idx(t) + (0,)),
        ] + [const(a) for a in consts] + state_specs,
        out_specs=[
            pl.BlockSpec((streams, c, d), lambda t: back_idx(t) + (0,)),
            fstate(GLA_QK, GLA_DV), fstate(M_HEADS, M_DH, M_DH), fstate(M_HEADS, 1, M_DH),
            fstate(M_HEADS, 1, LANES), bstate(FFN_CONV - 1, d_ff),
        ],
        out_shape=[
            jax.ShapeDtypeStruct((batch, length, d), F32),
            jax.ShapeDtypeStruct((batch, GLA_QK, GLA_DV), F32),
            jax.ShapeDtypeStruct((batch, M_HEADS, M_DH, M_DH), F32),
            jax.ShapeDtypeStruct((batch, M_HEADS, 1, M_DH), F32),
            jax.ShapeDtypeStruct((batch, M_HEADS, 1, LANES), F32),
            jax.ShapeDtypeStruct((batch, FFN_CONV - 1, d_ff), F32),
        ],
        scratch_shapes=[
            pltpu.VMEM((streams * c, P_COLS), F32),
            pltpu.VMEM((2, streams * c, D_MIX), F32),
            pltpu.VMEM((streams, GLA_DV, GLA_QK), F32),
            pltpu.VMEM((streams, M_HEADS, M_DH, 2 * M_DH), F32),
            pltpu.VMEM((streams, M_HEADS, 1, LANES), F32),
            pltpu.VMEM((streams, FFN_CONV - 1, d_ff), F32),
        ],
        compiler_params=pltpu.CompilerParams(
            dimension_semantics=("arbitrary",), vmem_limit_bytes=VMEM_LIMIT),
        name="layer",
    )(x, x, *consts, *state_ops)
    y, s_new, c_new, n_new, m_e, conv_new = outs
    return (y, s_new.reshape(batch, GLA_HEADS, GLA_DK, GLA_DV), c_new,
            n_new.reshape(batch, M_HEADS, M_DH), m_e[:, :, 0, 0], conv_new)


def _regroup_kernel(wg_ref, wm_ref, glr_ref, mif_ref, o_ref, osm_ref, *, n_gla):
    j = pl.program_id(0)

    @pl.when(j < n_gla)
    def _():
        o_ref[...] = wg_ref[...].T.astype(BF16)

    @pl.when(j >= n_gla)
    def _():
        o_ref[...] = wm_ref[...].T.astype(BF16)

    @pl.when(j == 0)
    def _():
        d = wg_ref.shape[1]
        small = jnp.concatenate(
            [glr_ref[...], mif_ref[...], jnp.zeros((LANES - GLA_RANK - 2 * M_HEADS, d), F32)], axis=0)
        osm_ref[...] = small.T.astype(BF16)


def _regroup_w_in(w):
    d = w.shape[0]
    wt = w.T
    o_glr = OFF_MQ
    o_m = o_glr + GLA_RANK
    o_if = o_m + 4 * M_W
    nslab = OFF_SMALL // INPROJ_SLAB
    n_gla = OFF_MQ // INPROJ_SLAB
    rows = lambda n: (pl.Element(n), pl.Element(d))
    return pl.pallas_call(
        functools.partial(_regroup_kernel, n_gla=n_gla),
        grid=(nslab,),
        in_specs=[
            pl.BlockSpec(rows(INPROJ_SLAB), lambda j: (jnp.minimum(j, n_gla - 1) * INPROJ_SLAB, 0)),
            pl.BlockSpec(rows(INPROJ_SLAB), lambda j: (
                (o_m // GLA_RANK + jnp.maximum(j - n_gla, 0) * (INPROJ_SLAB // GLA_RANK)) * GLA_RANK, 0)),
            pl.BlockSpec(rows(GLA_RANK), lambda j: (o_glr, 0)),
            pl.BlockSpec(rows(2 * M_HEADS), lambda j: (o_if, 0)),
        ],
        out_specs=[
            pl.BlockSpec((d, INPROJ_SLAB), lambda j: (0, j)),
            pl.BlockSpec((d, LANES), lambda j: (0, 0)),
        ],
        out_shape=[
            jax.ShapeDtypeStruct((d, OFF_SMALL), BF16),
            jax.ShapeDtypeStruct((d, LANES), BF16),
        ],
        compiler_params=pltpu.CompilerParams(dimension_semantics=("arbitrary",)),
        name="regroup_w_in",
    )(wt, wt, wt, wt)


def kernel(x_prompt, x_sample, state_gla, state_mlstm_C, state_mlstm_n, state_mlstm_m, cache_ffn_conv,
           norm1, w_in, w_gla_a, b_gla_a, gla_norm, b_m_i, b_m_f, m_norm, w_out, norm2, w_g, w_u,
           conv_w, conv_b, w_d, norm_f):
    assert w_in.shape[0] == 1, "single-layer stack"
    w_in_r, w_in_sm = _regroup_w_in(w_in[0])
    wa_pad = jnp.concatenate(
        [w_gla_a[0], jnp.zeros((LANES - GLA_RANK, GLA_QK), w_gla_a.dtype)], axis=0).astype(BF16)
    b_small = jnp.concatenate(
        [jnp.zeros((GLA_RANK,), F32), b_m_i[0], b_m_f[0],
         jnp.zeros((LANES - GLA_RANK - 2 * M_HEADS,), F32)])[None]
    wts = dict(
        norm1=norm1, w_in_r=w_in_r, w_in_sm=w_in_sm, wa_pad=wa_pad, b_gla_a=b_gla_a, gla_norm=gla_norm,
        b_small=b_small, m_norm=m_norm, w_out=w_out[0].astype(BF16), norm2=norm2,
        w_g=jnp.pad(w_g[0], ((0, 0), (0, LANES))).astype(BF16),
        w_u=jnp.pad(w_u[0], ((0, 0), (0, LANES))).astype(BF16), conv_w=conv_w[0], conv_b=conv_b,
        w_d=w_d[0].astype(BF16), norm_f=norm_f[None])

    outs_p = _layer(x_prompt, None, wts)
    outs_s = _layer(x_sample, (state_gla[0], state_mlstm_C[0], state_mlstm_n[0], state_mlstm_m[0],
                               cache_ffn_conv[0]), wts)
    y_p, *st_p = outs_p
    y_s, *st_s = outs_s
    return (y_p, y_s) + tuple(s[None] for s in st_p) + tuple(s[None] for s in st_s)
```

```python
import functools

import numpy as np
import jax
import jax.numpy as jnp
from jax import lax
from jax.experimental import pallas as pl
from jax.experimental.pallas import tpu as pltpu

F32 = jnp.float32
BF16 = jnp.bfloat16

EPS = 1e-6
CHUNK = 64
GLA_HEADS = 4
GLA_DK = 64
GLA_DV = 128
GLA_RANK = 16
GLA_TAU = 16.0
M_HEADS = 4
M_DH = 128
FFN_CONV = 3
LANES = 128
SUBLANES = 8
MXU_ROWS = 256

GLA_QK = GLA_HEADS * GLA_DK
GLA_V = GLA_HEADS * GLA_DV
M_W = M_HEADS * M_DH
D_MIX = GLA_V + M_W
OFF_GQ = 0
OFF_GK = OFF_GQ + GLA_QK
OFF_GV = OFF_GK + GLA_QK
OFF_GR = OFF_GV + GLA_V
OFF_MQ = OFF_GR + GLA_V
OFF_MK = OFF_MQ + M_W
OFF_MV = OFF_MK + M_W
OFF_MO = OFF_MV + M_W
OFF_SMALL = OFF_MO + M_W
P_COLS = OFF_SMALL + LANES
SM_I = GLA_RANK
SM_F = GLA_RANK + M_HEADS

INPROJ_SLAB = 512
FFN_SLAB = 256
DOWN_SLAB = 256
VMEM_LIMIT = 60 * 1024 * 1024
N_WEIGHTS = 20


def _dot(a, b):
    return jnp.dot(a, b, preferred_element_type=F32)


def _dot_nt(a, b):
    return lax.dot_general(a, b, (((1,), (1,)), ((), ())), preferred_element_type=F32)


def _dot_tn(a, b):
    return lax.dot_general(a, b, (((0,), (0,)), ((), ())), preferred_element_type=F32)


def _split_hilo(x):
    hi = x.astype(BF16)
    lo = (x - hi.astype(F32)).astype(BF16)
    return hi, lo


def _log_sigmoid(x):
    return jnp.minimum(x, 0.0) - jnp.log1p(jnp.exp(-jnp.abs(x)))


def _rms(x, g):
    return x * lax.rsqrt(jnp.mean(x * x, axis=-1, keepdims=True) + EPS) * g


def _interleave(tasks, strides):
    alive = list(zip(tasks, strides))
    while alive:
        for item in list(alive):
            task, stride = item
            for _ in range(stride):
                try:
                    next(task)
                except StopIteration:
                    alive.remove(item)
                    break


def _mixer_consts(c):
    nlev = int(np.log2(c))
    assert 2 ** nlev == c
    t = np.arange(c)[:, None]
    u = np.arange(c)[None, :]
    blocks = [(u <= t), (u > t)]
    for l in range(1, nlev + 1):
        bsz, half = 2 ** l, 2 ** (l - 1)
        mid = (t // bsz) * bsz + half
        right = (t % bsz) >= half
        blocks.append(np.where(right, (u >= mid) & (u <= t), (u > t) & (u < mid)))
    a = np.concatenate(blocks, axis=0).astype(np.float32)
    a2 = np.concatenate([a, a], axis=1)
    x = np.bitwise_xor(t, u)
    lv = np.where(u > t, -1, np.where(x == 0, 0, np.floor(np.log2(np.maximum(x, 1))) + 1))
    e = np.zeros((LANES, 2 * M_HEADS * LANES), np.float32)
    for h in range(M_HEADS):
        e[SM_I + h, h * LANES:(h + 1) * LANES] = 1.0
        e[SM_F + h, (M_HEADS + h) * LANES:(M_HEADS + h + 1) * LANES] = 1.0
    e2 = np.concatenate([e, e], axis=0)
    lv4 = np.tile(lv, (1, GLA_HEADS))
    bd = (np.arange(GLA_HEADS * c)[:, None] // c == np.arange(GLA_QK)[None, :] // GLA_DK)
    return (jnp.asarray(a2, BF16), jnp.asarray(lv4, jnp.int32), jnp.asarray(e2, BF16),
            jnp.asarray(bd, BF16), nlev)


def _inproj(xf_ref, n1_ref, win_ref, wsm_ref, p_sc, ready, *, c, streams):
    rows = streams * c
    d = xf_ref.shape[-1]
    h_in = _rms(xf_ref[...].reshape(rows, d), n1_ref[...]).astype(BF16)
    for lo in range(0, OFF_SMALL, INPROJ_SLAB):
        hi = lo + INPROJ_SLAB
        p_sc[:, lo:hi] = _dot(h_in, win_ref[:, lo:hi])
        ready.append((lo, hi))
        yield
    p_sc[:, OFF_SMALL:] = _dot(h_in, wsm_ref[...])
    ready.append((OFF_SMALL, P_COLS))
    yield


def _p_reader(p_sc, ready, rs):
    def pcol(g, off, width):
        assert any(lo <= off and off + width <= hi for lo, hi in ready), (off, width)
        return p_sc[rs[g], off:off + width]
    return pcol


def _gla(wa_ref, ba_ref, gn_ref, a2_ref, lv_ref, bd_ref, p_sc, ready, mix_sc, slot, s_sc, *, c, nlev,
         streams):
    gs = range(streams)
    rs = [slice(g * c, (g + 1) * c) for g in gs]
    pcol = _p_reader(p_sc, ready, rs)
    assert (OFF_SMALL, P_COLS) in ready
    small = p_sc[:, OFF_SMALL:OFF_SMALL + LANES]
    xa = _dot(small.astype(BF16), wa_ref[...]) + ba_ref[...]
    la_hi, la_lo = _split_hilo(_log_sigmoid(xa) * (1.0 / GLA_TAU))
    yield
    z = [_dot(a2_ref[...], jnp.concatenate([la_hi[rs[g]], la_lo[rs[g]]], axis=0))
         for g in gs]
    yield
    ez = [jnp.exp(z[g]) for g in gs]
    yield
    q = [pcol(g, OFF_GQ, GLA_QK) * (GLA_DK ** -0.5) for g in gs]
    k = [pcol(g, OFF_GK, GLA_QK) for g in gs]
    q_in = [(q[g] * ez[g][0:c]).astype(BF16) for g in gs]
    k_dec = [(k[g] * ez[g][c:2 * c]).astype(BF16) for g in gs]
    dec = [ez[g][c - 1:c, :] for g in gs]
    q_lv = [[q[g].astype(BF16)] + [(q[g] * ez[g][(2 + l) * c:(3 + l) * c]).astype(BF16)
                                   for l in range(nlev)] for g in gs]
    k_lv = [[k[g].astype(BF16)] + [(k[g] * ez[g][(2 + l) * c:(3 + l) * c]).astype(BF16)
                                   for l in range(nlev)] for g in gs]
    yield
    lv = lv_ref[...]
    bd = bd_ref[...]
    sc = [jnp.zeros((c, GLA_HEADS * c), F32) for g in gs]
    for l in range(nlev + 1):
        for g in gs:
            k_bd = jnp.concatenate([k_lv[g][l]] * GLA_HEADS, axis=0) * bd
            sc[g] = jnp.where(lv == l, _dot_nt(q_lv[g][l], k_bd), sc[g])
        yield
    sc = [sc[g].astype(BF16) for g in gs]
    for h in range(GLA_HEADS):
        sl = slice(h * GLA_DK, (h + 1) * GLA_DK)
        for g in gs:
            vb = pcol(g, OFF_GV + h * GLA_DV, GLA_DV).astype(BF16)
            st = s_sc[g, :, sl]
            o = _dot_nt(q_in[g][:, sl], st.astype(BF16)) + _dot(sc[g][:, h * c:(h + 1) * c], vb)
            s_sc[g, :, sl] = dec[g][:, sl] * st + _dot_tn(vb, k_dec[g][:, sl])
            r = pcol(g, OFF_GR + h * GLA_DV, GLA_DV)
            mix_sc[slot, rs[g], h * GLA_DV:(h + 1) * GLA_DV] = (
                _rms(o, gn_ref[...]) * (r * jax.nn.sigmoid(r)))
        yield


def _mlstm(bsm_ref, mn_ref, a2_ref, e2_ref, p_sc, ready, mix_sc, slot, c_sc, m_sc, *, c, streams):
    rows = streams * c
    gs = range(streams)
    rs = [slice(g * c, (g + 1) * c) for g in gs]
    pcol = _p_reader(p_sc, ready, rs)
    assert (OFF_SMALL, P_COLS) in ready
    lane = lax.broadcasted_iota(jnp.int32, (c, LANES), 1)
    row = lax.broadcasted_iota(jnp.int32, (c, LANES), 0)
    on_diag = row == lane
    causal = lane <= row
    ones = jnp.ones((c, LANES), F32)
    lane_all = lax.broadcasted_iota(jnp.int32, (rows, LANES), 1)
    is_f = (lane_all >= SM_F) & (lane_all < SM_F + M_HEADS)

    x = p_sc[:, OFF_SMALL:OFF_SMALL + LANES] + bsm_ref[...]
    x = jnp.where(is_f, _log_sigmoid(x), x)
    x_hi, x_lo = _split_hilo(x)
    yield
    xcum = jnp.concatenate(
        [_dot(a2_ref[0:c, :], jnp.concatenate([x_hi[rs[g]], x_lo[rs[g]]], axis=0)) for g in gs],
        axis=0)
    xc = jnp.where(is_f, xcum, x)
    yield
    xe_all = _dot(jnp.concatenate(_split_hilo(xc), axis=1), e2_ref[...])
    xe = [xe_all[rs[g]] for g in gs]
    yield
    for h in range(M_HEADS):
        ie = [xe[g][:, h * LANES:(h + 1) * LANES] for g in gs]
        fe = [xe[g][:, (M_HEADS + h) * LANES:(M_HEADS + h + 1) * LANES] for g in gs]
        a = [ie[g] - fe[g] for g in gs]
        a_row = [jnp.sum(jnp.where(on_diag, a[g], 0.0), axis=0, keepdims=True) for g in gs]
        dm = [jnp.where(causal, fe[g] + a_row[g], -jnp.inf)[:, :c] for g in gs]
        gi = [fe[g] + m_sc[g, h] for g in gs]
        m_t = [jnp.maximum(gi[g], jnp.max(dm[g], axis=-1, keepdims=True)) for g in gs]
        w_int = [jnp.exp(gi[g] - m_t[g]) for g in gs]
        w = [jnp.exp(dm[g] - m_t[g][:, :c]) for g in gs]
        yield
        qh = [pcol(g, OFF_MQ + h * M_DH, M_DH).astype(BF16) for g in gs]
        kf = [pcol(g, OFF_MK + h * M_DH, M_DH) * (M_DH ** -0.5) for g in gs]
        vaug = [jnp.concatenate([pcol(g, OFF_MV + h * M_DH, M_DH), ones], axis=1).astype(BF16)
                for g in gs]
        qk = [(_dot_nt(qh[g], kf[g].astype(BF16)) * w[g]).astype(BF16) for g in gs]
        yield
        caug = [c_sc[g, h] for g in gs]
        nd = [_dot(qk[g], vaug[g])
              + jnp.concatenate([w_int[g], w_int[g]], axis=1) * _dot(qh[g], caug[g].astype(BF16))
              for g in gs]
        for g in gs:
            den = jnp.maximum(jnp.abs(nd[g][:, M_DH:]), jnp.exp(-m_t[g]))
            og = jax.nn.sigmoid(pcol(g, OFF_MO + h * M_DH, M_DH)) * (nd[g][:, :M_DH] / den)
            mix_sc[slot, rs[g], GLA_V + h * M_DH:GLA_V + (h + 1) * M_DH] = _rms(og, mn_ref[...])
        yield
        for g in gs:
            m_c = m_t[g][c - 1:c, :]
            decay = jnp.exp(gi[g][c - 1:c, :] - m_c)
            ws = jnp.exp(a[g] + (fe[g][c - 1:c, :] - m_c))
            kw = (kf[g] * ws).astype(BF16)
            c_sc[g, h] = jnp.concatenate([decay, decay], axis=1) * caug[g] + _dot_tn(kw, vaug[g])
            m_sc[g, h] = m_c
        yield


def _back(xb_ref, mix_sc, slot_prev, wo_ref, n2_ref, wg_ref, wu_ref, cw_ref, cb_ref, wd_ref, nf_ref,
          y_ref, carry_sc, *, c, streams):
    rows = streams * c
    d = xb_ref.shape[-1]
    d_ff = wg_ref.shape[1]
    nslab = d_ff // FFN_SLAB
    x1 = xb_ref[...].reshape(rows, d) + _dot(mix_sc[slot_prev].astype(BF16), wo_ref[...])
    h2 = _rms(x1, n2_ref[...]).astype(BF16)
    yield

    def gate_up(s):
        cs = slice(s * FFN_SLAB, (s + 1) * FFN_SLAB)
        return _dot(h2, wg_ref[:, cs]), _dot(h2, wu_ref[:, cs])

    row8 = lax.broadcasted_iota(jnp.int32, (SUBLANES, FFN_SLAB), 0)
    nxt = gate_up(0)
    acts = []
    yield
    for s in range(nslab):
        cs = slice(s * FFN_SLAB, (s + 1) * FFN_SLAB)
        g, u = nxt
        if s + 1 < nslab:
            nxt = gate_up(s + 1)
        yield
        r1 = pltpu.roll(g, 1, axis=0)
        r2 = pltpu.roll(g, 2, axis=0)
        g1, g2 = [], []
        for st in range(streams):
            b0 = st * c
            p0 = carry_sc[st, 0:1, cs]
            p1 = carry_sc[st, 1:2, cs]
            g1 += [jnp.where(row8 == 0, p1, r1[b0:b0 + SUBLANES]), r1[b0 + SUBLANES:b0 + c]]
            g2 += [jnp.where(row8 == 0, p0, jnp.where(row8 == 1, p1, r2[b0:b0 + SUBLANES])),
                   r2[b0 + SUBLANES:b0 + c]]
            carry_sc[st, 0:1, cs] = g[b0 + c - 2:b0 + c - 1, :]
            carry_sc[st, 1:2, cs] = g[b0 + c - 1:b0 + c, :]
        g1 = jnp.concatenate(g1, axis=0)
        g2 = jnp.concatenate(g2, axis=0)
        gc = cb_ref[:, cs] + (cw_ref[0:1, cs] * g2 + cw_ref[1:2, cs] * g1 + cw_ref[2:3, cs] * g)
        yield
        acts.append((jax.nn.gelu(gc) * u).astype(BF16))
        yield
    act = jnp.concatenate(acts, axis=1)
    down = []
    for lo in range(0, d, DOWN_SLAB):
        down.append(_dot(act, wd_ref[:, lo:lo + DOWN_SLAB]))
        yield
    y_ref[...] = _rms(x1 + jnp.concatenate(down, axis=1), nf_ref[...]).reshape(streams, c, d)


def _layer_kernel(*refs, c, nlev, streams, nj, nsteps, skew, zero_state):
    n_in = 2 + N_WEIGHTS + (0 if zero_state else 5)
    (xf_ref, xb_ref, n1_ref, win_ref, wsm_ref, wa_ref, ba_ref, gn_ref, bsm_ref, mn_ref, a2_ref, lv_ref,
     e2_ref, bd_ref, wo_ref, n2_ref, wg_ref, wu_ref, cw_ref, cb_ref, wd_ref, nf_ref) = refs[:2 + N_WEIGHTS]
    s0_ref, c0_ref, n0_ref, m0_ref, cst_ref = (None,) * 5 if zero_state else refs[2 + N_WEIGHTS:n_in]
    y_ref, s_out, c_out, n_out, m_out, cv_out = refs[n_in:n_in + 6]
    p_sc, mix_sc, s_sc, c_sc, m_sc, carry_sc = refs[n_in + 6:n_in + 12]
    w_copies = []
    if not skew:
        w_bufs, w_sem = refs[n_in + 12:n_in + 18], refs[n_in + 18]
        w_hbm = (win_ref, wsm_ref, wo_ref, wg_ref, wu_ref, wd_ref)
        w_copies = [pltpu.make_async_copy(h, v, w_sem.at[i]) for i, (h, v) in enumerate(zip(w_hbm, w_bufs))]
        for cp in w_copies:
            cp.start()
        win_ref, wsm_ref, wo_ref, wg_ref, wu_ref, wd_ref = w_bufs

    t = pl.program_id(0)
    tf = jnp.minimum(t, nsteps - 1)
    jf = lax.rem(tf, nj)
    tb = jnp.maximum(t - skew, 0)
    jb = lax.rem(tb, nj)
    slot = lax.rem(t, 2)
    slot_back = 1 - slot if skew else slot

    if skew:
        @pl.when(t == 0)
        def _():
            mix_sc[1] = jnp.zeros(mix_sc.shape[1:], F32)

    @pl.when(jf == 0)
    def _():
        if zero_state:
            s_sc[...] = jnp.zeros(s_sc.shape, F32)
            c_sc[...] = jnp.zeros(c_sc.shape, F32)
            m_sc[...] = jnp.zeros(m_sc.shape, F32)
        else:
            for g in range(streams):
                for lo in range(0, GLA_QK, LANES):
                    s_sc[g, :, lo:lo + LANES] = s0_ref[g, lo:lo + LANES, :].T
            c_sc[:, :, :, :M_DH] = c0_ref[...]
            c_sc[:, :, :, M_DH:] = jnp.broadcast_to(n0_ref[...], c0_ref.shape)
            m_sc[...] = m0_ref[...]

    @pl.when(jb == 0)
    def _():
        carry_sc[...] = jnp.zeros(carry_sc.shape, F32) if zero_state else cst_ref[...]

    back = _back(xb_ref, mix_sc, slot_back, wo_ref, n2_ref, wg_ref, wu_ref, cw_ref, cb_ref, wd_ref,
                 nf_ref, y_ref, carry_sc, c=c, streams=streams)
    if skew:
        next(back)
    ready = []
    for cp in w_copies[:2]:
        cp.wait()
    for _ in _inproj(xf_ref, n1_ref, win_ref, wsm_ref, p_sc, ready, c=c, streams=streams):
        pass
    mixers = [_gla(wa_ref, ba_ref, gn_ref, a2_ref, lv_ref, bd_ref, p_sc, ready, mix_sc, slot, s_sc,
                   c=c, nlev=nlev, streams=streams),
              _mlstm(bsm_ref, mn_ref, a2_ref, e2_ref, p_sc, ready, mix_sc, slot, c_sc, m_sc,
                     c=c, streams=streams)]
    if skew:
        _interleave(mixers + [back], strides=(1, 1, 3))
    else:
        _interleave(mixers, strides=(1, 1))
        for cp in w_copies[2:]:
            cp.wait()
        _interleave([back], strides=(1,))

    @pl.when((jf == nj - 1) & (t < nsteps))
    def _():
        for g in range(streams):
            for lo in range(0, GLA_QK, LANES):
                s_out[g, lo:lo + LANES, :] = s_sc[g, :, lo:lo + LANES].T
        c_out[...] = c_sc[:, :, :, :M_DH]
        on_diag = (lax.broadcasted_iota(jnp.int32, (M_DH, M_DH), 0)
                   == lax.broadcasted_iota(jnp.int32, (M_DH, M_DH), 1))
        for g in range(streams):
            for h in range(M_HEADS):
                n_out[g, h] = jnp.sum(jnp.where(on_diag, c_sc[g, h, :, M_DH:], 0.0), axis=0, keepdims=True)
        m_out[...] = m_sc[...]

    @pl.when((jb == nj - 1) & (t >= skew))
    def _():
        cv_out[...] = carry_sc[...]


def _layer(x, states, wts):
    batch, length, d = x.shape
    c = min(CHUNK, length)
    nj = length // c
    streams = min(batch, MXU_ROWS // c)
    assert batch % streams == 0 and length % c == 0
    nsteps = (batch // streams) * nj
    d_ff = wts["w_g"].shape[1]
    a2, lv, e2, bd, nlev = _mixer_consts(c)
    skew = 1 if nsteps > 1 else 0

    def front_idx(t):
        tf = jnp.minimum(t, nsteps - 1)
        return tf // nj, tf % nj

    def back_idx(t):
        tb = jnp.maximum(t - skew, 0)
        return tb // nj, tb % nj

    once = pl.Buffered(1)
    const = lambda a: pl.BlockSpec(a.shape, lambda t: (0,) * a.ndim, pipeline_mode=once)
    fstate = lambda *blk, **kw: pl.BlockSpec(
        (streams,) + blk, lambda t: (front_idx(t)[0],) + (0,) * len(blk), **kw)
    bstate = lambda *blk, **kw: pl.BlockSpec(
        (streams,) + blk, lambda t: (back_idx(t)[0],) + (0,) * len(blk), **kw)
    consts = [wts["norm1"], wts["w_in_r"], wts["w_in_sm"], wts["wa_pad"], wts["b_gla_a"], wts["gla_norm"],
              wts["b_small"], wts["m_norm"], a2, lv, e2, bd, wts["w_out"], wts["norm2"], wts["w_g"],
              wts["w_u"], wts["conv_w"], wts["conv_b"], wts["w_d"], wts["norm_f"]]
    assert len(consts) == N_WEIGHTS
    const_specs = [const(a) for a in consts]
    w_scratch = []
    if not skew:
        big = [wts[k] for k in ("w_in_r", "w_in_sm", "w_out", "w_g", "w_u", "w_d")]
        for i, a in enumerate(consts):
            if any(a is b for b in big):
                const_specs[i] = pl.BlockSpec(memory_space=pl.ANY)
        w_scratch = [pltpu.VMEM(a.shape, a.dtype) for a in big] + [pltpu.SemaphoreType.DMA((len(big),))]
    state_ops, state_specs = [], []
    if states is not None:
        s_gla, c_m, n_m, m_m, conv_st = states
        state_ops = [
            s_gla.reshape(batch, GLA_QK, GLA_DV),
            c_m, n_m[..., None],
            jnp.broadcast_to(m_m[..., None, None], (batch, M_HEADS, 1, LANES)), conv_st]
        state_specs = [
            fstate(GLA_QK, GLA_DV, pipeline_mode=once),
            fstate(M_HEADS, M_DH, M_DH, pipeline_mode=once),
            fstate(M_HEADS, M_DH, 1, pipeline_mode=once),
            fstate(M_HEADS, 1, LANES, pipeline_mode=once),
            bstate(FFN_CONV - 1, d_ff, pipeline_mode=once)]
    outs = pl.pallas_call(
        functools.partial(_layer_kernel, c=c, nlev=nlev, streams=streams, nj=nj, nsteps=nsteps,
                          skew=skew, zero_state=states is None),
        grid=(nsteps + skew,),
        in_specs=[
            pl.BlockSpec((streams, c, d), lambda t: front_idx(t) + (0,)),
            pl.BlockSpec((streams, c, d), lambda t: back_idx(t) + (0,)),
        ] + const_specs + state_specs,
        out_specs=[
            pl.BlockSpec((streams, c, d), lambda t: back_idx(t) + (0,)),
            fstate(GLA_QK, GLA_DV), fstate(M_HEADS, M_DH, M_DH), fstate(M_HEADS, 1, M_DH),
            fstate(M_HEADS, 1, LANES), bstate(FFN_CONV - 1, d_ff),
        ],
        out_shape=[
            jax.ShapeDtypeStruct((batch, length, d), F32),
            jax.ShapeDtypeStruct((batch, GLA_QK, GLA_DV), F32),
            jax.ShapeDtypeStruct((batch, M_HEADS, M_DH, M_DH), F32),
            jax.ShapeDtypeStruct((batch, M_HEADS, 1, M_DH), F32),
            jax.ShapeDtypeStruct((batch, M_HEADS, 1, LANES), F32),
            jax.ShapeDtypeStruct((batch, FFN_CONV - 1, d_ff), F32),
        ],
        scratch_shapes=[
            pltpu.VMEM((streams * c, P_COLS), F32),
            pltpu.VMEM((2, streams * c, D_MIX), F32),
            pltpu.VMEM((streams, GLA_DV, GLA_QK), F32),
            pltpu.VMEM((streams, M_HEADS, M_DH, 2 * M_DH), F32),
            pltpu.VMEM((streams, M_HEADS, 1, LANES), F32),
            pltpu.VMEM((streams, FFN_CONV - 1, d_ff), F32),
        ] + w_scratch,
        compiler_params=pltpu.CompilerParams(
            dimension_semantics=("arbitrary",), vmem_limit_bytes=VMEM_LIMIT),
        name="layer",
    )(x, x, *consts, *state_ops)
    y, s_new, c_new, n_new, m_e, conv_new = outs
    return (y, s_new.reshape(batch, GLA_HEADS, GLA_DK, GLA_DV), c_new,
            n_new.reshape(batch, M_HEADS, M_DH), m_e[:, :, 0, 0], conv_new)


def _regroup_kernel(wg_ref, wm_ref, glr_ref, mif_ref, o_ref, osm_ref, *, n_gla):
    j = pl.program_id(0)

    @pl.when(j < n_gla)
    def _():
        o_ref[...] = wg_ref[...].T.astype(BF16)

    @pl.when(j >= n_gla)
    def _():
        o_ref[...] = wm_ref[...].T.astype(BF16)

    @pl.when(j == 0)
    def _():
        d = wg_ref.shape[1]
        small = jnp.concatenate(
            [glr_ref[...], mif_ref[...], jnp.zeros((LANES - GLA_RANK - 2 * M_HEADS, d), F32)], axis=0)
        osm_ref[...] = small.T.astype(BF16)


def _regroup_w_in(w):
    d = w.shape[0]
    wt = w.T
    o_glr = OFF_MQ
    o_m = o_glr + GLA_RANK
    o_if = o_m + 4 * M_W
    nslab = OFF_SMALL // INPROJ_SLAB
    n_gla = OFF_MQ // INPROJ_SLAB
    rows = lambda n: (pl.Element(n), pl.Element(d))
    return pl.pallas_call(
        functools.partial(_regroup_kernel, n_gla=n_gla),
        grid=(nslab,),
        in_specs=[
            pl.BlockSpec(rows(INPROJ_SLAB), lambda j: (jnp.minimum(j, n_gla - 1) * INPROJ_SLAB, 0)),
            pl.BlockSpec(rows(INPROJ_SLAB), lambda j: (
                (o_m // GLA_RANK + jnp.maximum(j - n_gla, 0) * (INPROJ_SLAB // GLA_RANK)) * GLA_RANK, 0)),
            pl.BlockSpec(rows(GLA_RANK), lambda j: (o_glr, 0)),
            pl.BlockSpec(rows(2 * M_HEADS), lambda j: (o_if, 0)),
        ],
        out_specs=[
            pl.BlockSpec((d, INPROJ_SLAB), lambda j: (0, j)),
            pl.BlockSpec((d, LANES), lambda j: (0, 0)),
        ],
        out_shape=[
            jax.ShapeDtypeStruct((d, OFF_SMALL), BF16),
            jax.ShapeDtypeStruct((d, LANES), BF16),
        ],
        compiler_params=pltpu.CompilerParams(dimension_semantics=("arbitrary",)),
        name="regroup_w_in",
    )(wt, wt, wt, wt)


def kernel(x_prompt, x_sample, state_gla, state_mlstm_C, state_mlstm_n, state_mlstm_m, cache_ffn_conv,
           norm1, w_in, w_gla_a, b_gla_a, gla_norm, b_m_i, b_m_f, m_norm, w_out, norm2, w_g, w_u,
           conv_w, conv_b, w_d, norm_f):
    assert w_in.shape[0] == 1, "single-layer stack"
    w_in_r, w_in_sm = _regroup_w_in(w_in[0])
    wa_pad = jnp.concatenate(
        [w_gla_a[0], jnp.zeros((LANES - GLA_RANK, GLA_QK), w_gla_a.dtype)], axis=0).astype(BF16)
    b_small = jnp.concatenate(
        [jnp.zeros((GLA_RANK,), F32), b_m_i[0], b_m_f[0],
         jnp.zeros((LANES - GLA_RANK - 2 * M_HEADS,), F32)])[None]
    wts = dict(
        norm1=norm1, w_in_r=w_in_r, w_in_sm=w_in_sm, wa_pad=wa_pad, b_gla_a=b_gla_a, gla_norm=gla_norm,
        b_small=b_small, m_norm=m_norm, w_out=w_out[0].astype(BF16), norm2=norm2,
        w_g=w_g[0].astype(BF16), w_u=w_u[0].astype(BF16), conv_w=conv_w[0], conv_b=conv_b,
        w_d=w_d[0].astype(BF16), norm_f=norm_f[None])

    outs_p = _layer(x_prompt, None, wts)
    outs_s = _layer(x_sample, (state_gla[0], state_mlstm_C[0], state_mlstm_n[0], state_mlstm_m[0],
                               cache_ffn_conv[0]), wts)
    y_p, *st_p = outs_p
    y_s, *st_s = outs_s
    return (y_p, y_s) + tuple(s[None] for s in st_p) + tuple(s[None] for s in st_s)
```

```python
import functools

import numpy as np
import jax
import jax.numpy as jnp
from jax import lax
from jax.experimental import pallas as pl
from jax.experimental.pallas import tpu as pltpu

F32 = jnp.float32
BF16 = jnp.bfloat16

EPS = 1e-6
CHUNK = 64
GLA_HEADS = 4
GLA_DK = 64
GLA_DV = 128
GLA_RANK = 16
GLA_TAU = 16.0
M_HEADS = 4
M_DH = 128
FFN_CONV = 3
LANES = 128
SUBLANES = 8
MXU_ROWS = 256

GLA_QK = GLA_HEADS * GLA_DK
GLA_V = GLA_HEADS * GLA_DV
M_W = M_HEADS * M_DH
D_MIX = GLA_V + M_W
OFF_GQ = 0
OFF_GK = OFF_GQ + GLA_QK
OFF_GV = OFF_GK + GLA_QK
OFF_GR = OFF_GV + GLA_V
OFF_MQ = OFF_GR + GLA_V
OFF_MK = OFF_MQ + M_W
OFF_MV = OFF_MK + M_W
OFF_MO = OFF_MV + M_W
OFF_SMALL = OFF_MO + M_W
P_COLS = OFF_SMALL + LANES
SM_I = GLA_RANK
SM_F = GLA_RANK + M_HEADS

INPROJ_SLAB = 512
FFN_SLAB = 256
DOWN_SLAB = 256
VMEM_LIMIT = 60 * 1024 * 1024
N_WEIGHTS = 20


def _dot(a, b):
    return jnp.dot(a, b, preferred_element_type=F32)


def _dot_nt(a, b):
    return lax.dot_general(a, b, (((1,), (1,)), ((), ())), preferred_element_type=F32)


def _dot_tn(a, b):
    return lax.dot_general(a, b, (((0,), (0,)), ((), ())), preferred_element_type=F32)


def _split_hilo(x):
    hi = x.astype(BF16)
    lo = (x - hi.astype(F32)).astype(BF16)
    return hi, lo


def _log_sigmoid(x):
    return jnp.minimum(x, 0.0) - jnp.log1p(jnp.exp(-jnp.abs(x)))


def _rms(x, g):
    return x * lax.rsqrt(jnp.mean(x * x, axis=-1, keepdims=True) + EPS) * g


def _interleave(tasks, strides):
    alive = list(zip(tasks, strides))
    while alive:
        for item in list(alive):
            task, stride = item
            for _ in range(stride):
                try:
                    next(task)
                except StopIteration:
                    alive.remove(item)
                    break


def _mixer_consts(c):
    nlev = int(np.log2(c))
    assert 2 ** nlev == c
    t = np.arange(c)[:, None]
    u = np.arange(c)[None, :]
    blocks = [(u <= t), (u > t)]
    for l in range(1, nlev + 1):
        bsz, half = 2 ** l, 2 ** (l - 1)
        mid = (t // bsz) * bsz + half
        right = (t % bsz) >= half
        blocks.append(np.where(right, (u >= mid) & (u <= t), (u > t) & (u < mid)))
    a = np.concatenate(blocks, axis=0).astype(np.float32)
    a2 = np.concatenate([a, a], axis=1)
    x = np.bitwise_xor(t, u)
    lv = np.where(u > t, -1, np.where(x == 0, 0, np.floor(np.log2(np.maximum(x, 1))) + 1))
    e = np.zeros((LANES, 2 * M_HEADS * LANES), np.float32)
    for h in range(M_HEADS):
        e[SM_I + h, h * LANES:(h + 1) * LANES] = 1.0
        e[SM_F + h, (M_HEADS + h) * LANES:(M_HEADS + h + 1) * LANES] = 1.0
    e2 = np.concatenate([e, e], axis=0)
    lv4 = np.tile(lv, (1, GLA_HEADS))
    bd = (np.arange(GLA_HEADS * c)[:, None] // c == np.arange(GLA_QK)[None, :] // GLA_DK)
    return (jnp.asarray(a2, BF16), jnp.asarray(lv4, jnp.int32), jnp.asarray(e2, BF16),
            jnp.asarray(bd, BF16), nlev)


def _inproj(xf_ref, n1_ref, win_ref, wsm_ref, p_sc, ready, *, c, streams):
    rows = streams * c
    d = xf_ref.shape[-1]
    h_in = _rms(xf_ref[...].reshape(rows, d), n1_ref[...]).astype(BF16)
    for lo in range(0, OFF_SMALL, INPROJ_SLAB):
        hi = lo + INPROJ_SLAB
        p_sc[:, lo:hi] = _dot(h_in, win_ref[:, lo:hi])
        ready.append((lo, hi))
        yield
    p_sc[:, OFF_SMALL:] = _dot(h_in, wsm_ref[...])
    ready.append((OFF_SMALL, P_COLS))
    yield


def _p_reader(p_sc, ready, rs):
    def pcol(g, off, width):
        assert any(lo <= off and off + width <= hi for lo, hi in ready), (off, width)
        return p_sc[rs[g], off:off + width]
    return pcol


def _gla(wa_ref, ba_ref, gn_ref, a2_ref, lv_ref, bd_ref, p_sc, ready, mix_sc, slot, s_sc, *, c, nlev,
         streams):
    gs = range(streams)
    rs = [slice(g * c, (g + 1) * c) for g in gs]
    pcol = _p_reader(p_sc, ready, rs)
    assert (OFF_SMALL, P_COLS) in ready
    small = p_sc[:, OFF_SMALL:OFF_SMALL + LANES]
    xa = _dot(small.astype(BF16), wa_ref[...]) + ba_ref[...]
    la_hi, la_lo = _split_hilo(_log_sigmoid(xa) * (1.0 / GLA_TAU))
    yield
    z = [_dot(a2_ref[...], jnp.concatenate([la_hi[rs[g]], la_lo[rs[g]]], axis=0))
         for g in gs]
    yield
    ez = [jnp.exp(z[g]) for g in gs]
    yield
    q = [pcol(g, OFF_GQ, GLA_QK) * (GLA_DK ** -0.5) for g in gs]
    k = [pcol(g, OFF_GK, GLA_QK) for g in gs]
    q_in = [(q[g] * ez[g][0:c]).astype(BF16) for g in gs]
    k_dec = [(k[g] * ez[g][c:2 * c]).astype(BF16) for g in gs]
    dec = [ez[g][c - 1:c, :] for g in gs]
    q_lv = [[q[g].astype(BF16)] + [(q[g] * ez[g][(2 + l) * c:(3 + l) * c]).astype(BF16)
                                   for l in range(nlev)] for g in gs]
    k_lv = [[k[g].astype(BF16)] + [(k[g] * ez[g][(2 + l) * c:(3 + l) * c]).astype(BF16)
                                   for l in range(nlev)] for g in gs]
    yield
    lv = lv_ref[...]
    bd = bd_ref[...]
    sc = [jnp.zeros((c, GLA_HEADS * c), F32) for g in gs]
    for l in range(nlev + 1):
        for g in gs:
            k_bd = jnp.concatenate([k_lv[g][l]] * GLA_HEADS, axis=0) * bd
            sc[g] = jnp.where(lv == l, _dot_nt(q_lv[g][l], k_bd), sc[g])
        yield
    sc = [sc[g].astype(BF16) for g in gs]
    for h in range(GLA_HEADS):
        sl = slice(h * GLA_DK, (h + 1) * GLA_DK)
        for g in gs:
            vb = pcol(g, OFF_GV + h * GLA_DV, GLA_DV).astype(BF16)
            st = s_sc[g, :, sl]
            o = _dot_nt(q_in[g][:, sl], st.astype(BF16)) + _dot(sc[g][:, h * c:(h + 1) * c], vb)
            s_sc[g, :, sl] = dec[g][:, sl] * st + _dot_tn(vb, k_dec[g][:, sl])
            r = pcol(g, OFF_GR + h * GLA_DV, GLA_DV)
            mix_sc[slot, rs[g], h * GLA_DV:(h + 1) * GLA_DV] = (
                _rms(o, gn_ref[...]) * (r * jax.nn.sigmoid(r)))
        yield


def _mlstm(bsm_ref, mn_ref, a2_ref, e2_ref, p_sc, ready, mix_sc, slot, c_sc, m_sc, *, c, streams):
    rows = streams * c
    gs = range(streams)
    rs = [slice(g * c, (g + 1) * c) for g in gs]
    pcol = _p_reader(p_sc, ready, rs)
    assert (OFF_SMALL, P_COLS) in ready
    lane = lax.broadcasted_iota(jnp.int32, (c, LANES), 1)
    row = lax.broadcasted_iota(jnp.int32, (c, LANES), 0)
    on_diag = row == lane
    causal = lane <= row
    ones = jnp.ones((c, LANES), F32)
    lane_all = lax.broadcasted_iota(jnp.int32, (rows, LANES), 1)
    is_f = (lane_all >= SM_F) & (lane_all < SM_F + M_HEADS)

    x = p_sc[:, OFF_SMALL:OFF_SMALL + LANES] + bsm_ref[...]
    x = jnp.where(is_f, _log_sigmoid(x), x)
    x_hi, x_lo = _split_hilo(x)
    yield
    xcum = jnp.concatenate(
        [_dot(a2_ref[0:c, :], jnp.concatenate([x_hi[rs[g]], x_lo[rs[g]]], axis=0)) for g in gs],
        axis=0)
    xc = jnp.where(is_f, xcum, x)
    yield
    xe_all = _dot(jnp.concatenate(_split_hilo(xc), axis=1), e2_ref[...])
    xe = [xe_all[rs[g]] for g in gs]
    yield
    for h in range(M_HEADS):
        ie = [xe[g][:, h * LANES:(h + 1) * LANES] for g in gs]
        fe = [xe[g][:, (M_HEADS + h) * LANES:(M_HEADS + h + 1) * LANES] for g in gs]
        a = [ie[g] - fe[g] for g in gs]
        a_row = [jnp.sum(jnp.where(on_diag, a[g], 0.0), axis=0, keepdims=True) for g in gs]
        dm = [jnp.where(causal, fe[g] + a_row[g], -jnp.inf)[:, :c] for g in gs]
        gi = [fe[g] + m_sc[g, h] for g in gs]
        m_t = [jnp.maximum(gi[g], jnp.max(dm[g], axis=-1, keepdims=True)) for g in gs]
        w_int = [jnp.exp(gi[g] - m_t[g]) for g in gs]
        w = [jnp.exp(dm[g] - m_t[g][:, :c]) for g in gs]
        yield
        qh = [pcol(g, OFF_MQ + h * M_DH, M_DH).astype(BF16) for g in gs]
        kf = [pcol(g, OFF_MK + h * M_DH, M_DH) * (M_DH ** -0.5) for g in gs]
        vaug = [jnp.concatenate([pcol(g, OFF_MV + h * M_DH, M_DH), ones], axis=1).astype(BF16)
                for g in gs]
        qk = [(_dot_nt(qh[g], kf[g].astype(BF16)) * w[g]).astype(BF16) for g in gs]
        yield
        caug = [c_sc[g, h] for g in gs]
        nd = [_dot(qk[g], vaug[g])
              + jnp.concatenate([w_int[g], w_int[g]], axis=1) * _dot(qh[g], caug[g].astype(BF16))
              for g in gs]
        for g in gs:
            den = jnp.maximum(jnp.abs(nd[g][:, M_DH:]), jnp.exp(-m_t[g]))
            og = jax.nn.sigmoid(pcol(g, OFF_MO + h * M_DH, M_DH)) * (nd[g][:, :M_DH] / den)
            mix_sc[slot, rs[g], GLA_V + h * M_DH:GLA_V + (h + 1) * M_DH] = _rms(og, mn_ref[...])
        yield
        for g in gs:
            m_c = m_t[g][c - 1:c, :]
            decay = jnp.exp(gi[g][c - 1:c, :] - m_c)
            ws = jnp.exp(a[g] + (fe[g][c - 1:c, :] - m_c))
            kw = (kf[g] * ws).astype(BF16)
            c_sc[g, h] = jnp.concatenate([decay, decay], axis=1) * caug[g] + _dot_tn(kw, vaug[g])
            m_sc[g, h] = m_c
        yield


def _back(xb_ref, mix_sc, slot_prev, wo_ref, n2_ref, wg_ref, wu_ref, cw_ref, cb_ref, wd_ref, nf_ref,
          y_ref, carry_sc, *, c, streams):
    rows = streams * c
    d = xb_ref.shape[-1]
    d_ff = wg_ref.shape[1]
    nslab = d_ff // FFN_SLAB
    x1 = xb_ref[...].reshape(rows, d) + _dot(mix_sc[slot_prev].astype(BF16), wo_ref[...])
    h2 = _rms(x1, n2_ref[...]).astype(BF16)
    yield

    def gate_up(s):
        cs = slice(s * FFN_SLAB, (s + 1) * FFN_SLAB)
        return _dot(h2, wg_ref[:, cs]), _dot(h2, wu_ref[:, cs])

    row8 = lax.broadcasted_iota(jnp.int32, (SUBLANES, FFN_SLAB), 0)
    nxt = gate_up(0)
    acts = []
    yield
    for s in range(nslab):
        cs = slice(s * FFN_SLAB, (s + 1) * FFN_SLAB)
        g, u = nxt
        if s + 1 < nslab:
            nxt = gate_up(s + 1)
        yield
        r1 = pltpu.roll(g, 1, axis=0)
        r2 = pltpu.roll(g, 2, axis=0)
        g1, g2 = [], []
        for st in range(streams):
            b0 = st * c
            p0 = carry_sc[st, 0:1, cs]
            p1 = carry_sc[st, 1:2, cs]
            g1 += [jnp.where(row8 == 0, p1, r1[b0:b0 + SUBLANES]), r1[b0 + SUBLANES:b0 + c]]
            g2 += [jnp.where(row8 == 0, p0, jnp.where(row8 == 1, p1, r2[b0:b0 + SUBLANES])),
                   r2[b0 + SUBLANES:b0 + c]]
            carry_sc[st, 0:1, cs] = g[b0 + c - 2:b0 + c - 1, :]
            carry_sc[st, 1:2, cs] = g[b0 + c - 1:b0 + c, :]
        g1 = jnp.concatenate(g1, axis=0)
        g2 = jnp.concatenate(g2, axis=0)
        gc = cb_ref[:, cs] + (cw_ref[0:1, cs] * g2 + cw_ref[1:2, cs] * g1 + cw_ref[2:3, cs] * g)
        yield
        acts.append((jax.nn.gelu(gc) * u).astype(BF16))
        yield
    act = jnp.concatenate(acts, axis=1)
    down = []
    for lo in range(0, d, DOWN_SLAB):
        down.append(_dot(act, wd_ref[:, lo:lo + DOWN_SLAB]))
        yield
    y_ref[...] = _rms(x1 + jnp.concatenate(down, axis=1), nf_ref[...]).reshape(streams, c, d)


def _layer_kernel(*refs, c, nlev, streams, nj, nsteps, skew, zero_state):
    n_in = 2 + N_WEIGHTS + (0 if zero_state else 5)
    (xf_ref, xb_ref, n1_ref, win_ref, wsm_ref, wa_ref, ba_ref, gn_ref, bsm_ref, mn_ref, a2_ref, lv_ref,
     e2_ref, bd_ref, wo_ref, n2_ref, wg_ref, wu_ref, cw_ref, cb_ref, wd_ref, nf_ref) = refs[:2 + N_WEIGHTS]
    s0_ref, c0_ref, n0_ref, m0_ref, cst_ref = (None,) * 5 if zero_state else refs[2 + N_WEIGHTS:n_in]
    y_ref, s_out, c_out, n_out, m_out, cv_out = refs[n_in:n_in + 6]
    p_sc, mix_sc, s_sc, c_sc, m_sc, carry_sc = refs[n_in + 6:n_in + 12]
    w_copies = []
    if not skew:
        w_bufs, w_sem = refs[n_in + 12:n_in + 18], refs[n_in + 18]
        w_hbm = (win_ref, wsm_ref, wo_ref, wg_ref, wu_ref, wd_ref)
        w_copies = [pltpu.make_async_copy(h, v, w_sem.at[i]) for i, (h, v) in enumerate(zip(w_hbm, w_bufs))]
        for cp in w_copies[:2]:
            cp.start()
        win_ref, wsm_ref, wo_ref, wg_ref, wu_ref, wd_ref = w_bufs

    t = pl.program_id(0)
    tf = jnp.minimum(t, nsteps - 1)
    jf = lax.rem(tf, nj)
    tb = jnp.maximum(t - skew, 0)
    jb = lax.rem(tb, nj)
    slot = lax.rem(t, 2)
    slot_back = 1 - slot if skew else slot

    if skew:
        @pl.when(t == 0)
        def _():
            mix_sc[1] = jnp.zeros(mix_sc.shape[1:], F32)

    @pl.when(jf == 0)
    def _():
        if zero_state:
            s_sc[...] = jnp.zeros(s_sc.shape, F32)
            c_sc[...] = jnp.zeros(c_sc.shape, F32)
            m_sc[...] = jnp.zeros(m_sc.shape, F32)
        else:
            for g in range(streams):
                for lo in range(0, GLA_QK, LANES):
                    s_sc[g, :, lo:lo + LANES] = s0_ref[g, lo:lo + LANES, :].T
            c_sc[:, :, :, :M_DH] = c0_ref[...]
            c_sc[:, :, :, M_DH:] = jnp.broadcast_to(n0_ref[...], c0_ref.shape)
            m_sc[...] = m0_ref[...]

    @pl.when(jb == 0)
    def _():
        carry_sc[...] = jnp.zeros(carry_sc.shape, F32) if zero_state else cst_ref[...]

    back = _back(xb_ref, mix_sc, slot_back, wo_ref, n2_ref, wg_ref, wu_ref, cw_ref, cb_ref, wd_ref,
                 nf_ref, y_ref, carry_sc, c=c, streams=streams)
    if skew:
        next(back)
    ready = []
    for cp in w_copies[:2]:
        cp.wait()
    for cp in w_copies[2:]:
        cp.start()
    for _ in _inproj(xf_ref, n1_ref, win_ref, wsm_ref, p_sc, ready, c=c, streams=streams):
        pass
    mixers = [_gla(wa_ref, ba_ref, gn_ref, a2_ref, lv_ref, bd_ref, p_sc, ready, mix_sc, slot, s_sc,
                   c=c, nlev=nlev, streams=streams),
              _mlstm(bsm_ref, mn_ref, a2_ref, e2_ref, p_sc, ready, mix_sc, slot, c_sc, m_sc,
                     c=c, streams=streams)]
    if skew:
        _interleave(mixers + [back], strides=(1, 1, 3))
    else:
        _interleave(mixers, strides=(1, 1))
        for cp in w_copies[2:]:
            cp.wait()
        _interleave([back], strides=(1,))

    @pl.when((jf == nj - 1) & (t < nsteps))
    def _():
        for g in range(streams):
            for lo in range(0, GLA_QK, LANES):
                s_out[g, lo:lo + LANES, :] = s_sc[g, :, lo:lo + LANES].T
        c_out[...] = c_sc[:, :, :, :M_DH]
        on_diag = (lax.broadcasted_iota(jnp.int32, (M_DH, M_DH), 0)
                   == lax.broadcasted_iota(jnp.int32, (M_DH, M_DH), 1))
        for g in range(streams):
            for h in range(M_HEADS):
                n_out[g, h] = jnp.sum(jnp.where(on_diag, c_sc[g, h, :, M_DH:], 0.0), axis=0, keepdims=True)
        m_out[...] = m_sc[...]

    @pl.when((jb == nj - 1) & (t >= skew))
    def _():
        cv_out[...] = carry_sc[...]


def _layer(x, states, wts):
    batch, length, d = x.shape
    c = min(CHUNK, length)
    nj = length // c
    streams = min(batch, MXU_ROWS // c)
    assert batch % streams == 0 and length % c == 0
    nsteps = (batch // streams) * nj
    d_ff = wts["w_g"].shape[1]
    a2, lv, e2, bd, nlev = _mixer_consts(c)
    skew = 1 if nsteps > 1 else 0

    def front_idx(t):
        tf = jnp.minimum(t, nsteps - 1)
        return tf // nj, tf % nj

    def back_idx(t):
        tb = jnp.maximum(t - skew, 0)
        return tb // nj, tb % nj

    once = pl.Buffered(1)
    const = lambda a: pl.BlockSpec(a.shape, lambda t: (0,) * a.ndim, pipeline_mode=once)
    fstate = lambda *blk, **kw: pl.BlockSpec(
        (streams,) + blk, lambda t: (front_idx(t)[0],) + (0,) * len(blk), **kw)
    bstate = lambda *blk, **kw: pl.BlockSpec(
        (streams,) + blk, lambda t: (back_idx(t)[0],) + (0,) * len(blk), **kw)
    consts = [wts["norm1"], wts["w_in_r"], wts["w_in_sm"], wts["wa_pad"], wts["b_gla_a"], wts["gla_norm"],
              wts["b_small"], wts["m_norm"], a2, lv, e2, bd, wts["w_out"], wts["norm2"], wts["w_g"],
              wts["w_u"], wts["conv_w"], wts["conv_b"], wts["w_d"], wts["norm_f"]]
    assert len(consts) == N_WEIGHTS
    const_specs = [const(a) for a in consts]
    w_scratch = []
    if not skew:
        big = [wts[k] for k in ("w_in_r", "w_in_sm", "w_out", "w_g", "w_u", "w_d")]
        for i, a in enumerate(consts):
            if any(a is b for b in big):
                const_specs[i] = pl.BlockSpec(memory_space=pl.ANY)
        w_scratch = [pltpu.VMEM(a.shape, a.dtype) for a in big] + [pltpu.SemaphoreType.DMA((len(big),))]
    state_ops, state_specs = [], []
    if states is not None:
        s_gla, c_m, n_m, m_m, conv_st = states
        state_ops = [
            s_gla.reshape(batch, GLA_QK, GLA_DV),
            c_m, n_m[..., None],
            jnp.broadcast_to(m_m[..., None, None], (batch, M_HEADS, 1, LANES)), conv_st]
        state_specs = [
            fstate(GLA_QK, GLA_DV, pipeline_mode=once),
            fstate(M_HEADS, M_DH, M_DH, pipeline_mode=once),
            fstate(M_HEADS, M_DH, 1, pipeline_mode=once),
            fstate(M_HEADS, 1, LANES, pipeline_mode=once),
            bstate(FFN_CONV - 1, d_ff, pipeline_mode=once)]
    outs = pl.pallas_call(
        functools.partial(_layer_kernel, c=c, nlev=nlev, streams=streams, nj=nj, nsteps=nsteps,
                          skew=skew, zero_state=states is None),
        grid=(nsteps + skew,),
        in_specs=[
            pl.BlockSpec((streams, c, d), lambda t: front_idx(t) + (0,)),
            pl.BlockSpec((streams, c, d), lambda t: back_idx(t) + (0,)),
        ] + const_specs + state_specs,
        out_specs=[
            pl.BlockSpec((streams, c, d), lambda t: back_idx(t) + (0,)),
            fstate(GLA_QK, GLA_DV), fstate(M_HEADS, M_DH, M_DH), fstate(M_HEADS, 1, M_DH),
            fstate(M_HEADS, 1, LANES), bstate(FFN_CONV - 1, d_ff),
        ],
        out_shape=[
            jax.ShapeDtypeStruct((batch, length, d), F32),
            jax.ShapeDtypeStruct((batch, GLA_QK, GLA_DV), F32),
            jax.ShapeDtypeStruct((batch, M_HEADS, M_DH, M_DH), F32),
            jax.ShapeDtypeStruct((batch, M_HEADS, 1, M_DH), F32),
            jax.ShapeDtypeStruct((batch, M_HEADS, 1, LANES), F32),
            jax.ShapeDtypeStruct((batch, FFN_CONV - 1, d_ff), F32),
        ],
        scratch_shapes=[
            pltpu.VMEM((streams * c, P_COLS), F32),
            pltpu.VMEM((2, streams * c, D_MIX), F32),
            pltpu.VMEM((streams, GLA_DV, GLA_QK), F32),
            pltpu.VMEM((streams, M_HEADS, M_DH, 2 * M_DH), F32),
            pltpu.VMEM((streams, M_HEADS, 1, LANES), F32),
            pltpu.VMEM((streams, FFN_CONV - 1, d_ff), F32),
        ] + w_scratch,
        compiler_params=pltpu.CompilerParams(
            dimension_semantics=("arbitrary",), vmem_limit_bytes=VMEM_LIMIT),
        name="layer",
    )(x, x, *consts, *state_ops)
    y, s_new, c_new, n_new, m_e, conv_new = outs
    return (y, s_new.reshape(batch, GLA_HEADS, GLA_DK, GLA_DV), c_new,
            n_new.reshape(batch, M_HEADS, M_DH), m_e[:, :, 0, 0], conv_new)


def _regroup_kernel(wg_ref, wm_ref, glr_ref, mif_ref, o_ref, osm_ref, *, n_gla):
    j = pl.program_id(0)

    @pl.when(j < n_gla)
    def _():
        o_ref[...] = wg_ref[...].T.astype(BF16)

    @pl.when(j >= n_gla)
    def _():
        o_ref[...] = wm_ref[...].T.astype(BF16)

    @pl.when(j == 0)
    def _():
        d = wg_ref.shape[1]
        small = jnp.concatenate(
            [glr_ref[...], mif_ref[...], jnp.zeros((LANES - GLA_RANK - 2 * M_HEADS, d), F32)], axis=0)
        osm_ref[...] = small.T.astype(BF16)


def _regroup_w_in(w):
    d = w.shape[0]
    wt = w.T
    o_glr = OFF_MQ
    o_m = o_glr + GLA_RANK
    o_if = o_m + 4 * M_W
    nslab = OFF_SMALL // INPROJ_SLAB
    n_gla = OFF_MQ // INPROJ_SLAB
    rows = lambda n: (pl.Element(n), pl.Element(d))
    return pl.pallas_call(
        functools.partial(_regroup_kernel, n_gla=n_gla),
        grid=(nslab,),
        in_specs=[
            pl.BlockSpec(rows(INPROJ_SLAB), lambda j: (jnp.minimum(j, n_gla - 1) * INPROJ_SLAB, 0)),
            pl.BlockSpec(rows(INPROJ_SLAB), lambda j: (
                (o_m // GLA_RANK + jnp.maximum(j - n_gla, 0) * (INPROJ_SLAB // GLA_RANK)) * GLA_RANK, 0)),
            pl.BlockSpec(rows(GLA_RANK), lambda j: (o_glr, 0)),
            pl.BlockSpec(rows(2 * M_HEADS), lambda j: (o_if, 0)),
        ],
        out_specs=[
            pl.BlockSpec((d, INPROJ_SLAB), lambda j: (0, j)),
            pl.BlockSpec((d, LANES), lambda j: (0, 0)),
        ],
        out_shape=[
            jax.ShapeDtypeStruct((d, OFF_SMALL), BF16),
            jax.ShapeDtypeStruct((d, LANES), BF16),
        ],
        compiler_params=pltpu.CompilerParams(dimension_semantics=("arbitrary",)),
        name="regroup_w_in",
    )(wt, wt, wt, wt)


def kernel(x_prompt, x_sample, state_gla, state_mlstm_C, state_mlstm_n, state_mlstm_m, cache_ffn_conv,
           norm1, w_in, w_gla_a, b_gla_a, gla_norm, b_m_i, b_m_f, m_norm, w_out, norm2, w_g, w_u,
           conv_w, conv_b, w_d, norm_f):
    assert w_in.shape[0] == 1, "single-layer stack"
    w_in_r, w_in_sm = _regroup_w_in(w_in[0])
    wa_pad = jnp.concatenate(
        [w_gla_a[0], jnp.zeros((LANES - GLA_RANK, GLA_QK), w_gla_a.dtype)], axis=0).astype(BF16)
    b_small = jnp.concatenate(
        [jnp.zeros((GLA_RANK,), F32), b_m_i[0], b_m_f[0],
         jnp.zeros((LANES - GLA_RANK - 2 * M_HEADS,), F32)])[None]
    wts = dict(
        norm1=norm1, w_in_r=w_in_r, w_in_sm=w_in_sm, wa_pad=wa_pad, b_gla_a=b_gla_a, gla_norm=gla_norm,
        b_small=b_small, m_norm=m_norm, w_out=w_out[0].astype(BF16), norm2=norm2,
        w_g=w_g[0].astype(BF16), w_u=w_u[0].astype(BF16), conv_w=conv_w[0], conv_b=conv_b,
        w_d=w_d[0].astype(BF16), norm_f=norm_f[None])

    outs_p = _layer(x_prompt, None, wts)
    outs_s = _layer(x_sample, (state_gla[0], state_mlstm_C[0], state_mlstm_n[0], state_mlstm_m[0],
                               cache_ffn_conv[0]), wts)
    y_p, *st_p = outs_p
    y_s, *st_s = outs_s
    return (y_p, y_s) + tuple(s[None] for s in st_p) + tuple(s[None] for s in st_s)
```

```python
import functools

import numpy as np
import jax
import jax.numpy as jnp
from jax import lax
from jax.experimental import pallas as pl
from jax.experimental.pallas import tpu as pltpu

F32 = jnp.float32
BF16 = jnp.bfloat16

EPS = 1e-6
CHUNK = 64
GLA_HEADS = 4
GLA_DK = 64
GLA_DV = 128
GLA_RANK = 16
GLA_TAU = 16.0
M_HEADS = 4
M_DH = 128
FFN_CONV = 3
LANES = 128
SUBLANES = 8
MXU_ROWS = 256

GLA_QK = GLA_HEADS * GLA_DK
GLA_V = GLA_HEADS * GLA_DV
M_W = M_HEADS * M_DH
D_MIX = GLA_V + M_W
OFF_GQ = 0
OFF_GK = OFF_GQ + GLA_QK
OFF_GV = OFF_GK + GLA_QK
OFF_GR = OFF_GV + GLA_V
OFF_MQ = OFF_GR + GLA_V
OFF_MK = OFF_MQ + M_W
OFF_MV = OFF_MK + M_W
OFF_MO = OFF_MV + M_W
OFF_SMALL = OFF_MO + M_W
P_COLS = OFF_SMALL + LANES
SM_I = GLA_RANK
SM_F = GLA_RANK + M_HEADS

INPROJ_SLAB = 512
FFN_SLAB = 256
DOWN_SLAB = 256
VMEM_LIMIT = 60 * 1024 * 1024
N_WEIGHTS = 20


def _dot(a, b):
    return jnp.dot(a, b, preferred_element_type=F32)


def _dot_nt(a, b):
    return lax.dot_general(a, b, (((1,), (1,)), ((), ())), preferred_element_type=F32)


def _dot_tn(a, b):
    return lax.dot_general(a, b, (((0,), (0,)), ((), ())), preferred_element_type=F32)


def _split_hilo(x):
    hi = x.astype(BF16)
    lo = (x - hi.astype(F32)).astype(BF16)
    return hi, lo


def _log_sigmoid(x):
    return jnp.minimum(x, 0.0) - jnp.log1p(jnp.exp(-jnp.abs(x)))


def _rms(x, g):
    return x * lax.rsqrt(jnp.mean(x * x, axis=-1, keepdims=True) + EPS) * g


def _interleave(tasks, strides):
    alive = list(zip(tasks, strides))
    while alive:
        for item in list(alive):
            task, stride = item
            for _ in range(stride):
                try:
                    next(task)
                except StopIteration:
                    alive.remove(item)
                    break


def _mixer_consts(c):
    nlev = int(np.log2(c))
    assert 2 ** nlev == c
    t = np.arange(c)[:, None]
    u = np.arange(c)[None, :]
    blocks = [(u <= t), (u > t)]
    for l in range(1, nlev + 1):
        bsz, half = 2 ** l, 2 ** (l - 1)
        mid = (t // bsz) * bsz + half
        right = (t % bsz) >= half
        blocks.append(np.where(right, (u >= mid) & (u <= t), (u > t) & (u < mid)))
    a = np.concatenate(blocks, axis=0).astype(np.float32)
    a2 = np.concatenate([a, a], axis=1)
    x = np.bitwise_xor(t, u)
    lv = np.where(u > t, -1, np.where(x == 0, 0, np.floor(np.log2(np.maximum(x, 1))) + 1))
    e = np.zeros((LANES, 2 * M_HEADS * LANES), np.float32)
    for h in range(M_HEADS):
        e[SM_I + h, h * LANES:(h + 1) * LANES] = 1.0
        e[SM_F + h, (M_HEADS + h) * LANES:(M_HEADS + h + 1) * LANES] = 1.0
    e2 = np.concatenate([e, e], axis=0)
    lv4 = np.tile(lv, (1, GLA_HEADS))
    bd = (np.arange(GLA_HEADS * c)[:, None] // c == np.arange(GLA_QK)[None, :] // GLA_DK)
    return (jnp.asarray(a2, BF16), jnp.asarray(lv4, jnp.int32), jnp.asarray(e2, BF16),
            jnp.asarray(bd, BF16), nlev)


def _inproj(xf_ref, n1_ref, win_ref, wsm_ref, p_sc, ready, *, c, streams):
    rows = streams * c
    d = xf_ref.shape[-1]
    h_in = _rms(xf_ref[...].reshape(rows, d), n1_ref[...]).astype(BF16)
    for lo in range(0, OFF_SMALL, INPROJ_SLAB):
        hi = lo + INPROJ_SLAB
        p_sc[:, lo:hi] = _dot(h_in, win_ref[:, lo:hi])
        ready.append((lo, hi))
        yield
    p_sc[:, OFF_SMALL:] = _dot(h_in, wsm_ref[...])
    ready.append((OFF_SMALL, P_COLS))
    yield


def _p_reader(p_sc, ready, rs):
    def pcol(g, off, width):
        assert any(lo <= off and off + width <= hi for lo, hi in ready), (off, width)
        return p_sc[rs[g], off:off + width]
    return pcol


def _gla(wa_ref, ba_ref, gn_ref, a2_ref, lv_ref, bd_ref, p_sc, ready, mix_sc, slot, s_sc, *, c, nlev,
         streams):
    gs = range(streams)
    rs = [slice(g * c, (g + 1) * c) for g in gs]
    pcol = _p_reader(p_sc, ready, rs)
    assert (OFF_SMALL, P_COLS) in ready
    small = p_sc[:, OFF_SMALL:OFF_SMALL + LANES]
    xa = _dot(small.astype(BF16), wa_ref[...]) + ba_ref[...]
    la_hi, la_lo = _split_hilo(_log_sigmoid(xa) * (1.0 / GLA_TAU))
    yield
    z = [_dot(a2_ref[...], jnp.concatenate([la_hi[rs[g]], la_lo[rs[g]]], axis=0))
         for g in gs]
    yield
    ez = [jnp.exp(z[g]) for g in gs]
    yield
    q = [pcol(g, OFF_GQ, GLA_QK) * (GLA_DK ** -0.5) for g in gs]
    k = [pcol(g, OFF_GK, GLA_QK) for g in gs]
    q_in = [(q[g] * ez[g][0:c]).astype(BF16) for g in gs]
    k_dec = [(k[g] * ez[g][c:2 * c]).astype(BF16) for g in gs]
    dec = [ez[g][c - 1:c, :] for g in gs]
    q_lv = [[q[g].astype(BF16)] + [(q[g] * ez[g][(2 + l) * c:(3 + l) * c]).astype(BF16)
                                   for l in range(nlev)] for g in gs]
    k_lv = [[k[g].astype(BF16)] + [(k[g] * ez[g][(2 + l) * c:(3 + l) * c]).astype(BF16)
                                   for l in range(nlev)] for g in gs]
    yield
    lv = lv_ref[...]
    bd = bd_ref[...]
    sc = [jnp.zeros((c, GLA_HEADS * c), F32) for g in gs]
    for l in range(nlev + 1):
        for g in gs:
            k_bd = jnp.concatenate([k_lv[g][l]] * GLA_HEADS, axis=0) * bd
            sc[g] = jnp.where(lv == l, _dot_nt(q_lv[g][l], k_bd), sc[g])
        yield
    sc = [sc[g].astype(BF16) for g in gs]
    for h in range(GLA_HEADS):
        sl = slice(h * GLA_DK, (h + 1) * GLA_DK)
        for g in gs:
            vb = pcol(g, OFF_GV + h * GLA_DV, GLA_DV).astype(BF16)
            st = s_sc[g, :, sl]
            o = _dot_nt(q_in[g][:, sl], st.astype(BF16)) + _dot(sc[g][:, h * c:(h + 1) * c], vb)
            s_sc[g, :, sl] = dec[g][:, sl] * st + _dot_tn(vb, k_dec[g][:, sl])
            r = pcol(g, OFF_GR + h * GLA_DV, GLA_DV)
            mix_sc[slot, rs[g], h * GLA_DV:(h + 1) * GLA_DV] = (
                _rms(o, gn_ref[...]) * (r * jax.nn.sigmoid(r)))
        yield


def _mlstm(bsm_ref, mn_ref, a2_ref, e2_ref, p_sc, ready, mix_sc, slot, c_sc, m_sc, *, c, streams):
    rows = streams * c
    gs = range(streams)
    rs = [slice(g * c, (g + 1) * c) for g in gs]
    pcol = _p_reader(p_sc, ready, rs)
    assert (OFF_SMALL, P_COLS) in ready
    lane = lax.broadcasted_iota(jnp.int32, (c, LANES), 1)
    row = lax.broadcasted_iota(jnp.int32, (c, LANES), 0)
    on_diag = row == lane
    causal = lane <= row
    ones = jnp.ones((c, LANES), F32)
    lane_all = lax.broadcasted_iota(jnp.int32, (rows, LANES), 1)
    is_f = (lane_all >= SM_F) & (lane_all < SM_F + M_HEADS)

    x = p_sc[:, OFF_SMALL:OFF_SMALL + LANES] + bsm_ref[...]
    x = jnp.where(is_f, _log_sigmoid(x), x)
    x_hi, x_lo = _split_hilo(x)
    yield
    xcum = jnp.concatenate(
        [_dot(a2_ref[0:c, :], jnp.concatenate([x_hi[rs[g]], x_lo[rs[g]]], axis=0)) for g in gs],
        axis=0)
    xc = jnp.where(is_f, xcum, x)
    yield
    xe_all = _dot(jnp.concatenate(_split_hilo(xc), axis=1), e2_ref[...])
    xe = [xe_all[rs[g]] for g in gs]
    yield
    for h in range(M_HEADS):
        ie = [xe[g][:, h * LANES:(h + 1) * LANES] for g in gs]
        fe = [xe[g][:, (M_HEADS + h) * LANES:(M_HEADS + h + 1) * LANES] for g in gs]
        a = [ie[g] - fe[g] for g in gs]
        a_row = [jnp.sum(jnp.where(on_diag, a[g], 0.0), axis=0, keepdims=True) for g in gs]
        dm = [jnp.where(causal, fe[g] + a_row[g], -jnp.inf)[:, :c] for g in gs]
        gi = [fe[g] + m_sc[g, h] for g in gs]
        m_t = [jnp.maximum(gi[g], jnp.max(dm[g], axis=-1, keepdims=True)) for g in gs]
        w_int = [jnp.exp(gi[g] - m_t[g]) for g in gs]
        w = [jnp.exp(dm[g] - m_t[g][:, :c]) for g in gs]
        yield
        qh = [pcol(g, OFF_MQ + h * M_DH, M_DH).astype(BF16) for g in gs]
        kf = [pcol(g, OFF_MK + h * M_DH, M_DH) * (M_DH ** -0.5) for g in gs]
        vaug = [jnp.concatenate([pcol(g, OFF_MV + h * M_DH, M_DH), ones], axis=1).astype(BF16)
                for g in gs]
        qk = [(_dot_nt(qh[g], kf[g].astype(BF16)) * w[g]).astype(BF16) for g in gs]
        yield
        caug = [c_sc[g, h] for g in gs]
        nd = [_dot(qk[g], vaug[g])
              + jnp.concatenate([w_int[g], w_int[g]], axis=1) * _dot(qh[g], caug[g].astype(BF16))
              for g in gs]
        for g in gs:
            den = jnp.maximum(jnp.abs(nd[g][:, M_DH:]), jnp.exp(-m_t[g]))
            og = jax.nn.sigmoid(pcol(g, OFF_MO + h * M_DH, M_DH)) * (nd[g][:, :M_DH] / den)
            mix_sc[slot, rs[g], GLA_V + h * M_DH:GLA_V + (h + 1) * M_DH] = _rms(og, mn_ref[...])
        yield
        for g in gs:
            m_c = m_t[g][c - 1:c, :]
            decay = jnp.exp(gi[g][c - 1:c, :] - m_c)
            ws = jnp.exp(a[g] + (fe[g][c - 1:c, :] - m_c))
            kw = (kf[g] * ws).astype(BF16)
            c_sc[g, h] = jnp.concatenate([decay, decay], axis=1) * caug[g] + _dot_tn(kw, vaug[g])
            m_sc[g, h] = m_c
        yield


def _back(xb_ref, mix_sc, slot_prev, wo_ref, n2_ref, wg_ref, wu_ref, cw_ref, cb_ref, wd_ref, nf_ref,
          y_ref, carry_sc, *, c, streams):
    rows = streams * c
    d = xb_ref.shape[-1]
    d_ff = wg_ref.shape[1]
    nslab = d_ff // FFN_SLAB
    x1 = xb_ref[...].reshape(rows, d) + _dot(mix_sc[slot_prev].astype(BF16), wo_ref[...])
    h2 = _rms(x1, n2_ref[...]).astype(BF16)
    yield

    def gate_up(s):
        cs = slice(s * FFN_SLAB, (s + 1) * FFN_SLAB)
        return _dot(h2, wg_ref[:, cs]), _dot(h2, wu_ref[:, cs])

    row8 = lax.broadcasted_iota(jnp.int32, (SUBLANES, FFN_SLAB), 0)
    nxt = gate_up(0)
    acts = []
    yield
    for s in range(nslab):
        cs = slice(s * FFN_SLAB, (s + 1) * FFN_SLAB)
        g, u = nxt
        if s + 1 < nslab:
            nxt = gate_up(s + 1)
        yield
        r1 = pltpu.roll(g, 1, axis=0)
        r2 = pltpu.roll(g, 2, axis=0)
        g1, g2 = [], []
        for st in range(streams):
            b0 = st * c
            p0 = carry_sc[st, 0:1, cs]
            p1 = carry_sc[st, 1:2, cs]
            g1 += [jnp.where(row8 == 0, p1, r1[b0:b0 + SUBLANES]), r1[b0 + SUBLANES:b0 + c]]
            g2 += [jnp.where(row8 == 0, p0, jnp.where(row8 == 1, p1, r2[b0:b0 + SUBLANES])),
                   r2[b0 + SUBLANES:b0 + c]]
            carry_sc[st, 0:1, cs] = g[b0 + c - 2:b0 + c - 1, :]
            carry_sc[st, 1:2, cs] = g[b0 + c - 1:b0 + c, :]
        g1 = jnp.concatenate(g1, axis=0)
        g2 = jnp.concatenate(g2, axis=0)
        gc = cb_ref[:, cs] + (cw_ref[0:1, cs] * g2 + cw_ref[1:2, cs] * g1 + cw_ref[2:3, cs] * g)
        yield
        acts.append((jax.nn.gelu(gc) * u).astype(BF16))
        yield
    act = jnp.concatenate(acts, axis=1)
    down = []
    for lo in range(0, d, DOWN_SLAB):
        down.append(_dot(act, wd_ref[:, lo:lo + DOWN_SLAB]))
        yield
    y_ref[...] = _rms(x1 + jnp.concatenate(down, axis=1), nf_ref[...]).reshape(streams, c, d)


def _layer_kernel(*refs, c, nlev, streams, nj, nsteps, skew, zero_state):
    n_in = 2 + N_WEIGHTS + (0 if zero_state else 5)
    (xf_ref, xb_ref, n1_ref, win_ref, wsm_ref, wa_ref, ba_ref, gn_ref, bsm_ref, mn_ref, a2_ref, lv_ref,
     e2_ref, bd_ref, wo_ref, n2_ref, wg_ref, wu_ref, cw_ref, cb_ref, wd_ref, nf_ref) = refs[:2 + N_WEIGHTS]
    s0_ref, c0_ref, n0_ref, m0_ref, cst_ref = (None,) * 5 if zero_state else refs[2 + N_WEIGHTS:n_in]
    y_ref, s_out, c_out, n_out, m_out, cv_out = refs[n_in:n_in + 6]
    p_sc, mix_sc, s_sc, c_sc, m_sc, carry_sc = refs[n_in + 6:n_in + 12]
    w_copies = []
    if not skew:
        w_bufs, w_sem = refs[n_in + 12:n_in + 18], refs[n_in + 18]
        w_hbm = (win_ref, wsm_ref, wo_ref, wg_ref, wu_ref, wd_ref)
        w_copies = [pltpu.make_async_copy(h, v, w_sem.at[i]) for i, (h, v) in enumerate(zip(w_hbm, w_bufs))]
        for cp in w_copies[:2]:
            cp.start(priority=0)
        for cp in w_copies[2:]:
            cp.start(priority=1)
        win_ref, wsm_ref, wo_ref, wg_ref, wu_ref, wd_ref = w_bufs

    t = pl.program_id(0)
    tf = jnp.minimum(t, nsteps - 1)
    jf = lax.rem(tf, nj)
    tb = jnp.maximum(t - skew, 0)
    jb = lax.rem(tb, nj)
    slot = lax.rem(t, 2)
    slot_back = 1 - slot if skew else slot

    if skew:
        @pl.when(t == 0)
        def _():
            mix_sc[1] = jnp.zeros(mix_sc.shape[1:], F32)

    @pl.when(jf == 0)
    def _():
        if zero_state:
            s_sc[...] = jnp.zeros(s_sc.shape, F32)
            c_sc[...] = jnp.zeros(c_sc.shape, F32)
            m_sc[...] = jnp.zeros(m_sc.shape, F32)
        else:
            for g in range(streams):
                for lo in range(0, GLA_QK, LANES):
                    s_sc[g, :, lo:lo + LANES] = s0_ref[g, lo:lo + LANES, :].T
            c_sc[:, :, :, :M_DH] = c0_ref[...]
            c_sc[:, :, :, M_DH:] = jnp.broadcast_to(n0_ref[...], c0_ref.shape)
            m_sc[...] = m0_ref[...]

    @pl.when(jb == 0)
    def _():
        carry_sc[...] = jnp.zeros(carry_sc.shape, F32) if zero_state else cst_ref[...]

    back = _back(xb_ref, mix_sc, slot_back, wo_ref, n2_ref, wg_ref, wu_ref, cw_ref, cb_ref, wd_ref,
                 nf_ref, y_ref, carry_sc, c=c, streams=streams)
    if skew:
        next(back)
    ready = []
    for cp in w_copies[:2]:
        cp.wait()
    for _ in _inproj(xf_ref, n1_ref, win_ref, wsm_ref, p_sc, ready, c=c, streams=streams):
        pass
    mixers = [_gla(wa_ref, ba_ref, gn_ref, a2_ref, lv_ref, bd_ref, p_sc, ready, mix_sc, slot, s_sc,
                   c=c, nlev=nlev, streams=streams),
              _mlstm(bsm_ref, mn_ref, a2_ref, e2_ref, p_sc, ready, mix_sc, slot, c_sc, m_sc,
                     c=c, streams=streams)]
    if skew:
        _interleave(mixers + [back], strides=(1, 1, 3))
    else:
        _interleave(mixers, strides=(1, 1))
        for cp in w_copies[2:]:
            cp.wait()
        _interleave([back], strides=(1,))

    @pl.when((jf == nj - 1) & (t < nsteps))
    def _():
        for g in range(streams):
            for lo in range(0, GLA_QK, LANES):
                s_out[g, lo:lo + LANES, :] = s_sc[g, :, lo:lo + LANES].T
        c_out[...] = c_sc[:, :, :, :M_DH]
        on_diag = (lax.broadcasted_iota(jnp.int32, (M_DH, M_DH), 0)
                   == lax.broadcasted_iota(jnp.int32, (M_DH, M_DH), 1))
        for g in range(streams):
            for h in range(M_HEADS):
                n_out[g, h] = jnp.sum(jnp.where(on_diag, c_sc[g, h, :, M_DH:], 0.0), axis=0, keepdims=True)
        m_out[...] = m_sc[...]

    @pl.when((jb == nj - 1) & (t >= skew))
    def _():
        cv_out[...] = carry_sc[...]


def _layer(x, states, wts):
    batch, length, d = x.shape
    c = min(CHUNK, length)
    nj = length // c
    streams = min(batch, MXU_ROWS // c)
    assert batch % streams == 0 and length % c == 0
    nsteps = (batch // streams) * nj
    d_ff = wts["w_g"].shape[1]
    a2, lv, e2, bd, nlev = _mixer_consts(c)
    skew = 1 if nsteps > 1 else 0

    def front_idx(t):
        tf = jnp.minimum(t, nsteps - 1)
        return tf // nj, tf % nj

    def back_idx(t):
        tb = jnp.maximum(t - skew, 0)
        return tb // nj, tb % nj

    once = pl.Buffered(1)
    const = lambda a: pl.BlockSpec(a.shape, lambda t: (0,) * a.ndim, pipeline_mode=once)
    fstate = lambda *blk, **kw: pl.BlockSpec(
        (streams,) + blk, lambda t: (front_idx(t)[0],) + (0,) * len(blk), **kw)
    bstate = lambda *blk, **kw: pl.BlockSpec(
        (streams,) + blk, lambda t: (back_idx(t)[0],) + (0,) * len(blk), **kw)
    consts = [wts["norm1"], wts["w_in_r"], wts["w_in_sm"], wts["wa_pad"], wts["b_gla_a"], wts["gla_norm"],
              wts["b_small"], wts["m_norm"], a2, lv, e2, bd, wts["w_out"], wts["norm2"], wts["w_g"],
              wts["w_u"], wts["conv_w"], wts["conv_b"], wts["w_d"], wts["norm_f"]]
    assert len(consts) == N_WEIGHTS
    const_specs = [const(a) for a in consts]
    w_scratch = []
    if not skew:
        big = [wts[k] for k in ("w_in_r", "w_in_sm", "w_out", "w_g", "w_u", "w_d")]
        for i, a in enumerate(consts):
            if any(a is b for b in big):
                const_specs[i] = pl.BlockSpec(memory_space=pl.ANY)
        w_scratch = [pltpu.VMEM(a.shape, a.dtype) for a in big] + [pltpu.SemaphoreType.DMA((len(big),))]
    state_ops, state_specs = [], []
    if states is not None:
        s_gla, c_m, n_m, m_m, conv_st = states
        state_ops = [
            s_gla.reshape(batch, GLA_QK, GLA_DV),
            c_m, n_m[..., None],
            jnp.broadcast_to(m_m[..., None, None], (batch, M_HEADS, 1, LANES)), conv_st]
        state_specs = [
            fstate(GLA_QK, GLA_DV, pipeline_mode=once),
            fstate(M_HEADS, M_DH, M_DH, pipeline_mode=once),
            fstate(M_HEADS, M_DH, 1, pipeline_mode=once),
            fstate(M_HEADS, 1, LANES, pipeline_mode=once),
            bstate(FFN_CONV - 1, d_ff, pipeline_mode=once)]
    outs = pl.pallas_call(
        functools.partial(_layer_kernel, c=c, nlev=nlev, streams=streams, nj=nj, nsteps=nsteps,
                          skew=skew, zero_state=states is None),
        grid=(nsteps + skew,),
        in_specs=[
            pl.BlockSpec((streams, c, d), lambda t: front_idx(t) + (0,)),
            pl.BlockSpec((streams, c, d), lambda t: back_idx(t) + (0,)),
        ] + const_specs + state_specs,
        out_specs=[
            pl.BlockSpec((streams, c, d), lambda t: back_idx(t) + (0,)),
            fstate(GLA_QK, GLA_DV), fstate(M_HEADS, M_DH, M_DH), fstate(M_HEADS, 1, M_DH),
            fstate(M_HEADS, 1, LANES), bstate(FFN_CONV - 1, d_ff),
        ],
        out_shape=[
            jax.ShapeDtypeStruct((batch, length, d), F32),
            jax.ShapeDtypeStruct((batch, GLA_QK, GLA_DV), F32),
            jax.ShapeDtypeStruct((batch, M_HEADS, M_DH, M_DH), F32),
            jax.ShapeDtypeStruct((batch, M_HEADS, 1, M_DH), F32),
            jax.ShapeDtypeStruct((batch, M_HEADS, 1, LANES), F32),
            jax.ShapeDtypeStruct((batch, FFN_CONV - 1, d_ff), F32),
        ],
        scratch_shapes=[
            pltpu.VMEM((streams * c, P_COLS), F32),
            pltpu.VMEM((2, streams * c, D_MIX), F32),
            pltpu.VMEM((streams, GLA_DV, GLA_QK), F32),
            pltpu.VMEM((streams, M_HEADS, M_DH, 2 * M_DH), F32),
            pltpu.VMEM((streams, M_HEADS, 1, LANES), F32),
            pltpu.VMEM((streams, FFN_CONV - 1, d_ff), F32),
        ] + w_scratch,
        compiler_params=pltpu.CompilerParams(
            dimension_semantics=("arbitrary",), vmem_limit_bytes=VMEM_LIMIT),
        name="layer",
    )(x, x, *consts, *state_ops)
    y, s_new, c_new, n_new, m_e, conv_new = outs
    return (y, s_new.reshape(batch, GLA_HEADS, GLA_DK, GLA_DV), c_new,
            n_new.reshape(batch, M_HEADS, M_DH), m_e[:, :, 0, 0], conv_new)


def _regroup_kernel(wg_ref, wm_ref, glr_ref, mif_ref, o_ref, osm_ref, *, n_gla):
    j = pl.program_id(0)

    @pl.when(j < n_gla)
    def _():
        o_ref[...] = wg_ref[...].T.astype(BF16)

    @pl.when(j >= n_gla)
    def _():
        o_ref[...] = wm_ref[...].T.astype(BF16)

    @pl.when(j == 0)
    def _():
        d = wg_ref.shape[1]
        small = jnp.concatenate(
            [glr_ref[...], mif_ref[...], jnp.zeros((LANES - GLA_RANK - 2 * M_HEADS, d), F32)], axis=0)
        osm_ref[...] = small.T.astype(BF16)


def _regroup_w_in(w):
    d = w.shape[0]
    wt = w.T
    o_glr = OFF_MQ
    o_m = o_glr + GLA_RANK
    o_if = o_m + 4 * M_W
    nslab = OFF_SMALL // INPROJ_SLAB
    n_gla = OFF_MQ // INPROJ_SLAB
    rows = lambda n: (pl.Element(n), pl.Element(d))
    return pl.pallas_call(
        functools.partial(_regroup_kernel, n_gla=n_gla),
        grid=(nslab,),
        in_specs=[
            pl.BlockSpec(rows(INPROJ_SLAB), lambda j: (jnp.minimum(j, n_gla - 1) * INPROJ_SLAB, 0)),
            pl.BlockSpec(rows(INPROJ_SLAB), lambda j: (
                (o_m // GLA_RANK + jnp.maximum(j - n_gla, 0) * (INPROJ_SLAB // GLA_RANK)) * GLA_RANK, 0)),
            pl.BlockSpec(rows(GLA_RANK), lambda j: (o_glr, 0)),
            pl.BlockSpec(rows(2 * M_HEADS), lambda j: (o_if, 0)),
        ],
        out_specs=[
            pl.BlockSpec((d, INPROJ_SLAB), lambda j: (0, j)),
            pl.BlockSpec((d, LANES), lambda j: (0, 0)),
        ],
        out_shape=[
            jax.ShapeDtypeStruct((d, OFF_SMALL), BF16),
            jax.ShapeDtypeStruct((d, LANES), BF16),
        ],
        compiler_params=pltpu.CompilerParams(dimension_semantics=("arbitrary",)),
        name="regroup_w_in",
    )(wt, wt, wt, wt)


def kernel(x_prompt, x_sample, state_gla, state_mlstm_C, state_mlstm_n, state_mlstm_m, cache_ffn_conv,
           norm1, w_in, w_gla_a, b_gla_a, gla_norm, b_m_i, b_m_f, m_norm, w_out, norm2, w_g, w_u,
           conv_w, conv_b, w_d, norm_f):
    assert w_in.shape[0] == 1, "single-layer stack"
    w_in_r, w_in_sm = _regroup_w_in(w_in[0])
    wa_pad = jnp.concatenate(
        [w_gla_a[0], jnp.zeros((LANES - GLA_RANK, GLA_QK), w_gla_a.dtype)], axis=0).astype(BF16)
    b_small = jnp.concatenate(
        [jnp.zeros((GLA_RANK,), F32), b_m_i[0], b_m_f[0],
         jnp.zeros((LANES - GLA_RANK - 2 * M_HEADS,), F32)])[None]
    wts = dict(
        norm1=norm1, w_in_r=w_in_r, w_in_sm=w_in_sm, wa_pad=wa_pad, b_gla_a=b_gla_a, gla_norm=gla_norm,
        b_small=b_small, m_norm=m_norm, w_out=w_out[0].astype(BF16), norm2=norm2,
        w_g=w_g[0].astype(BF16), w_u=w_u[0].astype(BF16), conv_w=conv_w[0], conv_b=conv_b,
        w_d=w_d[0].astype(BF16), norm_f=norm_f[None])

    outs_p = _layer(x_prompt, None, wts)
    outs_s = _layer(x_sample, (state_gla[0], state_mlstm_C[0], state_mlstm_n[0], state_mlstm_m[0],
                               cache_ffn_conv[0]), wts)
    y_p, *st_p = outs_p
    y_s, *st_s = outs_s
    return (y_p, y_s) + tuple(s[None] for s in st_p) + tuple(s[None] for s in st_s)
```

```python
import functools

import numpy as np
import jax
import jax.numpy as jnp
from jax import lax
from jax.experimental import pallas as pl
from jax.experimental.pallas import tpu as pltpu

F32 = jnp.float32
BF16 = jnp.bfloat16

EPS = 1e-6
CHUNK = 64
GLA_HEADS = 4
GLA_DK = 64
GLA_DV = 128
GLA_RANK = 16
GLA_TAU = 16.0
M_HEADS = 4
M_DH = 128
FFN_CONV = 3
LANES = 128
SUBLANES = 8
MXU_ROWS = 256

GLA_QK = GLA_HEADS * GLA_DK
GLA_V = GLA_HEADS * GLA_DV
M_W = M_HEADS * M_DH
D_MIX = GLA_V + M_W
OFF_GQ = 0
OFF_GK = OFF_GQ + GLA_QK
OFF_GV = OFF_GK + GLA_QK
OFF_GR = OFF_GV + GLA_V
OFF_MQ = OFF_GR + GLA_V
OFF_MK = OFF_MQ + M_W
OFF_MV = OFF_MK + M_W
OFF_MO = OFF_MV + M_W
OFF_SMALL = OFF_MO + M_W
P_COLS = OFF_SMALL + LANES
SM_I = GLA_RANK
SM_F = GLA_RANK + M_HEADS

INPROJ_SLAB = 512
FFN_SLAB = 256
DOWN_SLAB = 256
VMEM_LIMIT = 60 * 1024 * 1024
N_WEIGHTS = 20


def _dot(a, b):
    return jnp.dot(a, b, preferred_element_type=F32)


def _dot_nt(a, b):
    return lax.dot_general(a, b, (((1,), (1,)), ((), ())), preferred_element_type=F32)


def _dot_tn(a, b):
    return lax.dot_general(a, b, (((0,), (0,)), ((), ())), preferred_element_type=F32)


def _split_hilo(x):
    hi = x.astype(BF16)
    lo = (x - hi.astype(F32)).astype(BF16)
    return hi, lo


def _log_sigmoid(x):
    return jnp.minimum(x, 0.0) - jnp.log1p(jnp.exp(-jnp.abs(x)))


def _rms(x, g):
    return x * lax.rsqrt(jnp.mean(x * x, axis=-1, keepdims=True) + EPS) * g


def _interleave(tasks, strides):
    alive = list(zip(tasks, strides))
    while alive:
        for item in list(alive):
            task, stride = item
            for _ in range(stride):
                try:
                    next(task)
                except StopIteration:
                    alive.remove(item)
                    break


def _mixer_consts(c):
    nlev = int(np.log2(c))
    assert 2 ** nlev == c
    t = np.arange(c)[:, None]
    u = np.arange(c)[None, :]
    blocks = [(u <= t), (u > t)]
    for l in range(1, nlev + 1):
        bsz, half = 2 ** l, 2 ** (l - 1)
        mid = (t // bsz) * bsz + half
        right = (t % bsz) >= half
        blocks.append(np.where(right, (u >= mid) & (u <= t), (u > t) & (u < mid)))
    a = np.concatenate(blocks, axis=0).astype(np.float32)
    a2 = np.concatenate([a, a], axis=1)
    x = np.bitwise_xor(t, u)
    lv = np.where(u > t, -1, np.where(x == 0, 0, np.floor(np.log2(np.maximum(x, 1))) + 1))
    e = np.zeros((LANES, 2 * M_HEADS * LANES), np.float32)
    for h in range(M_HEADS):
        e[SM_I + h, h * LANES:(h + 1) * LANES] = 1.0
        e[SM_F + h, (M_HEADS + h) * LANES:(M_HEADS + h + 1) * LANES] = 1.0
    e2 = np.concatenate([e, e], axis=0)
    lv4 = np.tile(lv, (1, GLA_HEADS))
    bd = (np.arange(GLA_HEADS * c)[:, None] // c == np.arange(GLA_QK)[None, :] // GLA_DK)
    return (jnp.asarray(a2, BF16), jnp.asarray(lv4, jnp.int32), jnp.asarray(e2, BF16),
            jnp.asarray(bd, BF16), nlev)


def _inproj(xf_ref, n1_ref, win_ref, wsm_ref, p_sc, ready, *, c, streams):
    rows = streams * c
    d = xf_ref.shape[-1]
    h_in = _rms(xf_ref[...].reshape(rows, d), n1_ref[...]).astype(BF16)
    for lo in range(0, OFF_SMALL, INPROJ_SLAB):
        hi = lo + INPROJ_SLAB
        p_sc[:, lo:hi] = _dot(h_in, win_ref[:, lo:hi])
        ready.append((lo, hi))
        yield
    p_sc[:, OFF_SMALL:] = _dot(h_in, wsm_ref[...])
    ready.append((OFF_SMALL, P_COLS))
    yield


def _p_reader(p_sc, ready, rs):
    def pcol(g, off, width):
        assert any(lo <= off and off + width <= hi for lo, hi in ready), (off, width)
        return p_sc[rs[g], off:off + width]
    return pcol


def _gla(wa_ref, ba_ref, gn_ref, a2_ref, lv_ref, bd_ref, p_sc, ready, mix_sc, slot, s_sc, *, c, nlev,
         streams):
    gs = range(streams)
    rs = [slice(g * c, (g + 1) * c) for g in gs]
    pcol = _p_reader(p_sc, ready, rs)
    assert (OFF_SMALL, P_COLS) in ready
    small = p_sc[:, OFF_SMALL:OFF_SMALL + LANES]
    xa = _dot(small.astype(BF16), wa_ref[...]) + ba_ref[...]
    la_hi, la_lo = _split_hilo(_log_sigmoid(xa) * (1.0 / GLA_TAU))
    yield
    z = [_dot(a2_ref[...], jnp.concatenate([la_hi[rs[g]], la_lo[rs[g]]], axis=0))
         for g in gs]
    yield
    ez = [jnp.exp(z[g]) for g in gs]
    yield
    q = [pcol(g, OFF_GQ, GLA_QK) * (GLA_DK ** -0.5) for g in gs]
    k = [pcol(g, OFF_GK, GLA_QK) for g in gs]
    q_in = [(q[g] * ez[g][0:c]).astype(BF16) for g in gs]
    k_dec = [(k[g] * ez[g][c:2 * c]).astype(BF16) for g in gs]
    dec = [ez[g][c - 1:c, :] for g in gs]
    q_lv = [[q[g].astype(BF16)] + [(q[g] * ez[g][(2 + l) * c:(3 + l) * c]).astype(BF16)
                                   for l in range(nlev)] for g in gs]
    k_lv = [[k[g].astype(BF16)] + [(k[g] * ez[g][(2 + l) * c:(3 + l) * c]).astype(BF16)
                                   for l in range(nlev)] for g in gs]
    yield
    lv = lv_ref[...]
    bd = bd_ref[...]
    sc = [jnp.zeros((c, GLA_HEADS * c), F32) for g in gs]
    for l in range(nlev + 1):
        for g in gs:
            k_bd = jnp.concatenate([k_lv[g][l]] * GLA_HEADS, axis=0) * bd
            sc[g] = jnp.where(lv == l, _dot_nt(q_lv[g][l], k_bd), sc[g])
        yield
    sc = [sc[g].astype(BF16) for g in gs]
    for h in range(GLA_HEADS):
        sl = slice(h * GLA_DK, (h + 1) * GLA_DK)
        for g in gs:
            vb = pcol(g, OFF_GV + h * GLA_DV, GLA_DV).astype(BF16)
            st = s_sc[g, :, sl]
            o = _dot_nt(q_in[g][:, sl], st.astype(BF16)) + _dot(sc[g][:, h * c:(h + 1) * c], vb)
            s_sc[g, :, sl] = dec[g][:, sl] * st + _dot_tn(vb, k_dec[g][:, sl])
            r = pcol(g, OFF_GR + h * GLA_DV, GLA_DV)
            mix_sc[slot, rs[g], h * GLA_DV:(h + 1) * GLA_DV] = (
                _rms(o, gn_ref[...]) * (r * jax.nn.sigmoid(r)))
        yield


def _mlstm(bsm_ref, mn_ref, a2_ref, e2_ref, p_sc, ready, mix_sc, slot, c_sc, m_sc, *, c, streams):
    rows = streams * c
    gs = range(streams)
    rs = [slice(g * c, (g + 1) * c) for g in gs]
    pcol = _p_reader(p_sc, ready, rs)
    assert (OFF_SMALL, P_COLS) in ready
    lane = lax.broadcasted_iota(jnp.int32, (c, LANES), 1)
    row = lax.broadcasted_iota(jnp.int32, (c, LANES), 0)
    on_diag = row == lane
    causal = lane <= row
    ones = jnp.ones((c, LANES), F32)
    lane_all = lax.broadcasted_iota(jnp.int32, (rows, LANES), 1)
    is_f = (lane_all >= SM_F) & (lane_all < SM_F + M_HEADS)

    x = p_sc[:, OFF_SMALL:OFF_SMALL + LANES] + bsm_ref[...]
    x = jnp.where(is_f, _log_sigmoid(x), x)
    x_hi, x_lo = _split_hilo(x)
    yield
    xcum = jnp.concatenate(
        [_dot(a2_ref[0:c, :], jnp.concatenate([x_hi[rs[g]], x_lo[rs[g]]], axis=0)) for g in gs],
        axis=0)
    xc = jnp.where(is_f, xcum, x)
    yield
    xe_all = _dot(jnp.concatenate(_split_hilo(xc), axis=1), e2_ref[...])
    xe = [xe_all[rs[g]] for g in gs]
    yield
    for h in range(M_HEADS):
        ie = [xe[g][:, h * LANES:(h + 1) * LANES] for g in gs]
        fe = [xe[g][:, (M_HEADS + h) * LANES:(M_HEADS + h + 1) * LANES] for g in gs]
        a = [ie[g] - fe[g] for g in gs]
        a_row = [jnp.sum(jnp.where(on_diag, a[g], 0.0), axis=0, keepdims=True) for g in gs]
        dm = [jnp.where(causal, fe[g] + a_row[g], -jnp.inf)[:, :c] for g in gs]
        gi = [fe[g] + m_sc[g, h] for g in gs]
        m_t = [jnp.maximum(gi[g], jnp.max(dm[g], axis=-1, keepdims=True)) for g in gs]
        w_int = [jnp.exp(gi[g] - m_t[g]) for g in gs]
        w = [jnp.exp(dm[g] - m_t[g][:, :c]) for g in gs]
        yield
        qh = [pcol(g, OFF_MQ + h * M_DH, M_DH).astype(BF16) for g in gs]
        kf = [pcol(g, OFF_MK + h * M_DH, M_DH) * (M_DH ** -0.5) for g in gs]
        vaug = [jnp.concatenate([pcol(g, OFF_MV + h * M_DH, M_DH), ones], axis=1).astype(BF16)
                for g in gs]
        qk = [(_dot_nt(qh[g], kf[g].astype(BF16)) * w[g]).astype(BF16) for g in gs]
        yield
        caug = [c_sc[g, h] for g in gs]
        nd = [_dot(qk[g], vaug[g])
              + jnp.concatenate([w_int[g], w_int[g]], axis=1) * _dot(qh[g], caug[g].astype(BF16))
              for g in gs]
        for g in gs:
            den = jnp.maximum(jnp.abs(nd[g][:, M_DH:]), jnp.exp(-m_t[g]))
            og = jax.nn.sigmoid(pcol(g, OFF_MO + h * M_DH, M_DH)) * (nd[g][:, :M_DH] / den)
            mix_sc[slot, rs[g], GLA_V + h * M_DH:GLA_V + (h + 1) * M_DH] = _rms(og, mn_ref[...])
        yield
        for g in gs:
            m_c = m_t[g][c - 1:c, :]
            decay = jnp.exp(gi[g][c - 1:c, :] - m_c)
            ws = jnp.exp(a[g] + (fe[g][c - 1:c, :] - m_c))
            kw = (kf[g] * ws).astype(BF16)
            c_sc[g, h] = jnp.concatenate([decay, decay], axis=1) * caug[g] + _dot_tn(kw, vaug[g])
            m_sc[g, h] = m_c
        yield


def _back(xb_ref, mix_sc, slot_prev, wo_ref, n2_ref, wg_ref, wu_ref, cw_ref, cb_ref, wd_ref, nf_ref,
          y_ref, carry_sc, *, c, streams):
    rows = streams * c
    d = xb_ref.shape[-1]
    d_ff = wg_ref.shape[1]
    nslab = d_ff // FFN_SLAB
    x1 = xb_ref[...].reshape(rows, d) + _dot(mix_sc[slot_prev].astype(BF16), wo_ref[...])
    h2 = _rms(x1, n2_ref[...]).astype(BF16)
    yield

    def gate_up(s):
        cs = slice(s * FFN_SLAB, (s + 1) * FFN_SLAB)
        return _dot(h2, wg_ref[:, cs]), _dot(h2, wu_ref[:, cs])

    row8 = lax.broadcasted_iota(jnp.int32, (SUBLANES, FFN_SLAB), 0)
    nxt = gate_up(0)
    acts = []
    yield
    for s in range(nslab):
        cs = slice(s * FFN_SLAB, (s + 1) * FFN_SLAB)
        g, u = nxt
        if s + 1 < nslab:
            nxt = gate_up(s + 1)
        yield
        halves = []
        for s0 in range(0, streams, streams // 2):
            lo_r, hi_r = s0 * c, (s0 + streams // 2) * c
            gh = g[lo_r:hi_r]
            r1 = pltpu.roll(gh, 1, axis=0)
            r2 = pltpu.roll(gh, 2, axis=0)
            g1, g2 = [], []
            for st in range(s0, s0 + streams // 2):
                b0 = st * c - lo_r
                p0 = carry_sc[st, 0:1, cs]
                p1 = carry_sc[st, 1:2, cs]
                g1 += [jnp.where(row8 == 0, p1, r1[b0:b0 + SUBLANES]), r1[b0 + SUBLANES:b0 + c]]
                g2 += [jnp.where(row8 == 0, p0, jnp.where(row8 == 1, p1, r2[b0:b0 + SUBLANES])),
                       r2[b0 + SUBLANES:b0 + c]]
                carry_sc[st, 0:1, cs] = gh[b0 + c - 2:b0 + c - 1, :]
                carry_sc[st, 1:2, cs] = gh[b0 + c - 1:b0 + c, :]
            g1 = jnp.concatenate(g1, axis=0)
            g2 = jnp.concatenate(g2, axis=0)
            gc = cb_ref[:, cs] + (cw_ref[0:1, cs] * g2 + cw_ref[1:2, cs] * g1 + cw_ref[2:3, cs] * gh)
            halves.append((gc, u[lo_r:hi_r]))
        yield
        acts.append(jnp.concatenate([(jax.nn.gelu(gc) * uh).astype(BF16) for gc, uh in halves], axis=0))
        yield
    act = jnp.concatenate(acts, axis=1)
    down = []
    for lo in range(0, d, DOWN_SLAB):
        down.append(_dot(act, wd_ref[:, lo:lo + DOWN_SLAB]))
        yield
    y_ref[...] = _rms(x1 + jnp.concatenate(down, axis=1), nf_ref[...]).reshape(streams, c, d)


def _layer_kernel(*refs, c, nlev, streams, nj, nsteps, skew, zero_state):
    n_in = 2 + N_WEIGHTS + (0 if zero_state else 5)
    (xf_ref, xb_ref, n1_ref, win_ref, wsm_ref, wa_ref, ba_ref, gn_ref, bsm_ref, mn_ref, a2_ref, lv_ref,
     e2_ref, bd_ref, wo_ref, n2_ref, wg_ref, wu_ref, cw_ref, cb_ref, wd_ref, nf_ref) = refs[:2 + N_WEIGHTS]
    s0_ref, c0_ref, n0_ref, m0_ref, cst_ref = (None,) * 5 if zero_state else refs[2 + N_WEIGHTS:n_in]
    y_ref, s_out, c_out, n_out, m_out, cv_out = refs[n_in:n_in + 6]
    p_sc, mix_sc, s_sc, c_sc, m_sc, carry_sc = refs[n_in + 6:n_in + 12]
    w_copies = []
    if not skew:
        w_bufs, w_sem = refs[n_in + 12:n_in + 18], refs[n_in + 18]
        w_hbm = (win_ref, wsm_ref, wo_ref, wg_ref, wu_ref, wd_ref)
        w_copies = [pltpu.make_async_copy(h, v, w_sem.at[i]) for i, (h, v) in enumerate(zip(w_hbm, w_bufs))]
        for cp in w_copies[:2]:
            cp.start()
        win_ref, wsm_ref, wo_ref, wg_ref, wu_ref, wd_ref = w_bufs

    t = pl.program_id(0)
    tf = jnp.minimum(t, nsteps - 1)
    jf = lax.rem(tf, nj)
    tb = jnp.maximum(t - skew, 0)
    jb = lax.rem(tb, nj)
    slot = lax.rem(t, 2)
    slot_back = 1 - slot if skew else slot

    if skew:
        @pl.when(t == 0)
        def _():
            mix_sc[1] = jnp.zeros(mix_sc.shape[1:], F32)

    @pl.when(jf == 0)
    def _():
        if zero_state:
            s_sc[...] = jnp.zeros(s_sc.shape, F32)
            c_sc[...] = jnp.zeros(c_sc.shape, F32)
            m_sc[...] = jnp.zeros(m_sc.shape, F32)
        else:
            for g in range(streams):
                for lo in range(0, GLA_QK, LANES):
                    s_sc[g, :, lo:lo + LANES] = s0_ref[g, lo:lo + LANES, :].T
            c_sc[:, :, :, :M_DH] = c0_ref[...]
            c_sc[:, :, :, M_DH:] = jnp.broadcast_to(n0_ref[...], c0_ref.shape)
            m_sc[...] = m0_ref[...]

    @pl.when(jb == 0)
    def _():
        carry_sc[...] = jnp.zeros(carry_sc.shape, F32) if zero_state else cst_ref[...]

    back = _back(xb_ref, mix_sc, slot_back, wo_ref, n2_ref, wg_ref, wu_ref, cw_ref, cb_ref, wd_ref,
                 nf_ref, y_ref, carry_sc, c=c, streams=streams)
    if skew:
        next(back)
    ready = []
    for cp in w_copies[:2]:
        cp.wait()
    for cp in w_copies[2:]:
        cp.start()
    for _ in _inproj(xf_ref, n1_ref, win_ref, wsm_ref, p_sc, ready, c=c, streams=streams):
        pass
    mixers = [_gla(wa_ref, ba_ref, gn_ref, a2_ref, lv_ref, bd_ref, p_sc, ready, mix_sc, slot, s_sc,
                   c=c, nlev=nlev, streams=streams),
              _mlstm(bsm_ref, mn_ref, a2_ref, e2_ref, p_sc, ready, mix_sc, slot, c_sc, m_sc,
                     c=c, streams=streams)]
    if skew:
        _interleave(mixers + [back], strides=(1, 1, 3))
    else:
        _interleave(mixers, strides=(1, 1))
        for cp in w_copies[2:]:
            cp.wait()
        _interleave([back], strides=(1,))

    @pl.when((jf == nj - 1) & (t < nsteps))
    def _():
        for g in range(streams):
            for lo in range(0, GLA_QK, LANES):
                s_out[g, lo:lo + LANES, :] = s_sc[g, :, lo:lo + LANES].T
        c_out[...] = c_sc[:, :, :, :M_DH]
        on_diag = (lax.broadcasted_iota(jnp.int32, (M_DH, M_DH), 0)
                   == lax.broadcasted_iota(jnp.int32, (M_DH, M_DH), 1))
        for g in range(streams):
            for h in range(M_HEADS):
                n_out[g, h] = jnp.sum(jnp.where(on_diag, c_sc[g, h, :, M_DH:], 0.0), axis=0, keepdims=True)
        m_out[...] = m_sc[...]

    @pl.when((jb == nj - 1) & (t >= skew))
    def _():
        cv_out[...] = carry_sc[...]


def _layer(x, states, wts):
    batch, length, d = x.shape
    c = min(CHUNK, length)
    nj = length // c
    streams = min(batch, MXU_ROWS // c)
    assert batch % streams == 0 and length % c == 0
    nsteps = (batch // streams) * nj
    d_ff = wts["w_g"].shape[1]
    a2, lv, e2, bd, nlev = _mixer_consts(c)
    skew = 1 if nsteps > 1 else 0

    def front_idx(t):
        tf = jnp.minimum(t, nsteps - 1)
        return tf // nj, tf % nj

    def back_idx(t):
        tb = jnp.maximum(t - skew, 0)
        return tb // nj, tb % nj

    once = pl.Buffered(1)
    const = lambda a: pl.BlockSpec(a.shape, lambda t: (0,) * a.ndim, pipeline_mode=once)
    fstate = lambda *blk, **kw: pl.BlockSpec(
        (streams,) + blk, lambda t: (front_idx(t)[0],) + (0,) * len(blk), **kw)
    bstate = lambda *blk, **kw: pl.BlockSpec(
        (streams,) + blk, lambda t: (back_idx(t)[0],) + (0,) * len(blk), **kw)
    consts = [wts["norm1"], wts["w_in_r"], wts["w_in_sm"], wts["wa_pad"], wts["b_gla_a"], wts["gla_norm"],
              wts["b_small"], wts["m_norm"], a2, lv, e2, bd, wts["w_out"], wts["norm2"], wts["w_g"],
              wts["w_u"], wts["conv_w"], wts["conv_b"], wts["w_d"], wts["norm_f"]]
    assert len(consts) == N_WEIGHTS
    const_specs = [const(a) for a in consts]
    w_scratch = []
    if not skew:
        big = [wts[k] for k in ("w_in_r", "w_in_sm", "w_out", "w_g", "w_u", "w_d")]
        for i, a in enumerate(consts):
            if any(a is b for b in big):
                const_specs[i] = pl.BlockSpec(memory_space=pl.ANY)
        w_scratch = [pltpu.VMEM(a.shape, a.dtype) for a in big] + [pltpu.SemaphoreType.DMA((len(big),))]
    state_ops, state_specs = [], []
    if states is not None:
        s_gla, c_m, n_m, m_m, conv_st = states
        state_ops = [
            s_gla.reshape(batch, GLA_QK, GLA_DV),
            c_m, n_m[..., None],
            jnp.broadcast_to(m_m[..., None, None], (batch, M_HEADS, 1, LANES)), conv_st]
        state_specs = [
            fstate(GLA_QK, GLA_DV, pipeline_mode=once),
            fstate(M_HEADS, M_DH, M_DH, pipeline_mode=once),
            fstate(M_HEADS, M_DH, 1, pipeline_mode=once),
            fstate(M_HEADS, 1, LANES, pipeline_mode=once),
            bstate(FFN_CONV - 1, d_ff, pipeline_mode=once)]
    outs = pl.pallas_call(
        functools.partial(_layer_kernel, c=c, nlev=nlev, streams=streams, nj=nj, nsteps=nsteps,
                          skew=skew, zero_state=states is None),
        grid=(nsteps + skew,),
        in_specs=[
            pl.BlockSpec((streams, c, d), lambda t: front_idx(t) + (0,)),
            pl.BlockSpec((streams, c, d), lambda t: back_idx(t) + (0,)),
        ] + const_specs + state_specs,
        out_specs=[
            pl.BlockSpec((streams, c, d), lambda t: back_idx(t) + (0,)),
            fstate(GLA_QK, GLA_DV), fstate(M_HEADS, M_DH, M_DH), fstate(M_HEADS, 1, M_DH),
            fstate(M_HEADS, 1, LANES), bstate(FFN_CONV - 1, d_ff),
        ],
        out_shape=[
            jax.ShapeDtypeStruct((batch, length, d), F32),
            jax.ShapeDtypeStruct((batch, GLA_QK, GLA_DV), F32),
            jax.ShapeDtypeStruct((batch, M_HEADS, M_DH, M_DH), F32),
            jax.ShapeDtypeStruct((batch, M_HEADS, 1, M_DH), F32),
            jax.ShapeDtypeStruct((batch, M_HEADS, 1, LANES), F32),
            jax.ShapeDtypeStruct((batch, FFN_CONV - 1, d_ff), F32),
        ],
        scratch_shapes=[
            pltpu.VMEM((streams * c, P_COLS), F32),
            pltpu.VMEM((2, streams * c, D_MIX), F32),
            pltpu.VMEM((streams, GLA_DV, GLA_QK), F32),
            pltpu.VMEM((streams, M_HEADS, M_DH, 2 * M_DH), F32),
            pltpu.VMEM((streams, M_HEADS, 1, LANES), F32),
            pltpu.VMEM((streams, FFN_CONV - 1, d_ff), F32),
        ] + w_scratch,
        compiler_params=pltpu.CompilerParams(
            dimension_semantics=("arbitrary",), vmem_limit_bytes=VMEM_LIMIT),
        name="layer",
    )(x, x, *consts, *state_ops)
    y, s_new, c_new, n_new, m_e, conv_new = outs
    return (y, s_new.reshape(batch, GLA_HEADS, GLA_DK, GLA_DV), c_new,
            n_new.reshape(batch, M_HEADS, M_DH), m_e[:, :, 0, 0], conv_new)


def _regroup_kernel(wg_ref, wm_ref, glr_ref, mif_ref, o_ref, osm_ref, *, n_gla):
    j = pl.program_id(0)

    @pl.when(j < n_gla)
    def _():
        o_ref[...] = wg_ref[...].T.astype(BF16)

    @pl.when(j >= n_gla)
    def _():
        o_ref[...] = wm_ref[...].T.astype(BF16)

    @pl.when(j == 0)
    def _():
        d = wg_ref.shape[1]
        small = jnp.concatenate(
            [glr_ref[...], mif_ref[...], jnp.zeros((LANES - GLA_RANK - 2 * M_HEADS, d), F32)], axis=0)
        osm_ref[...] = small.T.astype(BF16)


def _regroup_w_in(w):
    d = w.shape[0]
    wt = w.T
    o_glr = OFF_MQ
    o_m = o_glr + GLA_RANK
    o_if = o_m + 4 * M_W
    nslab = OFF_SMALL // INPROJ_SLAB
    n_gla = OFF_MQ // INPROJ_SLAB
    rows = lambda n: (pl.Element(n), pl.Element(d))
    return pl.pallas_call(
        functools.partial(_regroup_kernel, n_gla=n_gla),
        grid=(nslab,),
        in_specs=[
            pl.BlockSpec(rows(INPROJ_SLAB), lambda j: (jnp.minimum(j, n_gla - 1) * INPROJ_SLAB, 0)),
            pl.BlockSpec(rows(INPROJ_SLAB), lambda j: (
                (o_m // GLA_RANK + jnp.maximum(j - n_gla, 0) * (INPROJ_SLAB // GLA_RANK)) * GLA_RANK, 0)),
            pl.BlockSpec(rows(GLA_RANK), lambda j: (o_glr, 0)),
            pl.BlockSpec(rows(2 * M_HEADS), lambda j: (o_if, 0)),
        ],
        out_specs=[
            pl.BlockSpec((d, INPROJ_SLAB), lambda j: (0, j)),
            pl.BlockSpec((d, LANES), lambda j: (0, 0)),
        ],
        out_shape=[
            jax.ShapeDtypeStruct((d, OFF_SMALL), BF16),
            jax.ShapeDtypeStruct((d, LANES), BF16),
        ],
        compiler_params=pltpu.CompilerParams(dimension_semantics=("arbitrary",)),
        name="regroup_w_in",
    )(wt, wt, wt, wt)


def kernel(x_prompt, x_sample, state_gla, state_mlstm_C, state_mlstm_n, state_mlstm_m, cache_ffn_conv,
           norm1, w_in, w_gla_a, b_gla_a, gla_norm, b_m_i, b_m_f, m_norm, w_out, norm2, w_g, w_u,
           conv_w, conv_b, w_d, norm_f):
    assert w_in.shape[0] == 1, "single-layer stack"
    w_in_r, w_in_sm = _regroup_w_in(w_in[0])
    wa_pad = jnp.concatenate(
        [w_gla_a[0], jnp.zeros((LANES - GLA_RANK, GLA_QK), w_gla_a.dtype)], axis=0).astype(BF16)
    b_small = jnp.concatenate(
        [jnp.zeros((GLA_RANK,), F32), b_m_i[0], b_m_f[0],
         jnp.zeros((LANES - GLA_RANK - 2 * M_HEADS,), F32)])[None]
    wts = dict(
        norm1=norm1, w_in_r=w_in_r, w_in_sm=w_in_sm, wa_pad=wa_pad, b_gla_a=b_gla_a, gla_norm=gla_norm,
        b_small=b_small, m_norm=m_norm, w_out=w_out[0].astype(BF16), norm2=norm2,
        w_g=w_g[0].astype(BF16), w_u=w_u[0].astype(BF16), conv_w=conv_w[0], conv_b=conv_b,
        w_d=w_d[0].astype(BF16), norm_f=norm_f[None])

    outs_p = _layer(x_prompt, None, wts)
    outs_s = _layer(x_sample, (state_gla[0], state_mlstm_C[0], state_mlstm_n[0], state_mlstm_m[0],
                               cache_ffn_conv[0]), wts)
    y_p, *st_p = outs_p
    y_s, *st_s = outs_s
    return (y_p, y_s) + tuple(s[None] for s in st_p) + tuple(s[None] for s in st_s)
```
